```python
import jax, jax.numpy as jnp
from jax import lax
import numpy as np

D_MODEL = 1024
BATCH = 8
SEQ = 2048
DEPTH = 1

GRID_W = 64
CTX_LEN = 256
D_MIX = D_MODEL
D_SGU = D_MIX // 2
SGU_HEADS = 4
SGU_HEAD_DIM = D_SGU // SGU_HEADS
CHUNK = 128
D_LRU = D_MIX - D_SGU
LRU_HEADS = 8
LRU_HEAD_DIM = D_LRU // LRU_HEADS
CONV_W = 4
CONV_PAD = (1, 2)
RG_C = 8.0
N_EXPERTS = 16
EC_FACTOR = 2
D_EXPERT = 2048
N_MOD = 6
D_IN = 2 * D_SGU + 2 * D_LRU
EPS = 1e-6

kernel_name = "hybrid_sgu_rglru_ecmoe_dit"


def rmsnorm(x, g):
    xf = x.astype(jnp.float32)
    y = xf * lax.rsqrt(jnp.mean(xf * xf, axis=-1, keepdims=True) + EPS)
    return (y * g.astype(jnp.float32)).astype(x.dtype)


def spatial_gating(u, v, g, w_s, b_s):
    bn, L, _ = u.shape
    u = jax.nn.gelu(u)
    vh = jax.nn.gelu(v).reshape(bn, L, SGU_HEADS, SGU_HEAD_DIM)
    vh = rmsnorm(vh, g.reshape(SGU_HEADS, SGU_HEAD_DIM))
    vc = vh.reshape(bn, L // CHUNK, CHUNK, SGU_HEADS, SGU_HEAD_DIM)
    s = jnp.einsum('hpq,bnqhc->bnphc', w_s, vc) + jnp.swapaxes(b_s, 0, 1)[None, None, :, :, None]
    return u * s.reshape(bn, L, D_SGU)


def depthwise_conv(x, w, b):
    ch = x.shape[-1]
    y = lax.conv_general_dilated(x, w[:, None, :].astype(x.dtype), window_strides=(1,),
                                 padding=[CONV_PAD], dimension_numbers=('NWC', 'WIO', 'NWC'),
                                 feature_group_count=ch)
    return y + b


def rglru_coeffs(xb, wa, ba, wi, bi, lam):
    bn, L, _ = xb.shape
    xh = xb.reshape(bn, L, LRU_HEADS, LRU_HEAD_DIM)
    r = jax.nn.sigmoid(jnp.einsum('blhi,hij->blhj', xh, wa).reshape(bn, L, D_LRU) + ba)
    i = jax.nn.sigmoid(jnp.einsum('blhi,hij->blhj', xh, wi).reshape(bn, L, D_LRU) + bi)
    log_a = RG_C * r.astype(jnp.float32) * jax.nn.log_sigmoid(lam.astype(jnp.float32))
    a = jnp.exp(log_a)
    drive = jnp.sqrt(-jnp.expm1(2.0 * log_a)) * (i * xb).astype(jnp.float32)
    return a, drive


def linear_scan(a, b, h0):
    if h0 is not None:
        b = b.at[:, 0].add(a[:, 0] * h0)

    def combine(left, right):
        a_l, b_l = left
        a_r, b_r = right
        return a_l * a_r, a_r * b_l + b_r

    _, h = lax.associative_scan(combine, (a, b), axis=1)
    return h


def token_mixer(hx, hc, w_in, sgu_g, sgu_w, sgu_b, conv_w, conv_b,
                rg_wa, rg_ba, rg_wi, rg_bi, rg_lam, w_out, ctx_out):
    bn, n, _ = hx.shape
    rows = n // GRID_W
    splits = [D_SGU, 2 * D_SGU, 2 * D_SGU + D_LRU]
    ux, vx, xx, gx = jnp.split(hx @ w_in, splits, axis=-1)
    uc, vc, xc, gc = jnp.split(hc @ w_in, splits, axis=-1)

    cx = depthwise_conv(xx.reshape(bn * rows, GRID_W, D_LRU), conv_w, conv_b).reshape(bn, n, D_LRU)
    cc = depthwise_conv(xc, conv_w, conv_b)

    a_cf, b_cf = rglru_coeffs(cc, rg_wa[0], rg_ba[0], rg_wi[0], rg_bi[0], rg_lam[0])
    h_cf = linear_scan(a_cf, b_cf, None)
    a_xf, b_xf = rglru_coeffs(cx, rg_wa[0], rg_ba[0], rg_wi[0], rg_bi[0], rg_lam[0])
    h_xf = linear_scan(a_xf, b_xf, h_cf[:, -1])

    a_cb, b_cb = rglru_coeffs(jnp.flip(cc, 1), rg_wa[1], rg_ba[1], rg_wi[1], rg_bi[1], rg_lam[1])
    h_cb = linear_scan(a_cb, b_cb, None)
    a_xb, b_xb = rglru_coeffs(jnp.flip(cx, 1), rg_wa[1], rg_ba[1], rg_wi[1], rg_bi[1], rg_lam[1])
    h_xb = jnp.flip(linear_scan(a_xb, b_xb, h_cb[:, -1]), 1)

    rec_x = jax.nn.gelu(gx) * (h_xf + h_xb).astype(hx.dtype)
    yx = jnp.concatenate([spatial_gating(ux, vx, sgu_g, sgu_w, sgu_b), rec_x], axis=-1) @ w_out
    if ctx_out:
        rec_c = jax.nn.gelu(gc) * (h_cf + jnp.flip(h_cb, 1)).astype(hc.dtype)
        yc = jnp.concatenate([spatial_gating(uc, vc, sgu_g, sgu_w, sgu_b), rec_c], axis=-1) @ w_out
    else:
        yc = None
    return yx, yc


def expert_choice_ffn(h, w_r, w1, w3, w2):
    bn, n, _ = h.shape
    cap = EC_FACTOR * n // N_EXPERTS
    aff = jax.nn.softmax((h @ w_r).astype(jnp.float32), axis=-1)
    gate, idx = lax.top_k(jnp.swapaxes(aff, 1, 2), cap)
    bidx = jnp.arange(bn)[:, None, None]
    xe = h[bidx, idx]
    hid = jax.nn.silu(jnp.einsum('becd,edf->becf', xe, w1)) * jnp.einsum('becd,edf->becf', xe, w3)
    ye = jnp.einsum('becf,efd->becd', hid, w2) * gate[..., None].astype(h.dtype)
    return jnp.zeros_like(h).at[bidx, idx].add(ye)


def setup_inputs(seed: int = 0) -> dict:
    key = jax.random.key(seed)
    ks = jax.random.split(key, 32)
    f32 = jnp.float32

    def nrm(k, shape, scale):
        return jax.random.normal(k, shape, f32) * scale

    a0 = jax.random.uniform(ks[20], (DEPTH, 2, D_LRU), f32, 0.9, 0.999)
    p = a0 ** (1.0 / RG_C)
    rg_lam = jnp.log(p) - jnp.log1p(-p)
    return {
        "x": nrm(ks[0], (BATCH, SEQ, D_MODEL), 1.0),
        "c": nrm(ks[1], (BATCH, D_MODEL), 1.0),
        "ctx": nrm(ks[2], (BATCH, CTX_LEN, D_MODEL), 1.0),
        "c_ctx": nrm(ks[3], (D_MODEL,), 1.0),
        "w_mod": nrm(ks[4], (DEPTH, D_MODEL, N_MOD * D_MODEL), 0.5 * D_MODEL ** -0.5),
        "b_mod": nrm(ks[5], (DEPTH, N_MOD * D_MODEL), 0.02),
        "norm1_g": 1.0 + nrm(ks[6], (DEPTH, D_MODEL), 0.02),
        "norm2_g": 1.0 + nrm(ks[7], (DEPTH, D_MODEL), 0.02),
        "w_in": nrm(ks[8], (DEPTH, D_MODEL, D_IN), D_MODEL ** -0.5),
        "sgu_g": 1.0 + nrm(ks[9], (DEPTH, D_SGU), 0.02),
        "sgu_w": nrm(ks[10], (DEPTH, SGU_HEADS, CHUNK, CHUNK), CHUNK ** -0.5),
        "sgu_b": 1.0 + nrm(ks[11], (DEPTH, SGU_HEADS, CHUNK), 0.02),
        "conv_w": nrm(ks[12], (DEPTH, CONV_W, D_LRU), CONV_W ** -0.5),
        "conv_b": nrm(ks[13], (DEPTH, D_LRU), 0.02),
        "rg_wa": nrm(ks[14], (DEPTH, 2, LRU_HEADS, LRU_HEAD_DIM, LRU_HEAD_DIM), LRU_HEAD_DIM ** -0.5),
        "rg_ba": nrm(ks[15], (DEPTH, 2, D_LRU), 0.02),
        "rg_wi": nrm(ks[16], (DEPTH, 2, LRU_HEADS, LRU_HEAD_DIM, LRU_HEAD_DIM), LRU_HEAD_DIM ** -0.5),
        "rg_bi": nrm(ks[17], (DEPTH, 2, D_LRU), 0.02),
        "rg_lam": rg_lam,
        "w_out": nrm(ks[18], (DEPTH, D_MIX, D_MODEL), D_MIX ** -0.5),
        "w_router": nrm(ks[19], (DEPTH, D_MODEL, N_EXPERTS), D_MODEL ** -0.5),
        "w1": nrm(ks[21], (DEPTH, N_EXPERTS, D_MODEL, D_EXPERT), D_MODEL ** -0.5),
        "w3": nrm(ks[22], (DEPTH, N_EXPERTS, D_MODEL, D_EXPERT), D_MODEL ** -0.5),
        "w2": nrm(ks[23], (DEPTH, N_EXPERTS, D_EXPERT, D_MODEL), D_EXPERT ** -0.5),
        "final_g": 1.0 + nrm(ks[24], (D_MODEL,), 0.02),
    }


def reference(x, c, ctx, c_ctx, w_mod, b_mod, norm1_g, norm2_g, w_in, sgu_g, sgu_w, sgu_b,
              conv_w, conv_b, rg_wa, rg_ba, rg_wi, rg_bi, rg_lam, w_out, w_router, w1, w3, w2,
              final_g):
    h_ctx = ctx
    for l in range(DEPTH):
        last = l == DEPTH - 1
        mod_x = jax.nn.silu(c) @ w_mod[l] + b_mod[l]
        mod_c = jax.nn.silu(c_ctx) @ w_mod[l] + b_mod[l]
        sh1x, sc1x, g1x, sh2x, sc2x, g2x = jnp.split(mod_x[:, None, :], N_MOD, axis=-1)
        sh1c, sc1c, g1c, sh2c, sc2c, g2c = jnp.split(mod_c, N_MOD, axis=-1)

        hx = rmsnorm(x, norm1_g[l]) * (1.0 + sc1x) + sh1x
        hc = rmsnorm(h_ctx, norm1_g[l]) * (1.0 + sc1c) + sh1c
        yx, yc = token_mixer(hx, hc, w_in[l], sgu_g[l], sgu_w[l], sgu_b[l], conv_w[l], conv_b[l],
                             rg_wa[l], rg_ba[l], rg_wi[l], rg_bi[l], rg_lam[l], w_out[l],
                             not last)
        x = x + g1x * yx
        hx2 = rmsnorm(x, norm2_g[l]) * (1.0 + sc2x) + sh2x
        x = x + g2x * expert_choice_ffn(hx2, w_router[l], w1[l], w3[l], w2[l])

        if not last:
            h_ctx = h_ctx + g1c * yc
            hc2 = rmsnorm(h_ctx, norm2_g[l]) * (1.0 + sc2c) + sh2c
            h_ctx = h_ctx + g2c * expert_choice_ffn(hc2, w_router[l], w1[l], w3[l], w2[l])
    return rmsnorm(x, final_g)
```

```python
import functools

import jax
import jax.numpy as jnp
from jax import lax
from jax.experimental import pallas as pl
from jax.experimental.pallas import tpu as pltpu

F32 = jnp.float32
BF16 = jnp.bfloat16
HIGHEST = lax.Precision.HIGHEST

D = 1024
B = 8
S = 2048
LC = 256
GRID_W = 64
DS = 512
NH = 4
HD = DS // NH
CHUNK = 128
DL = 512
LRU_HEADS = 8
LRU_HD = DL // LRU_HEADS
E = 16
CAP = 2 * S // E
FF = 2048
N_MOD = 6
EPS = 1e-6
RG_C = 8.0

SUBLANES = 8
LANES = 128
VMEM_LIMIT_V7X = 60000 * 1024

TP = CHUNK
ROWS = TP * B
N_TT = S // TP
SUB_PER_TOK = D // LANES
FN = 512
MC = 512
PAD_F = SUBLANES
PAD_B = 2 * SUBLANES


def _params(sem, vmem_mb):
    return pltpu.CompilerParams(dimension_semantics=sem, vmem_limit_bytes=min(vmem_mb << 20, VMEM_LIMIT_V7X))


def _gelu(x):
    return 0.5 * x * (1.0 + jnp.tanh(0.7978845608028654 * (x + 0.044715 * (x * x * x))))


def _rms_mod(x, g, sc, sh):
    y = x * lax.rsqrt(jnp.mean(x * x, axis=-1, keepdims=True) + EPS)
    return (y * g) * (1.0 + sc) + sh


def _log_sigmoid(x):
    return -(jnp.maximum(-x, 0.0) + jnp.log1p(jnp.exp(-jnp.abs(x))))


def _conv_interleaved(xx, xi_ref, tp, period, cw, cb):
    rows = tp * B
    nslab = DL // LANES
    xi_ref[:, 0:PAD_F, :] = jnp.zeros((nslab, PAD_F, LANES), F32)
    xi_ref[:, PAD_F + rows:PAD_F + rows + PAD_B, :] = jnp.zeros((nslab, PAD_B, LANES), F32)
    for k in range(nslab):
        for b in range(B):
            xi_ref[k, pl.ds(PAD_F + b, tp, stride=B), :] = xx[b * tp:(b + 1) * tp, k * LANES:(k + 1) * LANES]
    pos = lax.broadcasted_iota(jnp.int32, (rows, LANES), 0) >> 3
    pm = pos & (period - 1)
    m0 = pm != 0
    m2 = pm != period - 1
    m3 = pm < period - 2
    outs = []
    for k in range(nslab):
        w = cw[:, k * LANES:(k + 1) * LANES]
        t0 = xi_ref[k, 0:rows, :]
        t1 = xi_ref[k, SUBLANES:SUBLANES + rows, :]
        t2 = xi_ref[k, 2 * SUBLANES:2 * SUBLANES + rows, :]
        t3 = xi_ref[k, 3 * SUBLANES:3 * SUBLANES + rows, :]
        acc = jnp.where(m0, t0, 0.0) * w[0:1] + t1 * w[1:2]
        acc = acc + jnp.where(m2, t2, 0.0) * w[2:3] + jnp.where(m3, t3, 0.0) * w[3:4]
        outs.append(acc + cb[:, k * LANES:(k + 1) * LANES])
    return jnp.concatenate(outs, axis=1)


def _rglru_coeffs(cx, wg_ref, d, ba, bi, lam):
    cxb = cx.astype(BF16)
    half = DL // 2
    a_parts, b_parts = [], []
    for k in range(2):
        sl = slice(half * k, half * (k + 1))
        z = jnp.dot(cxb[:, sl], wg_ref[d, k], preferred_element_type=F32)
        r = jax.nn.sigmoid(z[:, :half] + ba[:, sl])
        i = jax.nn.sigmoid(z[:, half:] + bi[:, sl])
        log_a = (RG_C * r) * _log_sigmoid(lam[:, sl])
        a = jnp.exp(log_a)
        one_minus_a2 = -jnp.tanh(log_a) * (a * a + 1.0)
        a_parts.append(a)
        b_parts.append(jnp.sqrt(one_minus_a2) * (i * cx[:, sl]))
    return jnp.concatenate(a_parts, axis=1), jnp.concatenate(b_parts, axis=1)


def _scan(a_ref, b_ref, h_out_ref, h0, nsteps, reverse):
    def body(j, h):
        p = (nsteps - 1 - j) if reverse else j
        r0 = pl.multiple_of(p * SUBLANES, SUBLANES)
        h = a_ref[pl.ds(r0, SUBLANES), :] * h + b_ref[pl.ds(r0, SUBLANES), :]
        if h_out_ref is not None:
            h_out_ref[pl.ds(r0, SUBLANES), :] = h
        return h

    return lax.fori_loop(0, nsteps, body, h0, unroll=8)


def _mod_kernel(c_ref, w_ref, b_ref, o_ref):
    c = c_ref[...]
    s = c * jax.nn.sigmoid(c)
    o_ref[...] = jnp.dot(s, w_ref[...], precision=HIGHEST, preferred_element_type=F32) + b_ref[...]


def _modulation(cc, w_mod, b_mod):
    nt = 4
    tn = N_MOD * D // nt
    return pl.pallas_call(
        _mod_kernel,
        grid=(nt,),
        in_specs=[pl.BlockSpec((2 * B, D), lambda i: (0, 0)),
                  pl.BlockSpec((D, tn), lambda i: (0, i)),
                  pl.BlockSpec((1, tn), lambda i: (0, i))],
        out_specs=pl.BlockSpec((2 * B, tn), lambda i: (0, i)),
        out_shape=jax.ShapeDtypeStruct((2 * B, N_MOD * D), F32),
        compiler_params=_params(("arbitrary",), 32),
        name="mod",
    )(cc, w_mod, b_mod)


def _ctx_kernel(ctx_ref, mod_ref, g1_ref, win_ref, cw_ref, cb_ref, wg_ref, ba_ref, bi_ref, lam_ref,
                hf_ref, hb_ref, xi_ref, a_s, b_s):
    g1 = g1_ref[...]
    sh = mod_ref[B:B + 1, 0:D]
    sc = mod_ref[B:B + 1, D:2 * D]
    parts = []
    for b in range(B):
        hc = _rms_mod(ctx_ref[b], g1, sc, sh).astype(BF16)
        parts.append(jnp.dot(hc, win_ref[...], preferred_element_type=F32))
    xx = jnp.concatenate(parts, axis=0)
    cx = _conv_interleaved(xx, xi_ref, LC, LC, cw_ref[...], cb_ref[...])
    for d in range(2):
        a, dr = _rglru_coeffs(cx, wg_ref, d, ba_ref[d:d + 1, :], bi_ref[d:d + 1, :], lam_ref[d:d + 1, :])
        a_s[...] = a
        b_s[...] = dr
        h = _scan(a_s, b_s, None, jnp.zeros((B, DL), F32), LC, reverse=(d == 1))
        if d == 0:
            hf_ref[...] = h
        else:
            hb_ref[...] = h


def _context_states(ctx, mod, g1, w_in_x, cw, cb, wg, ba, bi, lam):
    rows = LC * B
    full = lambda shape: pl.BlockSpec(shape, lambda i: (0,) * len(shape))
    return pl.pallas_call(
        _ctx_kernel,
        grid=(1,),
        in_specs=[full((B, LC, D)), full((2 * B, N_MOD * D)), full((1, D)), full((D, DL)), full((4, DL)),
                  full((1, DL)), full((2, 2, DL // 2, DL)), full((2, DL)), full((2, DL)), full((2, DL))],
        out_specs=[full((B, DL)), full((B, DL))],
        out_shape=[jax.ShapeDtypeStruct((B, DL), F32)] * 2,
        scratch_shapes=[pltpu.VMEM((DL // LANES, PAD_F + rows + PAD_B, LANES), F32),
                        pltpu.VMEM((rows, DL), F32), pltpu.VMEM((rows, DL), F32)],
        compiler_params=_params(("arbitrary",), 56),
        name="ctx",
    )(ctx, mod, g1, w_in_x, cw, cb, wg, ba, bi, lam)


def _mix_in_kernel(x_ref, mod_ref, g1_ref, win_ref, sg_ref, sw_ref, sbias_ref, cw_ref, cb_ref, wg_ref,
                   ba_ref, bi_ref, lam_ref, h0_ref,
                   sgu_ref, gg_ref, cx_ref, hf_ref,
                   hbuf, xi_ref, a_s, b_s, carry):
    @pl.when(pl.program_id(0) == 0)
    def _():
        carry[...] = h0_ref[...]

    g1 = g1_ref[...]
    for b in range(B):
        sh = mod_ref[b:b + 1, 0:D]
        sc = mod_ref[b:b + 1, D:2 * D]
        hbuf[b * TP:(b + 1) * TP, :] = _rms_mod(x_ref[b], g1, sc, sh).astype(BF16)
    z = jnp.dot(hbuf[...], win_ref[...], preferred_element_type=F32)
    ug = _gelu(z[:, 0:DS])
    vg = _gelu(z[:, DS:2 * DS])
    xx = z[:, 2 * DS:2 * DS + DL]
    gg = _gelu(z[:, 2 * DS + DL:])
    for b in range(B):
        gg_ref[b] = gg[b * TP:(b + 1) * TP, :].astype(BF16)

    for h in range(NH):
        hs = slice(h * HD, (h + 1) * HD)
        vh = vg[:, hs]
        vn = (vh * lax.rsqrt(jnp.mean(vh * vh, axis=-1, keepdims=True) + EPS)) * sg_ref[:, hs]
        vnb = vn.astype(BF16)
        for b in range(B):
            rs = slice(b * TP, (b + 1) * TP)
            s = jnp.dot(sw_ref[h], vnb[rs], preferred_element_type=F32) + sbias_ref[:, hs]
            sgu_ref[b, :, hs] = (ug[rs, hs] * s).astype(BF16)

    cx = _conv_interleaved(xx, xi_ref, TP, GRID_W, cw_ref[...], cb_ref[...])
    cx_ref[...] = cx
    a, dr = _rglru_coeffs(cx, wg_ref, 0, ba_ref[0:1, :], bi_ref[0:1, :], lam_ref[0:1, :])
    a_s[...] = a
    b_s[...] = dr
    carry[...] = _scan(a_s, b_s, hf_ref, carry[...], TP, reverse=False)


def _mix_in(x, mod, g1, w_in_b, sg, sw_b, sbias, cw, cb, wg, ba, bi, lam, h0f):
    full = lambda shape: pl.BlockSpec(shape, lambda i: (0,) * len(shape))
    return pl.pallas_call(
        _mix_in_kernel,
        grid=(N_TT,),
        in_specs=[pl.BlockSpec((B, TP, D), lambda i: (0, i, 0)),
                  full((2 * B, N_MOD * D)), full((1, D)), full((D, 2 * DS + 2 * DL)), full((1, DS)),
                  full((NH, CHUNK, CHUNK)), full((CHUNK, DS)), full((4, DL)), full((1, DL)),
                  full((2, 2, DL // 2, DL)), full((2, DL)), full((2, DL)), full((2, DL)), full((B, DL))],
        out_specs=[pl.BlockSpec((B, TP, DS), lambda i: (0, i, 0)),
                   pl.BlockSpec((B, TP, DL), lambda i: (0, i, 0)),
                   pl.BlockSpec((ROWS, DL), lambda i: (i, 0)),
                   pl.BlockSpec((ROWS, DL), lambda i: (i, 0))],
        out_shape=[jax.ShapeDtypeStruct((B, S, DS), BF16), jax.ShapeDtypeStruct((B, S, DL), BF16),
                   jax.ShapeDtypeStruct((S * B, DL), F32), jax.ShapeDtypeStruct((S * B, DL), F32)],
        scratch_shapes=[pltpu.VMEM((ROWS, D), BF16),
                        pltpu.VMEM((DL // LANES, PAD_F + ROWS + PAD_B, LANES), F32),
                        pltpu.VMEM((ROWS, DL), F32), pltpu.VMEM((ROWS, DL), F32),
                        pltpu.VMEM((B, DL), F32)],
        compiler_params=_params(("arbitrary",), 56),
        name="mix_in",
    )(x, mod, g1, w_in_b, sg, sw_b, sbias, cw, cb, wg, ba, bi, lam, h0f)


def _mix_out_kernel(x_ref, mod_ref, cx_ref, hf_ref, sgu_ref, gg_ref, wg_ref, ba_ref, bi_ref, lam_ref, h0_ref,
                    wout_ref, g2_ref, wr_ref,
                    x1_ref, hx2t_ref, lg_ref,
                    a_s, b_s, hb_s, hs_ref, mix_ref, carry):
    @pl.when(pl.program_id(0) == 0)
    def _():
        carry[...] = h0_ref[...]

    a, dr = _rglru_coeffs(cx_ref[...], wg_ref, 1, ba_ref[1:2, :], bi_ref[1:2, :], lam_ref[1:2, :])
    a_s[...] = a
    b_s[...] = dr
    carry[...] = _scan(a_s, b_s, hb_s, carry[...], TP, reverse=True)
    hsum = hf_ref[...] + hb_s[...]
    nslab = DL // LANES
    for k in range(nslab):
        hs_ref[k] = hsum[:, k * LANES:(k + 1) * LANES]
    for b in range(B):
        rs = slice(b * TP, (b + 1) * TP)
        mix_ref[rs, 0:DS] = sgu_ref[b]
        for k in range(nslab):
            hk = hs_ref[k, pl.ds(b, TP, stride=B), :]
            gk = gg_ref[b, :, k * LANES:(k + 1) * LANES].astype(F32)
            mix_ref[rs, DS + k * LANES:DS + (k + 1) * LANES] = (gk * hk).astype(BF16)

    y = jnp.dot(mix_ref[...], wout_ref[...], preferred_element_type=F32)
    g2 = g2_ref[...]
    for b in range(B):
        g1x = mod_ref[b:b + 1, 2 * D:3 * D]
        sh2 = mod_ref[b:b + 1, 3 * D:4 * D]
        sc2 = mod_ref[b:b + 1, 4 * D:5 * D]
        x1 = x_ref[b] + g1x * y[b * TP:(b + 1) * TP, :]
        x1_ref[b] = x1
        hx2 = _rms_mod(x1, g2, sc2, sh2)
        for k in range(SUB_PER_TOK):
            hx2t_ref[b, pl.ds(k, TP, stride=SUB_PER_TOK), :] = hx2[:, k * LANES:(k + 1) * LANES]
        lg_ref[b] = lax.dot_general(wr_ref[...], hx2, (((1,), (1,)), ((), ())),
                                    precision=HIGHEST, preferred_element_type=F32)


def _mix_out(x, mod, cx, hf, sgu, gg, wg, ba, bi, lam, h0b, w_out_b, g2, wr_t):
    full = lambda shape: pl.BlockSpec(shape, lambda i: (0,) * len(shape))
    rev = lambda i: N_TT - 1 - i
    return pl.pallas_call(
        _mix_out_kernel,
        grid=(N_TT,),
        in_specs=[pl.BlockSpec((B, TP, D), lambda i: (0, rev(i), 0)),
                  full((2 * B, N_MOD * D)),
                  pl.BlockSpec((ROWS, DL), lambda i: (rev(i), 0)),
                  pl.BlockSpec((ROWS, DL), lambda i: (rev(i), 0)),
                  pl.BlockSpec((B, TP, DS), lambda i: (0, rev(i), 0)),
                  pl.BlockSpec((B, TP, DL), lambda i: (0, rev(i), 0)),
                  full((2, 2, DL // 2, DL)), full((2, DL)), full((2, DL)), full((2, DL)), full((B, DL)),
                  full((D, D)), full((1, D)), full((E, D))],
        out_specs=[pl.BlockSpec((B, TP, D), lambda i: (0, rev(i), 0)),
                   pl.BlockSpec((B, TP * SUB_PER_TOK, LANES), lambda i: (0, rev(i), 0)),
                   pl.BlockSpec((B, E, TP), lambda i: (0, 0, rev(i)))],
        out_shape=[jax.ShapeDtypeStruct((B, S, D), F32),
                   jax.ShapeDtypeStruct((B, S * SUB_PER_TOK, LANES), F32),
                   jax.ShapeDtypeStruct((B, E, S), F32)],
        scratch_shapes=[pltpu.VMEM((ROWS, DL), F32), pltpu.VMEM((ROWS, DL), F32), pltpu.VMEM((ROWS, DL), F32),
                        pltpu.VMEM((DL // LANES, ROWS, LANES), F32),
                        pltpu.VMEM((ROWS, D), BF16),
                        pltpu.VMEM((B, DL), F32)],
        compiler_params=_params(("arbitrary",), 56),
        name="mix_out",
    )(x, mod, cx, hf, sgu, gg, wg, ba, bi, lam, h0b, w_out_b, g2, wr_t)


def _excl_prefix(mask):
    q = lax.broadcasted_iota(jnp.int32, (LANES, LANES), 0)
    t = lax.broadcasted_iota(jnp.int32, (LANES, LANES), 1)
    upper = jnp.where(q < t, 1.0, 0.0).astype(BF16)
    off = jnp.zeros((mask.shape[0], 1), F32)
    outs = []
    for j in range(mask.shape[1] // LANES):
        blk = jnp.where(mask[:, j * LANES:(j + 1) * LANES], 1.0, 0.0)
        outs.append(jnp.dot(blk.astype(BF16), upper, preferred_element_type=F32) + off)
        off = off + jnp.sum(blk, axis=1, keepdims=True)
    return jnp.concatenate(outs, axis=1)


def _route_kernel(lg_ref, idx_ref, gate_ref, aff_s, cum_s):
    for b in range(B):
        l = lg_ref[b]
        ex = jnp.exp(l - jnp.max(l, axis=0, keepdims=True))
        aff_s[b * E:(b + 1) * E, :] = ex / jnp.sum(ex, axis=0, keepdims=True)
    aff = aff_s[...]
    nrow = B * E

    def bisect(_, lohi):
        lo, hi = lohi
        mid = lo + ((hi - lo + 1) >> 1)
        cnt = jnp.sum(jnp.where(aff >= pltpu.bitcast(mid, F32), 1.0, 0.0), axis=1, keepdims=True)
        ge = cnt >= float(CAP)
        return jnp.where(ge, mid, lo), jnp.where(ge, hi, mid - 1)

    lo0 = jnp.zeros((nrow, 1), jnp.int32)
    hi0 = jnp.full((nrow, 1), 0x7F800000, jnp.int32)
    thr_bits, _ = lax.fori_loop(0, 31, bisect, (lo0, hi0))
    thr = pltpu.bitcast(thr_bits, F32)
    gt = aff > thr
    eq = aff == thr
    need = float(CAP) - jnp.sum(jnp.where(gt, 1.0, 0.0), axis=1, keepdims=True)
    sel = gt | (eq & (_excl_prefix(eq) < need))
    slot1 = _excl_prefix(sel) + 1.0
    cum_s[...] = jnp.where(sel, slot1, 0.0)

    want = (lax.broadcasted_iota(jnp.int32, (CAP, S), 0) + 1).astype(F32)
    tok = lax.broadcasted_iota(jnp.int32, (CAP, S), 1).astype(F32)
    lane = lax.broadcasted_iota(jnp.int32, (CAP, LANES), 1)

    def row_body(r, carry):
        idxm, gm = carry
        hit = cum_s[pl.ds(r, 1), :] == want
        iv = jnp.sum(jnp.where(hit, tok, 0.0), axis=1, keepdims=True)
        gv = jnp.sum(jnp.where(hit, aff_s[pl.ds(r, 1), :], 0.0), axis=1, keepdims=True)
        put = lane == r
        return jnp.where(put, iv, idxm), jnp.where(put, gv, gm)

    z = jnp.zeros((CAP, LANES), F32)
    idxm, gm = lax.fori_loop(0, nrow, row_body, (z, z))
    idx_ref[...] = idxm.T.astype(jnp.int32)
    gate_ref[...] = gm.T


def _route(logits_t):
    full = lambda shape: pl.BlockSpec(shape, lambda i: (0,) * len(shape))
    return pl.pallas_call(
        _route_kernel,
        grid=(1,),
        in_specs=[full((B, E, S))],
        out_specs=[full((B * E, CAP)), full((B * E, CAP))],
        out_shape=[jax.ShapeDtypeStruct((B * E, CAP), jnp.int32), jax.ShapeDtypeStruct((B * E, CAP), F32)],
        scratch_shapes=[pltpu.VMEM((B * E, S), F32), pltpu.VMEM((B * E, S), F32)],
        compiler_params=_params(("arbitrary",), 48),
        name="route",
    )(logits_t)


def _dispatch_kernel(idx_ref, h_ref, xg_ref, xt):
    def body(s, c):
        t = idx_ref[0, 0, s]
        src = pl.multiple_of(t * SUB_PER_TOK, SUB_PER_TOK)
        dst = pl.multiple_of(s * SUB_PER_TOK, SUB_PER_TOK)
        xt[pl.ds(dst, SUB_PER_TOK), :] = h_ref[0, pl.ds(src, SUB_PER_TOK), :]
        return c

    lax.fori_loop(0, CAP, body, 0, unroll=8)
    for k in range(SUB_PER_TOK):
        xg_ref[0, :, k * LANES:(k + 1) * LANES] = xt[pl.ds(k, CAP, stride=SUB_PER_TOK), :].astype(BF16)


def _dispatch(idx3, hx2t):
    return pl.pallas_call(
        _dispatch_kernel,
        grid=(B, E),
        in_specs=[pl.BlockSpec((1, 1, CAP), lambda b, e: (b * E + e, 0, 0), memory_space=pltpu.SMEM),
                  pl.BlockSpec((1, S * SUB_PER_TOK, LANES), lambda b, e: (b, 0, 0))],
        out_specs=pl.BlockSpec((1, CAP, D), lambda b, e: (e, b, 0)),
        out_shape=jax.ShapeDtypeStruct((E, B * CAP, D), BF16),
        scratch_shapes=[pltpu.VMEM((CAP * SUB_PER_TOK, LANES), F32)],
        compiler_params=_params(("arbitrary", "arbitrary"), 32),
        name="dispatch",
    )(idx3, hx2t)


def _moe_kernel(x_ref, w1_ref, w3_ref, w2_ref, y_ref):
    f = pl.program_id(1)
    w1 = w1_ref[0].astype(BF16)
    w3 = w3_ref[0].astype(BF16)
    w2 = w2_ref[0].astype(BF16)
    for c in range(B * CAP // MC):
        rs = slice(c * MC, (c + 1) * MC)
        x = x_ref[0, rs, :]
        h1 = jnp.dot(x, w1, preferred_element_type=F32)
        h3 = jnp.dot(x, w3, preferred_element_type=F32)
        hid = (h1 * jax.nn.sigmoid(h1)) * h3
        part = jnp.dot(hid.astype(BF16), w2, preferred_element_type=F32)

        @pl.when(f == 0)
        def _():
            y_ref[0, rs, :] = part

        @pl.when(f != 0)
        def _():
            y_ref[0, rs, :] = y_ref[0, rs, :] + part


def _moe(xg, w1, w3, w2):
    return pl.pallas_call(
        _moe_kernel,
        grid=(E, FF // FN),
        in_specs=[pl.BlockSpec((1, B * CAP, D), lambda e, f: (e, 0, 0)),
                  pl.BlockSpec((1, D, FN), lambda e, f: (e, 0, f)),
                  pl.BlockSpec((1, D, FN), lambda e, f: (e, 0, f)),
                  pl.BlockSpec((1, FN, D), lambda e, f: (e, f, 0))],
        out_specs=pl.BlockSpec((1, B * CAP, D), lambda e, f: (e, 0, 0)),
        out_shape=jax.ShapeDtypeStruct((E, B * CAP, D), F32),
        compiler_params=_params(("arbitrary", "arbitrary"), 56),
        name="moe",
    )(xg, w1, w3, w2)


def _combine_kernel(idx_ref, gate_ref, y_ref, x1_ref, mod_ref, fg_ref, out_ref, acc, yt):
    b = pl.program_id(0)
    e = pl.program_id(1)

    @pl.when(e == 0)
    def _():
        acc[...] = jnp.zeros(acc.shape, F32)

    for k in range(SUB_PER_TOK):
        yt[pl.ds(k, CAP, stride=SUB_PER_TOK), :] = y_ref[0, :, k * LANES:(k + 1) * LANES]

    def body(s, c):
        t = idx_ref[0, 0, s]
        g = gate_ref[0, 0, s]
        dst = pl.multiple_of(t * SUB_PER_TOK, SUB_PER_TOK)
        src = pl.multiple_of(s * SUB_PER_TOK, SUB_PER_TOK)
        acc[pl.ds(dst, SUB_PER_TOK), :] = acc[pl.ds(dst, SUB_PER_TOK), :] + yt[pl.ds(src, SUB_PER_TOK), :] * g
        return c

    lax.fori_loop(0, CAP, body, 0, unroll=4)

    @pl.when(e == E - 1)
    def _():
        g2x = mod_ref[pl.ds(b, 1), 5 * D:6 * D]
        ssq = jnp.zeros((S, 1), F32)
        for k in range(SUB_PER_TOK):
            ls = slice(k * LANES, (k + 1) * LANES)
            xo = x1_ref[0, :, ls] + g2x[:, ls] * acc[pl.ds(k, S, stride=SUB_PER_TOK), :]
            out_ref[0, :, ls] = xo
            ssq = ssq + jnp.sum(xo * xo, axis=1, keepdims=True)
        inv = lax.rsqrt(ssq * (1.0 / D) + EPS)
        out_ref[0] = (out_ref[0] * inv) * fg_ref[...]


def _combine(idx3, gate3, y, x1, mod, fg):
    return pl.pallas_call(
        _combine_kernel,
        grid=(B, E),
        in_specs=[pl.BlockSpec((1, 1, CAP), lambda b, e: (b * E + e, 0, 0), memory_space=pltpu.SMEM),
                  pl.BlockSpec((1, 1, CAP), lambda b, e: (b * E + e, 0, 0), memory_space=pltpu.SMEM),
                  pl.BlockSpec((1, CAP, D), lambda b, e: (e, b, 0)),
                  pl.BlockSpec((1, S, D), lambda b, e: (b, 0, 0)),
                  pl.BlockSpec((2 * B, N_MOD * D), lambda b, e: (0, 0)),
                  pl.BlockSpec((1, D), lambda b, e: (0, 0))],
        out_specs=pl.BlockSpec((1, S, D), lambda b, e: (b, 0, 0)),
        out_shape=jax.ShapeDtypeStruct((B, S, D), F32),
        scratch_shapes=[pltpu.VMEM((S * SUB_PER_TOK, LANES), F32),
                        pltpu.VMEM((CAP * SUB_PER_TOK, LANES), F32)],
        compiler_params=_params(("arbitrary", "arbitrary"), 56),
        name="combine",
    )(idx3, gate3, y, x1, mod, fg)


def _pack_gate_weights(wa, wi):
    eye = jnp.eye(4, dtype=wa.dtype)

    def bdiag(w4):
        return jnp.einsum('hij,hg->higj', w4, eye).reshape(4 * LRU_HD, 4 * LRU_HD)

    dirs = []
    for d in range(2):
        halves = []
        for k in range(2):
            hs = slice(4 * k, 4 * (k + 1))
            halves.append(jnp.concatenate([bdiag(wa[d, hs]), bdiag(wi[d, hs])], axis=1))
        dirs.append(jnp.stack(halves))
    return jnp.stack(dirs).astype(BF16)


def kernel(x, c, ctx, c_ctx, w_mod, b_mod, norm1_g, norm2_g, w_in, sgu_g, sgu_w, sgu_b, conv_w, conv_b,
           rg_wa, rg_ba, rg_wi, rg_bi, rg_lam, w_out, w_router, w1, w3, w2, final_g):
    assert x.shape == (B, S, D) and ctx.shape == (B, LC, D) and w_mod.shape[0] == 1

    cc = jnp.concatenate([c, c_ctx[None, :], jnp.zeros((B - 1, D), F32)], axis=0)
    mod = _modulation(cc, w_mod[0], b_mod[0][None, :])

    g1 = norm1_g[0][None, :]
    g2 = norm2_g[0][None, :]
    w_in_b = w_in[0].astype(BF16)
    w_in_x = w_in_b[:, 2 * DS:2 * DS + DL]
    wg = _pack_gate_weights(rg_wa[0], rg_wi[0])
    cw = conv_w[0]
    cb = conv_b[0][None, :]
    ba, bi, lam = rg_ba[0], rg_bi[0], rg_lam[0]
    sbias = jnp.repeat(sgu_b[0].T, HD, axis=1)

    h0f, h0b = _context_states(ctx, mod, g1, w_in_x, cw, cb, wg, ba, bi, lam)
    sgu, gg, cx, hf = _mix_in(x, mod, g1, w_in_b, sgu_g[0][None, :], sgu_w[0].astype(BF16), sbias, cw, cb,
                              wg, ba, bi, lam, h0f)
    x1, hx2t, logits_t = _mix_out(x, mod, cx, hf, sgu, gg, wg, ba, bi, lam, h0b, w_out[0].astype(BF16), g2,
                                  w_router[0].T)
    idx, gate = _route(logits_t)
    idx3 = idx.reshape(B * E, 1, CAP)
    gate3 = gate.reshape(B * E, 1, CAP)
    xg = _dispatch(idx3, hx2t)
    y = _moe(xg, w1[0], w3[0], w2[0])
    return _combine(idx3, gate3, y, x1, mod, final_g[None, :])
```

```python
import functools

import jax
import jax.numpy as jnp
from jax import lax
from jax.experimental import pallas as pl
from jax.experimental.pallas import tpu as pltpu

F32 = jnp.float32
BF16 = jnp.bfloat16
HIGHEST = lax.Precision.HIGHEST

D = 1024
B = 8
S = 2048
LC = 256
GRID_W = 64
DS = 512
NH = 4
HD = DS // NH
CHUNK = 128
DL = 512
LRU_HEADS = 8
LRU_HD = DL // LRU_HEADS
E = 16
CAP = 2 * S // E
FF = 2048
N_MOD = 6
EPS = 1e-6
RG_C = 8.0

SUBLANES = 8
LANES = 128
VMEM_LIMIT_V7X = 60000 * 1024

TP = CHUNK
ROWS = TP * B
N_TT = S // TP
SUB_PER_TOK = D // LANES
FN = 512
MC = 512
SCATTER_BATCH = 16
PAD_F = SUBLANES
PAD_B = 2 * SUBLANES


def _params(sem, vmem_mb):
    return pltpu.CompilerParams(dimension_semantics=sem, vmem_limit_bytes=min(vmem_mb << 20, VMEM_LIMIT_V7X))


def _gelu(x):
    return 0.5 * x * (1.0 + jnp.tanh(0.7978845608028654 * (x + 0.044715 * (x * x * x))))


def _rms_mod(x, g, sc, sh):
    y = x * lax.rsqrt(jnp.mean(x * x, axis=-1, keepdims=True) + EPS)
    return (y * g) * (1.0 + sc) + sh


def _log_sigmoid(x):
    return -(jnp.maximum(-x, 0.0) + jnp.log1p(jnp.exp(-jnp.abs(x))))


def _conv_interleaved(xx, xi_ref, tp, period, cw, cb):
    rows = tp * B
    nslab = DL // LANES
    xi_ref[:, 0:PAD_F, :] = jnp.zeros((nslab, PAD_F, LANES), F32)
    xi_ref[:, PAD_F + rows:PAD_F + rows + PAD_B, :] = jnp.zeros((nslab, PAD_B, LANES), F32)
    for k in range(nslab):
        for b in range(B):
            xi_ref[k, pl.ds(PAD_F + b, tp, stride=B), :] = xx[b * tp:(b + 1) * tp, k * LANES:(k + 1) * LANES]
    pos = lax.broadcasted_iota(jnp.int32, (rows, LANES), 0) >> 3
    pm = pos & (period - 1)
    m0 = pm != 0
    m2 = pm != period - 1
    m3 = pm < period - 2
    outs = []
    for k in range(nslab):
        w = cw[:, k * LANES:(k + 1) * LANES]
        t0 = xi_ref[k, 0:rows, :]
        t1 = xi_ref[k, SUBLANES:SUBLANES + rows, :]
        t2 = xi_ref[k, 2 * SUBLANES:2 * SUBLANES + rows, :]
        t3 = xi_ref[k, 3 * SUBLANES:3 * SUBLANES + rows, :]
        acc = jnp.where(m0, t0, 0.0) * w[0:1] + t1 * w[1:2]
        acc = acc + jnp.where(m2, t2, 0.0) * w[2:3] + jnp.where(m3, t3, 0.0) * w[3:4]
        outs.append(acc + cb[:, k * LANES:(k + 1) * LANES])
    return jnp.concatenate(outs, axis=1)


def _rglru_coeffs(cx, wg_ref, d, ba, bi, lam):
    cxb = cx.astype(BF16)
    half = DL // 2
    a_parts, b_parts = [], []
    for k in range(2):
        sl = slice(half * k, half * (k + 1))
        z = jnp.dot(cxb[:, sl], wg_ref[d, k], preferred_element_type=F32)
        r = jax.nn.sigmoid(z[:, :half] + ba[:, sl])
        i = jax.nn.sigmoid(z[:, half:] + bi[:, sl])
        log_a = (RG_C * r) * _log_sigmoid(lam[:, sl])
        a = jnp.exp(log_a)
        one_minus_a2 = -jnp.tanh(log_a) * (a * a + 1.0)
        a_parts.append(a)
        b_parts.append(jnp.sqrt(one_minus_a2) * (i * cx[:, sl]))
    return jnp.concatenate(a_parts, axis=1), jnp.concatenate(b_parts, axis=1)


def _scan(a_ref, b_ref, h_out_ref, h0, nsteps, reverse):
    def body(j, h):
        p = (nsteps - 1 - j) if reverse else j
        r0 = pl.multiple_of(p * SUBLANES, SUBLANES)
        h = a_ref[pl.ds(r0, SUBLANES), :] * h + b_ref[pl.ds(r0, SUBLANES), :]
        if h_out_ref is not None:
            h_out_ref[pl.ds(r0, SUBLANES), :] = h
        return h

    return lax.fori_loop(0, nsteps, body, h0, unroll=8)


def _mod_kernel(c_ref, w_ref, b_ref, o_ref):
    c = c_ref[...]
    s = c * jax.nn.sigmoid(c)
    o_ref[...] = jnp.dot(s, w_ref[...], precision=HIGHEST, preferred_element_type=F32) + b_ref[...]


def _modulation(cc, w_mod, b_mod):
    nt = 4
    tn = N_MOD * D // nt
    return pl.pallas_call(
        _mod_kernel,
        grid=(nt,),
        in_specs=[pl.BlockSpec((2 * B, D), lambda i: (0, 0)),
                  pl.BlockSpec((D, tn), lambda i: (0, i)),
                  pl.BlockSpec((1, tn), lambda i: (0, i))],
        out_specs=pl.BlockSpec((2 * B, tn), lambda i: (0, i)),
        out_shape=jax.ShapeDtypeStruct((2 * B, N_MOD * D), F32),
        compiler_params=_params(("arbitrary",), 32),
        name="mod",
    )(cc, w_mod, b_mod)


def _ctx_kernel(ctx_ref, mod_ref, g1_ref, win_ref, cw_ref, cb_ref, wg_ref, ba_ref, bi_ref, lam_ref,
                hf_ref, hb_ref, xi_ref, a_s, b_s):
    g1 = g1_ref[...]
    sh = mod_ref[B:B + 1, 0:D]
    sc = mod_ref[B:B + 1, D:2 * D]
    parts = []
    for b in range(B):
        hc = _rms_mod(ctx_ref[b], g1, sc, sh).astype(BF16)
        parts.append(jnp.dot(hc, win_ref[...], preferred_element_type=F32))
    xx = jnp.concatenate(parts, axis=0)
    cx = _conv_interleaved(xx, xi_ref, LC, LC, cw_ref[...], cb_ref[...])
    for d in range(2):
        a, dr = _rglru_coeffs(cx, wg_ref, d, ba_ref[d:d + 1, :], bi_ref[d:d + 1, :], lam_ref[d:d + 1, :])
        a_s[...] = a
        b_s[...] = dr
        h = _scan(a_s, b_s, None, jnp.zeros((B, DL), F32), LC, reverse=(d == 1))
        if d == 0:
            hf_ref[...] = h
        else:
            hb_ref[...] = h


def _context_states(ctx, mod, g1, w_in_x, cw, cb, wg, ba, bi, lam):
    rows = LC * B
    full = lambda shape: pl.BlockSpec(shape, lambda i: (0,) * len(shape))
    return pl.pallas_call(
        _ctx_kernel,
        grid=(1,),
        in_specs=[full((B, LC, D)), full((2 * B, N_MOD * D)), full((1, D)), full((D, DL)), full((4, DL)),
                  full((1, DL)), full((2, 2, DL // 2, DL)), full((2, DL)), full((2, DL)), full((2, DL))],
        out_specs=[full((B, DL)), full((B, DL))],
        out_shape=[jax.ShapeDtypeStruct((B, DL), F32)] * 2,
        scratch_shapes=[pltpu.VMEM((DL // LANES, PAD_F + rows + PAD_B, LANES), F32),
                        pltpu.VMEM((rows, DL), F32), pltpu.VMEM((rows, DL), F32)],
        compiler_params=_params(("arbitrary",), 56),
        name="ctx",
    )(ctx, mod, g1, w_in_x, cw, cb, wg, ba, bi, lam)


def _mix_in_kernel(x_ref, mod_ref, g1_ref, win_ref, sg_ref, sw_ref, sbias_ref, cw_ref, cb_ref, wg_ref,
                   ba_ref, bi_ref, lam_ref, h0_ref,
                   sgu_ref, gg_ref, cx_ref, hf_ref,
                   hbuf, xi_ref, a_s, b_s, carry):
    @pl.when(pl.program_id(0) == 0)
    def _():
        carry[...] = h0_ref[...]

    g1 = g1_ref[...]
    for b in range(B):
        sh = mod_ref[b:b + 1, 0:D]
        sc = mod_ref[b:b + 1, D:2 * D]
        hbuf[b * TP:(b + 1) * TP, :] = _rms_mod(x_ref[b], g1, sc, sh).astype(BF16)
    z = jnp.dot(hbuf[...], win_ref[...], preferred_element_type=F32)
    ug = _gelu(z[:, 0:DS])
    vg = _gelu(z[:, DS:2 * DS])
    xx = z[:, 2 * DS:2 * DS + DL]
    gg = _gelu(z[:, 2 * DS + DL:])
    for b in range(B):
        gg_ref[b] = gg[b * TP:(b + 1) * TP, :].astype(BF16)

    for h in range(NH):
        hs = slice(h * HD, (h + 1) * HD)
        vh = vg[:, hs]
        vn = (vh * lax.rsqrt(jnp.mean(vh * vh, axis=-1, keepdims=True) + EPS)) * sg_ref[:, hs]
        vnb = vn.astype(BF16)
        for b in range(B):
            rs = slice(b * TP, (b + 1) * TP)
            s = jnp.dot(sw_ref[h], vnb[rs], preferred_element_type=F32) + sbias_ref[:, hs]
            sgu_ref[b, :, hs] = (ug[rs, hs] * s).astype(BF16)

    cx = _conv_interleaved(xx, xi_ref, TP, GRID_W, cw_ref[...], cb_ref[...])
    cx_ref[...] = cx
    a, dr = _rglru_coeffs(cx, wg_ref, 0, ba_ref[0:1, :], bi_ref[0:1, :], lam_ref[0:1, :])
    a_s[...] = a
    b_s[...] = dr
    carry[...] = _scan(a_s, b_s, hf_ref, carry[...], TP, reverse=False)


def _mix_in(x, mod, g1, w_in_b, sg, sw_b, sbias, cw, cb, wg, ba, bi, lam, h0f):
    full = lambda shape: pl.BlockSpec(shape, lambda i: (0,) * len(shape))
    return pl.pallas_call(
        _mix_in_kernel,
        grid=(N_TT,),
        in_specs=[pl.BlockSpec((B, TP, D), lambda i: (0, i, 0)),
                  full((2 * B, N_MOD * D)), full((1, D)), full((D, 2 * DS + 2 * DL)), full((1, DS)),
                  full((NH, CHUNK, CHUNK)), full((CHUNK, DS)), full((4, DL)), full((1, DL)),
                  full((2, 2, DL // 2, DL)), full((2, DL)), full((2, DL)), full((2, DL)), full((B, DL))],
        out_specs=[pl.BlockSpec((B, TP, DS), lambda i: (0, i, 0)),
                   pl.BlockSpec((B, TP, DL), lambda i: (0, i, 0)),
                   pl.BlockSpec((ROWS, DL), lambda i: (i, 0)),
                   pl.BlockSpec((ROWS, DL), lambda i: (i, 0))],
        out_shape=[jax.ShapeDtypeStruct((B, S, DS), BF16), jax.ShapeDtypeStruct((B, S, DL), BF16),
                   jax.ShapeDtypeStruct((S * B, DL), F32), jax.ShapeDtypeStruct((S * B, DL), F32)],
        scratch_shapes=[pltpu.VMEM((ROWS, D), BF16),
                        pltpu.VMEM((DL // LANES, PAD_F + ROWS + PAD_B, LANES), F32),
                        pltpu.VMEM((ROWS, DL), F32), pltpu.VMEM((ROWS, DL), F32),
                        pltpu.VMEM((B, DL), F32)],
        compiler_params=_params(("arbitrary",), 56),
        name="mix_in",
    )(x, mod, g1, w_in_b, sg, sw_b, sbias, cw, cb, wg, ba, bi, lam, h0f)


def _mix_out_kernel(x_ref, mod_ref, cx_ref, hf_ref, sgu_ref, gg_ref, wg_ref, ba_ref, bi_ref, lam_ref, h0_ref,
                    wout_ref, g2_ref, wr_ref,
                    x1_ref, hx2t_ref, lg_ref,
                    a_s, b_s, hb_s, hs_ref, mix_ref, carry):
    @pl.when(pl.program_id(0) == 0)
    def _():
        carry[...] = h0_ref[...]

    a, dr = _rglru_coeffs(cx_ref[...], wg_ref, 1, ba_ref[1:2, :], bi_ref[1:2, :], lam_ref[1:2, :])
    a_s[...] = a
    b_s[...] = dr
    carry[...] = _scan(a_s, b_s, hb_s, carry[...], TP, reverse=True)
    hsum = hf_ref[...] + hb_s[...]
    nslab = DL // LANES
    for k in range(nslab):
        hs_ref[k] = hsum[:, k * LANES:(k + 1) * LANES]
    for b in range(B):
        rs = slice(b * TP, (b + 1) * TP)
        mix_ref[rs, 0:DS] = sgu_ref[b]
        for k in range(nslab):
            hk = hs_ref[k, pl.ds(b, TP, stride=B), :]
            gk = gg_ref[b, :, k * LANES:(k + 1) * LANES].astype(F32)
            mix_ref[rs, DS + k * LANES:DS + (k + 1) * LANES] = (gk * hk).astype(BF16)

    y = jnp.dot(mix_ref[...], wout_ref[...], preferred_element_type=F32)
    g2 = g2_ref[...]
    for b in range(B):
        g1x = mod_ref[b:b + 1, 2 * D:3 * D]
        sh2 = mod_ref[b:b + 1, 3 * D:4 * D]
        sc2 = mod_ref[b:b + 1, 4 * D:5 * D]
        x1 = x_ref[b] + g1x * y[b * TP:(b + 1) * TP, :]
        x1_ref[b] = x1
        hx2 = _rms_mod(x1, g2, sc2, sh2)
        for k in range(SUB_PER_TOK):
            hx2t_ref[b, pl.ds(k, TP, stride=SUB_PER_TOK), :] = hx2[:, k * LANES:(k + 1) * LANES]
        lg_ref[b] = lax.dot_general(wr_ref[...], hx2, (((1,), (1,)), ((), ())),
                                    precision=HIGHEST, preferred_element_type=F32)


def _mix_out(x, mod, cx, hf, sgu, gg, wg, ba, bi, lam, h0b, w_out_b, g2, wr_t):
    full = lambda shape: pl.BlockSpec(shape, lambda i: (0,) * len(shape))
    rev = lambda i: N_TT - 1 - i
    return pl.pallas_call(
        _mix_out_kernel,
        grid=(N_TT,),
        in_specs=[pl.BlockSpec((B, TP, D), lambda i: (0, rev(i), 0)),
                  full((2 * B, N_MOD * D)),
                  pl.BlockSpec((ROWS, DL), lambda i: (rev(i), 0)),
                  pl.BlockSpec((ROWS, DL), lambda i: (rev(i), 0)),
                  pl.BlockSpec((B, TP, DS), lambda i: (0, rev(i), 0)),
                  pl.BlockSpec((B, TP, DL), lambda i: (0, rev(i), 0)),
                  full((2, 2, DL // 2, DL)), full((2, DL)), full((2, DL)), full((2, DL)), full((B, DL)),
                  full((D, D)), full((1, D)), full((E, D))],
        out_specs=[pl.BlockSpec((B, TP, D), lambda i: (0, rev(i), 0)),
                   pl.BlockSpec((B, TP * SUB_PER_TOK, LANES), lambda i: (0, rev(i), 0)),
                   pl.BlockSpec((B, E, TP), lambda i: (0, 0, rev(i)))],
        out_shape=[jax.ShapeDtypeStruct((B, S, D), F32),
                   jax.ShapeDtypeStruct((B, S * SUB_PER_TOK, LANES), F32),
                   jax.ShapeDtypeStruct((B, E, S), F32)],
        scratch_shapes=[pltpu.VMEM((ROWS, DL), F32), pltpu.VMEM((ROWS, DL), F32), pltpu.VMEM((ROWS, DL), F32),
                        pltpu.VMEM((DL // LANES, ROWS, LANES), F32),
                        pltpu.VMEM((ROWS, D), BF16),
                        pltpu.VMEM((B, DL), F32)],
        compiler_params=_params(("arbitrary",), 56),
        name="mix_out",
    )(x, mod, cx, hf, sgu, gg, wg, ba, bi, lam, h0b, w_out_b, g2, wr_t)


def _excl_prefix(mask):
    q = lax.broadcasted_iota(jnp.int32, (LANES, LANES), 0)
    t = lax.broadcasted_iota(jnp.int32, (LANES, LANES), 1)
    upper = jnp.where(q < t, 1.0, 0.0).astype(BF16)
    off = jnp.zeros((mask.shape[0], 1), F32)
    outs = []
    for j in range(mask.shape[1] // LANES):
        blk = jnp.where(mask[:, j * LANES:(j + 1) * LANES], 1.0, 0.0)
        outs.append(jnp.dot(blk.astype(BF16), upper, preferred_element_type=F32) + off)
        off = off + jnp.sum(blk, axis=1, keepdims=True)
    return jnp.concatenate(outs, axis=1)


def _route_kernel(lg_ref, idx_ref, gate_ref, aff_s, cum_s):
    for b in range(B):
        l = lg_ref[b]
        ex = jnp.exp(l - jnp.max(l, axis=0, keepdims=True))
        aff_s[b * E:(b + 1) * E, :] = ex / jnp.sum(ex, axis=0, keepdims=True)
    aff = aff_s[...]
    nrow = B * E

    def bisect(_, lohi):
        lo, hi = lohi
        mid = lo + ((hi - lo + 1) >> 1)
        cnt = jnp.sum(jnp.where(aff >= pltpu.bitcast(mid, F32), 1.0, 0.0), axis=1, keepdims=True)
        ge = cnt >= float(CAP)
        return jnp.where(ge, mid, lo), jnp.where(ge, hi, mid - 1)

    lo0 = jnp.zeros((nrow, 1), jnp.int32)
    hi0 = jnp.full((nrow, 1), 0x7F800000, jnp.int32)
    thr_bits, _ = lax.fori_loop(0, 31, bisect, (lo0, hi0))
    thr = pltpu.bitcast(thr_bits, F32)
    gt = aff > thr
    eq = aff == thr
    need = float(CAP) - jnp.sum(jnp.where(gt, 1.0, 0.0), axis=1, keepdims=True)
    sel = gt | (eq & (_excl_prefix(eq) < need))
    slot1 = _excl_prefix(sel) + 1.0
    cum_s[...] = jnp.where(sel, slot1, 0.0)

    want = (lax.broadcasted_iota(jnp.int32, (CAP, S), 0) + 1).astype(F32)
    tok = lax.broadcasted_iota(jnp.int32, (CAP, S), 1).astype(F32)
    lane = lax.broadcasted_iota(jnp.int32, (CAP, LANES), 1)

    def row_body(r, carry):
        idxm, gm = carry
        hit = cum_s[pl.ds(r, 1), :] == want
        iv = jnp.sum(jnp.where(hit, tok, 0.0), axis=1, keepdims=True)
        gv = jnp.sum(jnp.where(hit, aff_s[pl.ds(r, 1), :], 0.0), axis=1, keepdims=True)
        put = lane == r
        return jnp.where(put, iv, idxm), jnp.where(put, gv, gm)

    z = jnp.zeros((CAP, LANES), F32)
    idxm, gm = lax.fori_loop(0, nrow, row_body, (z, z))
    idx_ref[...] = (idxm.T * float(SUB_PER_TOK)).astype(jnp.int32)
    gate_ref[...] = gm.T


def _route(logits_t):
    full = lambda shape: pl.BlockSpec(shape, lambda i: (0,) * len(shape))
    return pl.pallas_call(
        _route_kernel,
        grid=(1,),
        in_specs=[full((B, E, S))],
        out_specs=[full((B * E, CAP)), full((B * E, CAP))],
        out_shape=[jax.ShapeDtypeStruct((B * E, CAP), jnp.int32), jax.ShapeDtypeStruct((B * E, CAP), F32)],
        scratch_shapes=[pltpu.VMEM((B * E, S), F32), pltpu.VMEM((B * E, S), F32)],
        compiler_params=_params(("arbitrary",), 48),
        name="route",
    )(logits_t)


def _dispatch_kernel(idx_ref, h_ref, xg_ref, xt):
    for e in range(E):
        def body(s, c):
            src = pl.multiple_of(idx_ref[e, 0, s], SUB_PER_TOK)
            dst = pl.multiple_of(s * SUB_PER_TOK, SUB_PER_TOK)
            xt[pl.ds(dst, SUB_PER_TOK), :] = h_ref[0, pl.ds(src, SUB_PER_TOK), :]
            return c

        lax.fori_loop(0, CAP, body, 0, unroll=8)
        for k in range(SUB_PER_TOK):
            xg_ref[e, :, k * LANES:(k + 1) * LANES] = xt[pl.ds(k, CAP, stride=SUB_PER_TOK), :].astype(BF16)


def _dispatch(idx3, hx2t):
    return pl.pallas_call(
        _dispatch_kernel,
        grid=(B,),
        in_specs=[pl.BlockSpec((E, 1, CAP), lambda b: (b, 0, 0), memory_space=pltpu.SMEM),
                  pl.BlockSpec((1, S * SUB_PER_TOK, LANES), lambda b: (b, 0, 0))],
        out_specs=pl.BlockSpec((E, CAP, D), lambda b: (0, b, 0)),
        out_shape=jax.ShapeDtypeStruct((E, B * CAP, D), BF16),
        scratch_shapes=[pltpu.VMEM((CAP * SUB_PER_TOK, LANES), F32)],
        compiler_params=_params(("arbitrary",), 40),
        name="dispatch",
    )(idx3, hx2t)


def _moe_kernel(x_ref, gate_ref, w1_ref, w3_ref, w2_ref, y_ref, hid_s, w2_s):
    f = pl.program_id(1)
    w1 = w1_ref[0].astype(BF16)
    w3 = w3_ref[0].astype(BF16)
    fs = pl.multiple_of(f * FN, FN)
    w2_s[pl.ds(fs, FN), :] = w2_ref[0].astype(BF16)
    for c in range(B * CAP // MC):
        rs = slice(c * MC, (c + 1) * MC)
        x = x_ref[0, rs, :]
        h1 = jnp.dot(x, w1, preferred_element_type=F32)
        h3 = jnp.dot(x, w3, preferred_element_type=F32)
        hid_s[f, rs, :] = ((h1 * jax.nn.sigmoid(h1)) * h3).astype(BF16)

    @pl.when(f == FF // FN - 1)
    def _():
        for c in range(B * CAP // MC):
            rs = slice(c * MC, (c + 1) * MC)
            hid = jnp.concatenate([hid_s[j, rs, :] for j in range(FF // FN)], axis=1)
            y = jnp.dot(hid, w2_s[...], preferred_element_type=F32) * gate_ref[0, rs, :]
            for k in range(SUB_PER_TOK):
                y_ref[0, pl.ds(c * MC * SUB_PER_TOK + k, MC, stride=SUB_PER_TOK), :] = y[:, k * LANES:(k + 1) * LANES]


def _moe(xg, gate_col, w1, w3, w2):
    return pl.pallas_call(
        _moe_kernel,
        grid=(E, FF // FN),
        in_specs=[pl.BlockSpec((1, B * CAP, D), lambda e, f: (e, 0, 0)),
                  pl.BlockSpec((1, B * CAP, 1), lambda e, f: (e, 0, 0)),
                  pl.BlockSpec((1, D, FN), lambda e, f: (e, 0, f)),
                  pl.BlockSpec((1, D, FN), lambda e, f: (e, 0, f)),
                  pl.BlockSpec((1, FN, D), lambda e, f: (e, f, 0))],
        out_specs=pl.BlockSpec((1, B * CAP * SUB_PER_TOK, LANES), lambda e, f: (e, 0, 0)),
        out_shape=jax.ShapeDtypeStruct((E, B * CAP * SUB_PER_TOK, LANES), F32),
        scratch_shapes=[pltpu.VMEM((FF // FN, B * CAP, FN), BF16), pltpu.VMEM((FF, D), BF16)],
        compiler_params=_params(("arbitrary", "arbitrary"), 58),
        name="moe",
    )(xg, gate_col, w1, w3, w2)


def _combine_kernel(idx_ref, y_ref, x1_ref, mod_ref, fg_ref, out_ref, acc):
    b = pl.program_id(0)
    e = pl.program_id(1)

    @pl.when(e == 0)
    def _():
        acc[...] = jnp.zeros(acc.shape, F32)

    def body(i, c):
        s0 = i * SCATTER_BATCH
        dsts, news = [], []
        for j in range(SCATTER_BATCH):
            dst = pl.multiple_of(idx_ref[0, 0, s0 + j], SUB_PER_TOK)
            src = pl.multiple_of((s0 + j) * SUB_PER_TOK, SUB_PER_TOK)
            dsts.append(dst)
            news.append(acc[pl.ds(dst, SUB_PER_TOK), :] + y_ref[0, pl.ds(src, SUB_PER_TOK), :])
        for dst, new in zip(dsts, news):
            acc[pl.ds(dst, SUB_PER_TOK), :] = new
        return c

    lax.fori_loop(0, CAP // SCATTER_BATCH, body, 0)

    @pl.when(e == E - 1)
    def _():
        g2x = mod_ref[pl.ds(b, 1), 5 * D:6 * D]
        ssq = jnp.zeros((S, 1), F32)
        for k in range(SUB_PER_TOK):
            ls = slice(k * LANES, (k + 1) * LANES)
            xo = x1_ref[0, :, ls] + g2x[:, ls] * acc[pl.ds(k, S, stride=SUB_PER_TOK), :]
            out_ref[0, :, ls] = xo
            ssq = ssq + jnp.sum(xo * xo, axis=1, keepdims=True)
        inv = lax.rsqrt(ssq * (1.0 / D) + EPS)
        out_ref[0] = (out_ref[0] * inv) * fg_ref[...]


def _combine(idx3, y, x1, mod, fg):
    return pl.pallas_call(
        _combine_kernel,
        grid=(B, E),
        in_specs=[pl.BlockSpec((1, 1, CAP), lambda b, e: (b * E + e, 0, 0), memory_space=pltpu.SMEM),
                  pl.BlockSpec((1, CAP * SUB_PER_TOK, LANES), lambda b, e: (e, b, 0)),
                  pl.BlockSpec((1, S, D), lambda b, e: (b, 0, 0)),
                  pl.BlockSpec((2 * B, N_MOD * D), lambda b, e: (0, 0)),
                  pl.BlockSpec((1, D), lambda b, e: (0, 0))],
        out_specs=pl.BlockSpec((1, S, D), lambda b, e: (b, 0, 0)),
        out_shape=jax.ShapeDtypeStruct((B, S, D), F32),
        scratch_shapes=[pltpu.VMEM((S * SUB_PER_TOK, LANES), F32)],
        compiler_params=_params(("arbitrary", "arbitrary"), 56),
        name="combine",
    )(idx3, y, x1, mod, fg)


def _pack_gate_weights(wa, wi):
    eye = jnp.eye(4, dtype=wa.dtype)

    def bdiag(w4):
        return jnp.einsum('hij,hg->higj', w4, eye).reshape(4 * LRU_HD, 4 * LRU_HD)

    dirs = []
    for d in range(2):
        halves = []
        for k in range(2):
            hs = slice(4 * k, 4 * (k + 1))
            halves.append(jnp.concatenate([bdiag(wa[d, hs]), bdiag(wi[d, hs])], axis=1))
        dirs.append(jnp.stack(halves))
    return jnp.stack(dirs).astype(BF16)


def kernel(x, c, ctx, c_ctx, w_mod, b_mod, norm1_g, norm2_g, w_in, sgu_g, sgu_w, sgu_b, conv_w, conv_b,
           rg_wa, rg_ba, rg_wi, rg_bi, rg_lam, w_out, w_router, w1, w3, w2, final_g):
    assert x.shape == (B, S, D) and ctx.shape == (B, LC, D) and w_mod.shape[0] == 1

    cc = jnp.concatenate([c, c_ctx[None, :], jnp.zeros((B - 1, D), F32)], axis=0)
    mod = _modulation(cc, w_mod[0], b_mod[0][None, :])

    g1 = norm1_g[0][None, :]
    g2 = norm2_g[0][None, :]
    w_in_b = w_in[0].astype(BF16)
    w_in_x = w_in_b[:, 2 * DS:2 * DS + DL]
    wg = _pack_gate_weights(rg_wa[0], rg_wi[0])
    cw = conv_w[0]
    cb = conv_b[0][None, :]
    ba, bi, lam = rg_ba[0], rg_bi[0], rg_lam[0]
    sbias = jnp.repeat(sgu_b[0].T, HD, axis=1)

    h0f, h0b = _context_states(ctx, mod, g1, w_in_x, cw, cb, wg, ba, bi, lam)
    sgu, gg, cx, hf = _mix_in(x, mod, g1, w_in_b, sgu_g[0][None, :], sgu_w[0].astype(BF16), sbias, cw, cb,
                              wg, ba, bi, lam, h0f)
    x1, hx2t, logits_t = _mix_out(x, mod, cx, hf, sgu, gg, wg, ba, bi, lam, h0b, w_out[0].astype(BF16), g2,
                                  w_router[0].T)
    idx, gate = _route(logits_t)
    idx3 = idx.reshape(B * E, 1, CAP)
    gate_col = gate.reshape(B, E, CAP).transpose(1, 0, 2).reshape(E, B * CAP, 1)
    xg = _dispatch(idx3, hx2t)
    y = _moe(xg, gate_col, w1[0], w3[0], w2[0])
    return _combine(idx3, y, x1, mod, final_g[None, :])
```

```python
import functools

import jax
import jax.numpy as jnp
from jax import lax
from jax.experimental import pallas as pl
from jax.experimental.pallas import tpu as pltpu

F32 = jnp.float32
BF16 = jnp.bfloat16
HIGHEST = lax.Precision.HIGHEST

D = 1024
B = 8
S = 2048
LC = 256
GRID_W = 64
DS = 512
NH = 4
HD = DS // NH
CHUNK = 128
DL = 512
LRU_HEADS = 8
LRU_HD = DL // LRU_HEADS
E = 16
CAP = 2 * S // E
FF = 2048
N_MOD = 6
EPS = 1e-6
RG_C = 8.0

SUBLANES = 8
LANES = 128
VMEM_LIMIT_V7X = 60000 * 1024

TP = CHUNK
ROWS = TP * B
N_TT = S // TP
SUB_PER_TOK = D // LANES
FN = 512
MC = 512
SCATTER_BATCH = 16
EG = 4
NG = E // EG
FIN_ROWS = 512
PAD_F = SUBLANES
PAD_B = 2 * SUBLANES


def _params(sem, vmem_mb):
    return pltpu.CompilerParams(dimension_semantics=sem, vmem_limit_bytes=min(vmem_mb << 20, VMEM_LIMIT_V7X))


def _gelu(x):
    return 0.5 * x * (1.0 + jnp.tanh(0.7978845608028654 * (x + 0.044715 * (x * x * x))))


def _rms_mod(x, g, sc, sh):
    y = x * lax.rsqrt(jnp.mean(x * x, axis=-1, keepdims=True) + EPS)
    return (y * g) * (1.0 + sc) + sh


def _log_sigmoid(x):
    return -(jnp.maximum(-x, 0.0) + jnp.log1p(jnp.exp(-jnp.abs(x))))


def _conv_interleaved(xx, xi_ref, tp, period, cw, cb):
    rows = tp * B
    nslab = DL // LANES
    xi_ref[:, 0:PAD_F, :] = jnp.zeros((nslab, PAD_F, LANES), F32)
    xi_ref[:, PAD_F + rows:PAD_F + rows + PAD_B, :] = jnp.zeros((nslab, PAD_B, LANES), F32)
    for k in range(nslab):
        for b in range(B):
            xi_ref[k, pl.ds(PAD_F + b, tp, stride=B), :] = xx[b * tp:(b + 1) * tp, k * LANES:(k + 1) * LANES]
    pos = lax.broadcasted_iota(jnp.int32, (rows, LANES), 0) >> 3
    pm = pos & (period - 1)
    m0 = pm != 0
    m2 = pm != period - 1
    m3 = pm < period - 2
    outs = []
    for k in range(nslab):
        w = cw[:, k * LANES:(k + 1) * LANES]
        t0 = xi_ref[k, 0:rows, :]
        t1 = xi_ref[k, SUBLANES:SUBLANES + rows, :]
        t2 = xi_ref[k, 2 * SUBLANES:2 * SUBLANES + rows, :]
        t3 = xi_ref[k, 3 * SUBLANES:3 * SUBLANES + rows, :]
        acc = jnp.where(m0, t0, 0.0) * w[0:1] + t1 * w[1:2]
        acc = acc + jnp.where(m2, t2, 0.0) * w[2:3] + jnp.where(m3, t3, 0.0) * w[3:4]
        outs.append(acc + cb[:, k * LANES:(k + 1) * LANES])
    return jnp.concatenate(outs, axis=1)


def _rglru_coeffs(cx, wg_ref, d, ba, bi, lam):
    cxb = cx.astype(BF16)
    half = DL // 2
    a_parts, b_parts = [], []
    for k in range(2):
        sl = slice(half * k, half * (k + 1))
        z = jnp.dot(cxb[:, sl], wg_ref[d, k], preferred_element_type=F32)
        r = jax.nn.sigmoid(z[:, :half] + ba[:, sl])
        i = jax.nn.sigmoid(z[:, half:] + bi[:, sl])
        log_a = (RG_C * r) * _log_sigmoid(lam[:, sl])
        a = jnp.exp(log_a)
        one_minus_a2 = -jnp.tanh(log_a) * (a * a + 1.0)
        a_parts.append(a)
        b_parts.append(jnp.sqrt(one_minus_a2) * (i * cx[:, sl]))
    return jnp.concatenate(a_parts, axis=1), jnp.concatenate(b_parts, axis=1)


def _scan(a_ref, b_ref, h_out_ref, h0, nsteps, reverse):
    def body(j, h):
        p = (nsteps - 1 - j) if reverse else j
        r0 = pl.multiple_of(p * SUBLANES, SUBLANES)
        h = a_ref[pl.ds(r0, SUBLANES), :] * h + b_ref[pl.ds(r0, SUBLANES), :]
        if h_out_ref is not None:
            h_out_ref[pl.ds(r0, SUBLANES), :] = h
        return h

    return lax.fori_loop(0, nsteps, body, h0, unroll=8)


def _mod_kernel(c_ref, w_ref, b_ref, o_ref):
    c = c_ref[...]
    s = c * jax.nn.sigmoid(c)
    o_ref[...] = jnp.dot(s, w_ref[...], precision=HIGHEST, preferred_element_type=F32) + b_ref[...]


def _modulation(cc, w_mod, b_mod):
    nt = 4
    tn = N_MOD * D // nt
    return pl.pallas_call(
        _mod_kernel,
        grid=(nt,),
        in_specs=[pl.BlockSpec((2 * B, D), lambda i: (0, 0)),
                  pl.BlockSpec((D, tn), lambda i: (0, i)),
                  pl.BlockSpec((1, tn), lambda i: (0, i))],
        out_specs=pl.BlockSpec((2 * B, tn), lambda i: (0, i)),
        out_shape=jax.ShapeDtypeStruct((2 * B, N_MOD * D), F32),
        compiler_params=_params(("arbitrary",), 32),
        name="mod",
    )(cc, w_mod, b_mod)


def _ctx_kernel(ctx_ref, mod_ref, g1_ref, win_ref, cw_ref, cb_ref, wg_ref, ba_ref, bi_ref, lam_ref,
                hf_ref, hb_ref, xi_ref, a_s, b_s):
    g1 = g1_ref[...]
    sh = mod_ref[B:B + 1, 0:D]
    sc = mod_ref[B:B + 1, D:2 * D]
    parts = []
    for b in range(B):
        hc = _rms_mod(ctx_ref[b], g1, sc, sh).astype(BF16)
        parts.append(jnp.dot(hc, win_ref[...], preferred_element_type=F32))
    xx = jnp.concatenate(parts, axis=0)
    cx = _conv_interleaved(xx, xi_ref, LC, LC, cw_ref[...], cb_ref[...])
    for d in range(2):
        a, dr = _rglru_coeffs(cx, wg_ref, d, ba_ref[d:d + 1, :], bi_ref[d:d + 1, :], lam_ref[d:d + 1, :])
        a_s[...] = a
        b_s[...] = dr
        h = _scan(a_s, b_s, None, jnp.zeros((B, DL), F32), LC, reverse=(d == 1))
        if d == 0:
            hf_ref[...] = h
        else:
            hb_ref[...] = h


def _context_states(ctx, mod, g1, w_in_x, cw, cb, wg, ba, bi, lam):
    rows = LC * B
    full = lambda shape: pl.BlockSpec(shape, lambda i: (0,) * len(shape))
    return pl.pallas_call(
        _ctx_kernel,
        grid=(1,),
        in_specs=[full((B, LC, D)), full((2 * B, N_MOD * D)), full((1, D)), full((D, DL)), full((4, DL)),
                  full((1, DL)), full((2, 2, DL // 2, DL)), full((2, DL)), full((2, DL)), full((2, DL))],
        out_specs=[full((B, DL)), full((B, DL))],
        out_shape=[jax.ShapeDtypeStruct((B, DL), F32)] * 2,
        scratch_shapes=[pltpu.VMEM((DL // LANES, PAD_F + rows + PAD_B, LANES), F32),
                        pltpu.VMEM((rows, DL), F32), pltpu.VMEM((rows, DL), F32)],
        compiler_params=_params(("arbitrary",), 56),
        name="ctx",
    )(ctx, mod, g1, w_in_x, cw, cb, wg, ba, bi, lam)


def _mix_in_kernel(x_ref, mod_ref, g1_ref, win_ref, sg_ref, sw_ref, sbias_ref, cw_ref, cb_ref, wg_ref,
                   ba_ref, bi_ref, lam_ref, h0_ref,
                   sgu_ref, gg_ref, cx_ref, hf_ref,
                   hbuf, xi_ref, a_s, b_s, carry):
    @pl.when(pl.program_id(0) == 0)
    def _():
        carry[...] = h0_ref[...]

    g1 = g1_ref[...]
    for b in range(B):
        sh = mod_ref[b:b + 1, 0:D]
        sc = mod_ref[b:b + 1, D:2 * D]
        hbuf[b * TP:(b + 1) * TP, :] = _rms_mod(x_ref[b], g1, sc, sh).astype(BF16)
    z = jnp.dot(hbuf[...], win_ref[...], preferred_element_type=F32)
    ug = _gelu(z[:, 0:DS])
    vg = _gelu(z[:, DS:2 * DS])
    xx = z[:, 2 * DS:2 * DS + DL]
    gg = _gelu(z[:, 2 * DS + DL:])
    for b in range(B):
        gg_ref[b] = gg[b * TP:(b + 1) * TP, :].astype(BF16)

    for h in range(NH):
        hs = slice(h * HD, (h + 1) * HD)
        vh = vg[:, hs]
        vn = (vh * lax.rsqrt(jnp.mean(vh * vh, axis=-1, keepdims=True) + EPS)) * sg_ref[:, hs]
        vnb = vn.astype(BF16)
        for b in range(B):
            rs = slice(b * TP, (b + 1) * TP)
            s = jnp.dot(sw_ref[h], vnb[rs], preferred_element_type=F32) + sbias_ref[:, hs]
            sgu_ref[b, :, hs] = (ug[rs, hs] * s).astype(BF16)

    cx = _conv_interleaved(xx, xi_ref, TP, GRID_W, cw_ref[...], cb_ref[...])
    cx_ref[...] = cx
    a, dr = _rglru_coeffs(cx, wg_ref, 0, ba_ref[0:1, :], bi_ref[0:1, :], lam_ref[0:1, :])
    a_s[...] = a
    b_s[...] = dr
    carry[...] = _scan(a_s, b_s, hf_ref, carry[...], TP, reverse=False)


def _mix_in(x, mod, g1, w_in_b, sg, sw_b, sbias, cw, cb, wg, ba, bi, lam, h0f):
    full = lambda shape: pl.BlockSpec(shape, lambda i: (0,) * len(shape))
    return pl.pallas_call(
        _mix_in_kernel,
        grid=(N_TT,),
        in_specs=[pl.BlockSpec((B, TP, D), lambda i: (0, i, 0)),
                  full((2 * B, N_MOD * D)), full((1, D)), full((D, 2 * DS + 2 * DL)), full((1, DS)),
                  full((NH, CHUNK, CHUNK)), full((CHUNK, DS)), full((4, DL)), full((1, DL)),
                  full((2, 2, DL // 2, DL)), full((2, DL)), full((2, DL)), full((2, DL)), full((B, DL))],
        out_specs=[pl.BlockSpec((B, TP, DS), lambda i: (0, i, 0)),
                   pl.BlockSpec((B, TP, DL), lambda i: (0, i, 0)),
                   pl.BlockSpec((ROWS, DL), lambda i: (i, 0)),
                   pl.BlockSpec((ROWS, DL), lambda i: (i, 0))],
        out_shape=[jax.ShapeDtypeStruct((B, S, DS), BF16), jax.ShapeDtypeStruct((B, S, DL), BF16),
                   jax.ShapeDtypeStruct((S * B, DL), F32), jax.ShapeDtypeStruct((S * B, DL), F32)],
        scratch_shapes=[pltpu.VMEM((ROWS, D), BF16),
                        pltpu.VMEM((DL // LANES, PAD_F + ROWS + PAD_B, LANES), F32),
                        pltpu.VMEM((ROWS, DL), F32), pltpu.VMEM((ROWS, DL), F32),
                        pltpu.VMEM((B, DL), F32)],
        compiler_params=_params(("arbitrary",), 56),
        name="mix_in",
    )(x, mod, g1, w_in_b, sg, sw_b, sbias, cw, cb, wg, ba, bi, lam, h0f)


def _mix_out_kernel(x_ref, mod_ref, cx_ref, hf_ref, sgu_ref, gg_ref, wg_ref, ba_ref, bi_ref, lam_ref, h0_ref,
                    wout_ref, g2_ref, wr_ref,
                    x1_ref, hx2t_ref, lg_ref,
                    a_s, b_s, hb_s, hs_ref, mix_ref, carry):
    @pl.when(pl.program_id(0) == 0)
    def _():
        carry[...] = h0_ref[...]

    a, dr = _rglru_coeffs(cx_ref[...], wg_ref, 1, ba_ref[1:2, :], bi_ref[1:2, :], lam_ref[1:2, :])
    a_s[...] = a
    b_s[...] = dr
    carry[...] = _scan(a_s, b_s, hb_s, carry[...], TP, reverse=True)
    hsum = hf_ref[...] + hb_s[...]
    nslab = DL // LANES
    for k in range(nslab):
        hs_ref[k] = hsum[:, k * LANES:(k + 1) * LANES]
    for b in range(B):
        rs = slice(b * TP, (b + 1) * TP)
        mix_ref[rs, 0:DS] = sgu_ref[b]
        for k in range(nslab):
            hk = hs_ref[k, pl.ds(b, TP, stride=B), :]
            gk = gg_ref[b, :, k * LANES:(k + 1) * LANES].astype(F32)
            mix_ref[rs, DS + k * LANES:DS + (k + 1) * LANES] = (gk * hk).astype(BF16)

    y = jnp.dot(mix_ref[...], wout_ref[...], preferred_element_type=F32)
    g2 = g2_ref[...]
    wr = wr_ref[...]
    wr_hi = wr.astype(BF16)
    wr_lo = (wr - wr_hi.astype(F32)).astype(BF16)
    for b in range(B):
        g1x = mod_ref[b:b + 1, 2 * D:3 * D]
        sh2 = mod_ref[b:b + 1, 3 * D:4 * D]
        sc2 = mod_ref[b:b + 1, 4 * D:5 * D]
        x1 = x_ref[b] + g1x * y[b * TP:(b + 1) * TP, :]
        x1_ref[b] = x1
        hx2 = _rms_mod(x1, g2, sc2, sh2)
        for k in range(SUB_PER_TOK):
            hx2t_ref[b, pl.ds(k, TP, stride=SUB_PER_TOK), :] = hx2[:, k * LANES:(k + 1) * LANES]
        hx_hi = hx2.astype(BF16)
        hx_lo = (hx2 - hx_hi.astype(F32)).astype(BF16)
        nt = (((1,), (1,)), ((), ()))
        lg_ref[b] = (lax.dot_general(wr_hi, hx_hi, nt, preferred_element_type=F32)
                     + (lax.dot_general(wr_hi, hx_lo, nt, preferred_element_type=F32)
                        + lax.dot_general(wr_lo, hx_hi, nt, preferred_element_type=F32)))


def _mix_out(x, mod, cx, hf, sgu, gg, wg, ba, bi, lam, h0b, w_out_b, g2, wr_t):
    full = lambda shape: pl.BlockSpec(shape, lambda i: (0,) * len(shape))
    rev = lambda i: N_TT - 1 - i
    return pl.pallas_call(
        _mix_out_kernel,
        grid=(N_TT,),
        in_specs=[pl.BlockSpec((B, TP, D), lambda i: (0, rev(i), 0)),
                  full((2 * B, N_MOD * D)),
                  pl.BlockSpec((ROWS, DL), lambda i: (rev(i), 0)),
                  pl.BlockSpec((ROWS, DL), lambda i: (rev(i), 0)),
                  pl.BlockSpec((B, TP, DS), lambda i: (0, rev(i), 0)),
                  pl.BlockSpec((B, TP, DL), lambda i: (0, rev(i), 0)),
                  full((2, 2, DL // 2, DL)), full((2, DL)), full((2, DL)), full((2, DL)), full((B, DL)),
                  full((D, D)), full((1, D)), full((E, D))],
        out_specs=[pl.BlockSpec((B, TP, D), lambda i: (0, rev(i), 0)),
                   pl.BlockSpec((B, TP * SUB_PER_TOK, LANES), lambda i: (0, rev(i), 0)),
                   pl.BlockSpec((B, E, TP), lambda i: (0, 0, rev(i)))],
        out_shape=[jax.ShapeDtypeStruct((B, S, D), F32),
                   jax.ShapeDtypeStruct((B, S * SUB_PER_TOK, LANES), F32),
                   jax.ShapeDtypeStruct((B, E, S), F32)],
        scratch_shapes=[pltpu.VMEM((ROWS, DL), F32), pltpu.VMEM((ROWS, DL), F32), pltpu.VMEM((ROWS, DL), F32),
                        pltpu.VMEM((DL // LANES, ROWS, LANES), F32),
                        pltpu.VMEM((ROWS, D), BF16),
                        pltpu.VMEM((B, DL), F32)],
        compiler_params=_params(("arbitrary",), 56),
        name="mix_out",
    )(x, mod, cx, hf, sgu, gg, wg, ba, bi, lam, h0b, w_out_b, g2, wr_t)


def _excl_prefix(mask):
    q = lax.broadcasted_iota(jnp.int32, (LANES, LANES), 0)
    t = lax.broadcasted_iota(jnp.int32, (LANES, LANES), 1)
    upper = jnp.where(q < t, 1.0, 0.0).astype(BF16)
    off = jnp.zeros((mask.shape[0], 1), F32)
    outs = []
    for j in range(mask.shape[1] // LANES):
        blk = jnp.where(mask[:, j * LANES:(j + 1) * LANES], 1.0, 0.0)
        outs.append(jnp.dot(blk.astype(BF16), upper, preferred_element_type=F32) + off)
        off = off + jnp.sum(blk, axis=1, keepdims=True)
    return jnp.concatenate(outs, axis=1)


def _route_kernel(lg_ref, idx_ref, gate_ref, aff_s, cum_s):
    for b in range(B):
        l = lg_ref[b]
        ex = jnp.exp(l - jnp.max(l, axis=0, keepdims=True))
        aff_s[b * E:(b + 1) * E, :] = ex / jnp.sum(ex, axis=0, keepdims=True)
    aff = aff_s[...]
    nrow = B * E

    def bisect(_, lohi):
        lo, hi = lohi
        mid = lo + ((hi - lo + 1) >> 1)
        cnt = jnp.sum(jnp.where(aff >= pltpu.bitcast(mid, F32), 1.0, 0.0), axis=1, keepdims=True)
        ge = cnt >= float(CAP)
        return jnp.where(ge, mid, lo), jnp.where(ge, hi, mid - 1)

    lo0 = jnp.zeros((nrow, 1), jnp.int32)
    hi0 = jnp.full((nrow, 1), 0x7F800000, jnp.int32)
    thr_bits, _ = lax.fori_loop(0, 31, bisect, (lo0, hi0))
    thr = pltpu.bitcast(thr_bits, F32)
    gt = aff > thr
    eq = aff == thr
    need = float(CAP) - jnp.sum(jnp.where(gt, 1.0, 0.0), axis=1, keepdims=True)
    sel = gt | (eq & (_excl_prefix(eq) < need))
    slot1 = _excl_prefix(sel) + 1.0
    cum_s[...] = jnp.where(sel, slot1, 0.0)

    want = (lax.broadcasted_iota(jnp.int32, (CAP, S), 0) + 1).astype(F32)
    tok = lax.broadcasted_iota(jnp.int32, (CAP, S), 1).astype(F32)
    lane = lax.broadcasted_iota(jnp.int32, (CAP, LANES), 1)

    def row_body(r, carry):
        idxm, gm = carry
        hit = cum_s[pl.ds(r, 1), :] == want
        iv = jnp.sum(jnp.where(hit, tok, 0.0), axis=1, keepdims=True)
        gv = jnp.sum(jnp.where(hit, aff_s[pl.ds(r, 1), :], 0.0), axis=1, keepdims=True)
        put = lane == r
        return jnp.where(put, iv, idxm), jnp.where(put, gv, gm)

    z = jnp.zeros((CAP, LANES), F32)
    idxm, gm = lax.fori_loop(0, nrow, row_body, (z, z))
    idx_ref[...] = (idxm.T * float(SUB_PER_TOK)).astype(jnp.int32)
    gate_ref[...] = gm


def _route(logits_t):
    full = lambda shape: pl.BlockSpec(shape, lambda i: (0,) * len(shape))
    return pl.pallas_call(
        _route_kernel,
        grid=(1,),
        in_specs=[full((B, E, S))],
        out_specs=[full((B * E, CAP)), full((CAP, B * E))],
        out_shape=[jax.ShapeDtypeStruct((B * E, CAP), jnp.int32), jax.ShapeDtypeStruct((CAP, B * E), F32)],
        scratch_shapes=[pltpu.VMEM((B * E, S), F32), pltpu.VMEM((B * E, S), F32)],
        compiler_params=_params(("arbitrary",), 48),
        name="route",
    )(logits_t)


def _dispatch_kernel(idx_ref, h_ref, xg_ref, xt):
    for e in range(E):
        def body(s, c):
            src = pl.multiple_of(idx_ref[e * CAP + s], SUB_PER_TOK)
            dst = pl.multiple_of(s * SUB_PER_TOK, SUB_PER_TOK)
            xt[pl.ds(dst, SUB_PER_TOK), :] = h_ref[0, pl.ds(src, SUB_PER_TOK), :]
            return c

        lax.fori_loop(0, CAP, body, 0, unroll=8)
        for k in range(SUB_PER_TOK):
            xg_ref[e, :, k * LANES:(k + 1) * LANES] = xt[pl.ds(k, CAP, stride=SUB_PER_TOK), :].astype(BF16)


def _dispatch(idx1, hx2t):
    return pl.pallas_call(
        _dispatch_kernel,
        grid=(B,),
        in_specs=[pl.BlockSpec((E * CAP,), lambda b: (b,), memory_space=pltpu.SMEM),
                  pl.BlockSpec((1, S * SUB_PER_TOK, LANES), lambda b: (b, 0, 0))],
        out_specs=pl.BlockSpec((E, CAP, D), lambda b: (0, b, 0)),
        out_shape=jax.ShapeDtypeStruct((E, B * CAP, D), BF16),
        scratch_shapes=[pltpu.VMEM((CAP * SUB_PER_TOK, LANES), F32)],
        compiler_params=_params(("arbitrary",), 40),
        name="dispatch",
    )(idx1, hx2t)


def _moe_kernel(x_ref, gate_ref, w1_ref, w3_ref, w2_ref, y_ref, hid_s, w2_s):
    f = pl.program_id(1)
    w1 = w1_ref[0].astype(BF16)
    w3 = w3_ref[0].astype(BF16)
    fs = pl.multiple_of(f * FN, FN)
    w2_s[pl.ds(fs, FN), :] = w2_ref[0].astype(BF16)
    for c in range(B * CAP // MC):
        rs = slice(c * MC, (c + 1) * MC)
        x = x_ref[0, rs, :]
        h1 = jnp.dot(x, w1, preferred_element_type=F32)
        h3 = jnp.dot(x, w3, preferred_element_type=F32)
        hid_s[f, rs, :] = ((h1 * jax.nn.sigmoid(h1)) * h3).astype(BF16)

    @pl.when(f == FF // FN - 1)
    def _():
        gates = gate_ref[...]
        lane = lax.broadcasted_iota(jnp.int32, gates.shape, 1)
        e = pl.program_id(0)
        for c in range(B * CAP // MC):
            rs = slice(c * MC, (c + 1) * MC)
            hid = jnp.concatenate([hid_s[j, rs, :] for j in range(FF // FN)], axis=1)
            y = jnp.dot(hid, w2_s[...], preferred_element_type=F32)
            for bb in range(MC // CAP):
                b = c * (MC // CAP) + bb
                gcol = jnp.sum(jnp.where(lane == b * E + e, gates, 0.0), axis=1, keepdims=True)
                yb = y[bb * CAP:(bb + 1) * CAP, :] * gcol
                for k in range(SUB_PER_TOK):
                    y_ref[0, pl.ds(b * CAP * SUB_PER_TOK + k, CAP, stride=SUB_PER_TOK), :] = yb[:, k * LANES:(k + 1) * LANES]


def _moe(xg, gate_cols, w1, w3, w2):
    return pl.pallas_call(
        _moe_kernel,
        grid=(E, FF // FN),
        in_specs=[pl.BlockSpec((1, B * CAP, D), lambda e, f: (e, 0, 0)),
                  pl.BlockSpec((CAP, B * E), lambda e, f: (0, 0)),
                  pl.BlockSpec((1, D, FN), lambda e, f: (e, 0, f)),
                  pl.BlockSpec((1, D, FN), lambda e, f: (e, 0, f)),
                  pl.BlockSpec((1, FN, D), lambda e, f: (e, f, 0))],
        out_specs=pl.BlockSpec((1, B * CAP * SUB_PER_TOK, LANES), lambda e, f: (e, 0, 0)),
        out_shape=jax.ShapeDtypeStruct((E, B * CAP * SUB_PER_TOK, LANES), F32),
        scratch_shapes=[pltpu.VMEM((FF // FN, B * CAP, FN), BF16), pltpu.VMEM((FF, D), BF16)],
        compiler_params=_params(("arbitrary", "arbitrary"), 58),
        name="moe",
    )(xg, gate_cols, w1, w3, w2)


def _combine_kernel(idx_ref, y_ref, x1_ref, mod_ref, fg_ref, out_ref, acc):
    b = pl.program_id(0)
    g = pl.program_id(1)

    @pl.when(g == 0)
    def _():
        acc[...] = jnp.zeros(acc.shape, F32)

    @pl.when(g < NG)
    def _():
        for el in range(EG):
            def body(i, c):
                s0 = i * SCATTER_BATCH
                dsts, news = [], []
                for j in range(SCATTER_BATCH):
                    dst = pl.multiple_of(idx_ref[el * CAP + s0 + j], SUB_PER_TOK)
                    src = pl.multiple_of((s0 + j) * SUB_PER_TOK, SUB_PER_TOK)
                    dsts.append(dst)
                    news.append(acc[pl.ds(dst, SUB_PER_TOK), :] + y_ref[el, pl.ds(src, SUB_PER_TOK), :])
                for dst, new in zip(dsts, news):
                    acc[pl.ds(dst, SUB_PER_TOK), :] = new
                return c

            lax.fori_loop(0, CAP // SCATTER_BATCH, body, 0)

    @pl.when(g >= NG)
    def _():
        row0 = pl.multiple_of((g - NG) * (FIN_ROWS * SUB_PER_TOK), FIN_ROWS * SUB_PER_TOK)
        g2x = mod_ref[pl.ds(b, 1), 5 * D:6 * D]
        ssq = jnp.zeros((FIN_ROWS, 1), F32)
        for k in range(SUB_PER_TOK):
            ls = slice(k * LANES, (k + 1) * LANES)
            xo = x1_ref[0, :, ls] + g2x[:, ls] * acc[pl.ds(row0 + k, FIN_ROWS, stride=SUB_PER_TOK), :]
            out_ref[0, :, ls] = xo
            ssq = ssq + jnp.sum(xo * xo, axis=1, keepdims=True)
        inv = lax.rsqrt(ssq * (1.0 / D) + EPS)
        out_ref[0] = (out_ref[0] * inv) * fg_ref[...]


def _combine(idx1, y, x1, mod, fg):
    fin = lambda b, g: (b, jnp.maximum(g - NG, 0), 0)
    grp = lambda g: jnp.minimum(g, NG - 1)
    return pl.pallas_call(
        _combine_kernel,
        grid=(B, NG + S // FIN_ROWS),
        in_specs=[pl.BlockSpec((EG * CAP,), lambda b, g: (b * NG + grp(g),), memory_space=pltpu.SMEM),
                  pl.BlockSpec((EG, CAP * SUB_PER_TOK, LANES), lambda b, g: (grp(g), b, 0)),
                  pl.BlockSpec((1, FIN_ROWS, D), fin),
                  pl.BlockSpec((2 * B, N_MOD * D), lambda b, g: (0, 0)),
                  pl.BlockSpec((1, D), lambda b, g: (0, 0))],
        out_specs=pl.BlockSpec((1, FIN_ROWS, D), fin),
        out_shape=jax.ShapeDtypeStruct((B, S, D), F32),
        scratch_shapes=[pltpu.VMEM((S * SUB_PER_TOK, LANES), F32)],
        compiler_params=_params(("arbitrary", "arbitrary"), 40),
        name="combine",
    )(idx1, y, x1, mod, fg)


def _pack_gate_weights(wa, wi):
    eye = jnp.eye(4, dtype=wa.dtype)

    def bdiag(w4):
        return jnp.einsum('hij,hg->higj', w4, eye).reshape(4 * LRU_HD, 4 * LRU_HD)

    dirs = []
    for d in range(2):
        halves = []
        for k in range(2):
            hs = slice(4 * k, 4 * (k + 1))
            halves.append(jnp.concatenate([bdiag(wa[d, hs]), bdiag(wi[d, hs])], axis=1))
        dirs.append(jnp.stack(halves))
    return jnp.stack(dirs).astype(BF16)


def kernel(x, c, ctx, c_ctx, w_mod, b_mod, norm1_g, norm2_g, w_in, sgu_g, sgu_w, sgu_b, conv_w, conv_b,
           rg_wa, rg_ba, rg_wi, rg_bi, rg_lam, w_out, w_router, w1, w3, w2, final_g):
    assert x.shape == (B, S, D) and ctx.shape == (B, LC, D) and w_mod.shape[0] == 1

    cc = jnp.concatenate([c, c_ctx[None, :], jnp.zeros((B - 1, D), F32)], axis=0)
    mod = _modulation(cc, w_mod[0], b_mod[0][None, :])

    g1 = norm1_g[0][None, :]
    g2 = norm2_g[0][None, :]
    w_in_b = w_in[0].astype(BF16)
    w_in_x = w_in_b[:, 2 * DS:2 * DS + DL]
    wg = _pack_gate_weights(rg_wa[0], rg_wi[0])
    cw = conv_w[0]
    cb = conv_b[0][None, :]
    ba, bi, lam = rg_ba[0], rg_bi[0], rg_lam[0]
    sbias = jnp.repeat(sgu_b[0].T, HD, axis=1)

    h0f, h0b = _context_states(ctx, mod, g1, w_in_x, cw, cb, wg, ba, bi, lam)
    sgu, gg, cx, hf = _mix_in(x, mod, g1, w_in_b, sgu_g[0][None, :], sgu_w[0].astype(BF16), sbias, cw, cb,
                              wg, ba, bi, lam, h0f)
    x1, hx2t, logits_t = _mix_out(x, mod, cx, hf, sgu, gg, wg, ba, bi, lam, h0b, w_out[0].astype(BF16), g2,
                                  w_router[0].T)
    idx, gate = _route(logits_t)
    idx1 = idx.reshape(B * E * CAP)
    xg = _dispatch(idx1, hx2t)
    y = _moe(xg, gate, w1[0], w3[0], w2[0])
    return _combine(idx1, y, x1, mod, final_g[None, :])
```

```python
import functools

import jax
import jax.numpy as jnp
from jax import lax
from jax.experimental import pallas as pl
from jax.experimental.pallas import tpu as pltpu

F32 = jnp.float32
BF16 = jnp.bfloat16
HIGHEST = lax.Precision.HIGHEST

D = 1024
B = 8
S = 2048
LC = 256
GRID_W = 64
DS = 512
NH = 4
HD = DS // NH
CHUNK = 128
DL = 512
LRU_HEADS = 8
LRU_HD = DL // LRU_HEADS
E = 16
CAP = 2 * S // E
FF = 2048
N_MOD = 6
EPS = 1e-6
RG_C = 8.0

SUBLANES = 8
LANES = 128
VMEM_LIMIT_V7X = 60000 * 1024

TP = CHUNK
ROWS = TP * B
N_TT = S // TP
SUB_PER_TOK = D // LANES
FN = 512
MC = 512
GATHER_BATCH = 16
SCATTER_BATCH = 16
EG = 8
NG = E // EG
FIN_ROWS = 1024
PAD_F = SUBLANES
PAD_B = 2 * SUBLANES


def _params(sem, vmem_mb):
    return pltpu.CompilerParams(dimension_semantics=sem, vmem_limit_bytes=min(vmem_mb << 20, VMEM_LIMIT_V7X))


GELU_C1 = 0.7978845608028654
GELU_C2 = GELU_C1 * 0.044715


def _gelu(x):
    half = 0.5 * x
    return half + half * jnp.tanh(x * (GELU_C1 + GELU_C2 * (x * x)))


def _rms_mod(x, gs, sh):
    return (x * lax.rsqrt(jnp.mean(x * x, axis=-1, keepdims=True) + EPS)) * gs + sh


def _log_sigmoid(x):
    return -(jnp.maximum(-x, 0.0) + jnp.log1p(jnp.exp(-jnp.abs(x))))


def _interleave(xx, xi_ref, pos0, npos):
    for k in range(DL // LANES):
        for b in range(B):
            xi_ref[k, pl.ds(PAD_F + pos0 * B + b, npos, stride=B), :] = xx[b][:, k * LANES:(k + 1) * LANES]


def _conv(xi_ref, pos0, npos, period, cw, cb):
    rows = npos * B
    r0 = pos0 * B
    pos = (lax.broadcasted_iota(jnp.int32, (rows, LANES), 0) >> 3) + pos0
    pm = pos & (period - 1)
    m0 = pm != 0
    m2 = pm != period - 1
    m3 = pm < period - 2
    outs = []
    for k in range(DL // LANES):
        w = cw[:, k * LANES:(k + 1) * LANES]
        t0 = xi_ref[k, r0:r0 + rows, :]
        t1 = xi_ref[k, r0 + SUBLANES:r0 + SUBLANES + rows, :]
        t2 = xi_ref[k, r0 + 2 * SUBLANES:r0 + 2 * SUBLANES + rows, :]
        t3 = xi_ref[k, r0 + 3 * SUBLANES:r0 + 3 * SUBLANES + rows, :]
        acc = jnp.where(m0, t0, 0.0) * w[0:1] + t1 * w[1:2]
        acc = acc + jnp.where(m2, t2, 0.0) * w[2:3] + jnp.where(m3, t3, 0.0) * w[3:4]
        outs.append(acc + cb[:, k * LANES:(k + 1) * LANES])
    return jnp.concatenate(outs, axis=1)


def _scan_steps(a, b, h, out_ref, row0, nsteps, reverse):
    for p in (range(nsteps - 1, -1, -1) if reverse else range(nsteps)):
        rs = slice(p * SUBLANES, (p + 1) * SUBLANES)
        h = a[rs, :] * h + b[rs, :]
        out_ref[row0 + p * SUBLANES:row0 + (p + 1) * SUBLANES, :] = h
    return h


def _rglru_coeffs(cx, wg_ref, d, ba, bi, lam):
    cxb = cx.astype(BF16)
    half = DL // 2
    a_parts, b_parts = [], []
    for k in range(2):
        sl = slice(half * k, half * (k + 1))
        z = jnp.dot(cxb[:, sl], wg_ref[d, k], preferred_element_type=F32)
        r = jax.nn.sigmoid(z[:, :half] + ba[:, sl])
        i = jax.nn.sigmoid(z[:, half:] + bi[:, sl])
        log_a = (RG_C * r) * _log_sigmoid(lam[:, sl])
        a = jnp.exp(log_a)
        one_minus_a2 = -jnp.tanh(log_a) * (a * a + 1.0)
        a_parts.append(a)
        b_parts.append(jnp.sqrt(one_minus_a2) * (i * cx[:, sl]))
    return jnp.concatenate(a_parts, axis=1), jnp.concatenate(b_parts, axis=1)


def _scan(a_ref, b_ref, h_out_ref, h0, nsteps, reverse):
    def body(j, h):
        p = (nsteps - 1 - j) if reverse else j
        r0 = pl.multiple_of(p * SUBLANES, SUBLANES)
        h = a_ref[pl.ds(r0, SUBLANES), :] * h + b_ref[pl.ds(r0, SUBLANES), :]
        if h_out_ref is not None:
            h_out_ref[pl.ds(r0, SUBLANES), :] = h
        return h

    return lax.fori_loop(0, nsteps, body, h0, unroll=8)


def _mod_kernel(c_ref, w_ref, b_ref, o_ref):
    c = c_ref[...]
    s = c * jax.nn.sigmoid(c)
    o_ref[...] = jnp.dot(s, w_ref[...], precision=HIGHEST, preferred_element_type=F32) + b_ref[...]


def _modulation(cc, w_mod, b_mod):
    nt = 4
    tn = N_MOD * D // nt
    return pl.pallas_call(
        _mod_kernel,
        grid=(nt,),
        in_specs=[pl.BlockSpec((2 * B, D), lambda i: (0, 0)),
                  pl.BlockSpec((D, tn), lambda i: (0, i)),
                  pl.BlockSpec((1, tn), lambda i: (0, i))],
        out_specs=pl.BlockSpec((2 * B, tn), lambda i: (0, i)),
        out_shape=jax.ShapeDtypeStruct((2 * B, N_MOD * D), F32),
        compiler_params=_params(("arbitrary",), 32),
        name="mod",
    )(cc, w_mod, b_mod)


def _ctx_kernel(ctx_ref, mod_ref, g1_ref, win_ref, cw_ref, cb_ref, wg_ref, ba_ref, bi_ref, lam_ref,
                hf_ref, hb_ref, xi_ref, a_s, b_s):
    sh = mod_ref[B:B + 1, 0:D]
    gs = g1_ref[...] * (1.0 + mod_ref[B:B + 1, D:2 * D])
    parts = []
    for b in range(B):
        hc = _rms_mod(ctx_ref[b], gs, sh).astype(BF16)
        parts.append(jnp.dot(hc, win_ref[...], preferred_element_type=F32))
    nslab = DL // LANES
    xi_ref[:, 0:PAD_F, :] = jnp.zeros((nslab, PAD_F, LANES), F32)
    xi_ref[:, PAD_F + LC * B:PAD_F + LC * B + PAD_B, :] = jnp.zeros((nslab, PAD_B, LANES), F32)
    _interleave(parts, xi_ref, 0, LC)
    cx = _conv(xi_ref, 0, LC, LC, cw_ref[...], cb_ref[...])
    for d in range(2):
        a, dr = _rglru_coeffs(cx, wg_ref, d, ba_ref[d:d + 1, :], bi_ref[d:d + 1, :], lam_ref[d:d + 1, :])
        a_s[...] = a
        b_s[...] = dr
        h = _scan(a_s, b_s, None, jnp.zeros((B, DL), F32), LC, reverse=(d == 1))
        if d == 0:
            hf_ref[...] = h
        else:
            hb_ref[...] = h


def _context_states(ctx, mod, g1, w_in_x, cw, cb, wg, ba, bi, lam):
    rows = LC * B
    full = lambda shape: pl.BlockSpec(shape, lambda i: (0,) * len(shape))
    return pl.pallas_call(
        _ctx_kernel,
        grid=(1,),
        in_specs=[full((B, LC, D)), full((2 * B, N_MOD * D)), full((1, D)), full((D, DL)), full((4, DL)),
                  full((1, DL)), full((2, 2, DL // 2, DL)), full((2, DL)), full((2, DL)), full((2, DL))],
        out_specs=[full((B, DL)), full((B, DL))],
        out_shape=[jax.ShapeDtypeStruct((B, DL), F32)] * 2,
        scratch_shapes=[pltpu.VMEM((DL // LANES, PAD_F + rows + PAD_B, LANES), F32),
                        pltpu.VMEM((rows, DL), F32), pltpu.VMEM((rows, DL), F32)],
        compiler_params=_params(("arbitrary",), 56),
        name="ctx",
    )(ctx, mod, g1, w_in_x, cw, cb, wg, ba, bi, lam)


def _mix_in_kernel(x_ref, mod_ref, g1_ref, win_ref, sg_ref, sw_ref, sbias_ref, cw_ref, cb_ref, wg_ref,
                   ba_ref, bi_ref, lam_ref, h0_ref,
                   sgu_ref, gg_ref, cx_ref, hf_ref,
                   xi_ref, a_s, b_s, carry):
    @pl.when(pl.program_id(0) == 0)
    def _():
        carry[...] = h0_ref[...]
        xi_ref[...] = jnp.zeros(xi_ref.shape, F32)

    g1 = g1_ref[...]
    hx = []
    for b in range(B):
        sh = mod_ref[b:b + 1, 0:D]
        gs = g1 * (1.0 + mod_ref[b:b + 1, D:2 * D])
        hx.append(_rms_mod(x_ref[b], gs, sh).astype(BF16))
    z = jnp.dot(jnp.concatenate(hx, axis=0), win_ref[...], preferred_element_type=F32)
    ug = _gelu(z[:, 0:DS])
    vg = _gelu(z[:, DS:2 * DS])
    gg = _gelu(z[:, 2 * DS + DL:])
    xx = []
    for b in range(B):
        gg_ref[b] = gg[b * TP:(b + 1) * TP, :].astype(BF16)
        xx.append(z[b * TP:(b + 1) * TP, 2 * DS:2 * DS + DL])

    for hd in range(NH):
        hs = slice(hd * HD, (hd + 1) * HD)
        vh = vg[:, hs]
        vn = (vh * lax.rsqrt(jnp.mean(vh * vh, axis=-1, keepdims=True) + EPS)) * sg_ref[:, hs]
        vnb = vn.astype(BF16)
        for b in range(B):
            rs = slice(b * TP, (b + 1) * TP)
            s = jnp.dot(sw_ref[hd], vnb[rs], preferred_element_type=F32) + sbias_ref[:, hs]
            sgu_ref[b, :, hs] = (ug[rs, hs] * s).astype(BF16)

    _interleave(xx, xi_ref, 0, TP)
    cx = _conv(xi_ref, 0, TP, GRID_W, cw_ref[...], cb_ref[...])
    cx_ref[...] = cx
    a, dr = _rglru_coeffs(cx, wg_ref, 0, ba_ref[0:1, :], bi_ref[0:1, :], lam_ref[0:1, :])
    a_s[...] = a
    b_s[...] = dr
    carry[...] = _scan(a_s, b_s, hf_ref, carry[...], TP, reverse=False)


def _mix_in(x, mod, g1, w_in_b, sg, sw_b, sbias, cw, cb, wg, ba, bi, lam, h0f):
    full = lambda shape: pl.BlockSpec(shape, lambda i: (0,) * len(shape))
    return pl.pallas_call(
        _mix_in_kernel,
        grid=(N_TT,),
        in_specs=[pl.BlockSpec((B, TP, D), lambda i: (0, i, 0)),
                  full((2 * B, N_MOD * D)), full((1, D)), full((D, 2 * DS + 2 * DL)), full((1, DS)),
                  full((NH, CHUNK, CHUNK)), full((CHUNK, DS)), full((4, DL)), full((1, DL)),
                  full((2, 2, DL // 2, DL)), full((2, DL)), full((2, DL)), full((2, DL)), full((B, DL))],
        out_specs=[pl.BlockSpec((B, TP, DS), lambda i: (0, i, 0)),
                   pl.BlockSpec((B, TP, DL), lambda i: (0, i, 0)),
                   pl.BlockSpec((ROWS, DL), lambda i: (i, 0)),
                   pl.BlockSpec((ROWS, DL), lambda i: (i, 0))],
        out_shape=[jax.ShapeDtypeStruct((B, S, DS), BF16), jax.ShapeDtypeStruct((B, S, DL), BF16),
                   jax.ShapeDtypeStruct((S * B, DL), F32), jax.ShapeDtypeStruct((S * B, DL), F32)],
        scratch_shapes=[pltpu.VMEM((DL // LANES, PAD_F + ROWS + PAD_B, LANES), F32),
                        pltpu.VMEM((ROWS, DL), F32), pltpu.VMEM((ROWS, DL), F32),
                        pltpu.VMEM((B, DL), F32)],
        compiler_params=_params(("arbitrary",), 56),
        name="mix_in",
    )(x, mod, g1, w_in_b, sg, sw_b, sbias, cw, cb, wg, ba, bi, lam, h0f)


def _mix_out_kernel(x_ref, mod_ref, cx_ref, hf_ref, sgu_ref, gg_ref, wg_ref, ba_ref, bi_ref, lam_ref, h0_ref,
                    wout_ref, g2_ref, wr_ref,
                    x1_ref, hx2t_ref, lg_ref,
                    a_s, b_s, hb_s, hs_ref, mix_ref, carry):
    @pl.when(pl.program_id(0) == 0)
    def _():
        carry[...] = h0_ref[...]

    a, dr = _rglru_coeffs(cx_ref[...], wg_ref, 1, ba_ref[1:2, :], bi_ref[1:2, :], lam_ref[1:2, :])
    a_s[...] = a
    b_s[...] = dr
    carry[...] = _scan(a_s, b_s, hb_s, carry[...], TP, reverse=True)
    hsum = hf_ref[...] + hb_s[...]
    nslab = DL // LANES
    for k in range(nslab):
        hs_ref[k] = hsum[:, k * LANES:(k + 1) * LANES]
    for b in range(B):
        rs = slice(b * TP, (b + 1) * TP)
        mix_ref[rs, 0:DS] = sgu_ref[b]
        for k in range(nslab):
            hk = hs_ref[k, pl.ds(b, TP, stride=B), :]
            gk = gg_ref[b, :, k * LANES:(k + 1) * LANES].astype(F32)
            mix_ref[rs, DS + k * LANES:DS + (k + 1) * LANES] = (gk * hk).astype(BF16)

    y = jnp.dot(mix_ref[...], wout_ref[...], preferred_element_type=F32)
    g2 = g2_ref[...]
    wr = wr_ref[...]
    wr_hi = wr.astype(BF16)
    wr_lo = (wr - wr_hi.astype(F32)).astype(BF16)
    for b in range(B):
        g1x = mod_ref[b:b + 1, 2 * D:3 * D]
        sh2 = mod_ref[b:b + 1, 3 * D:4 * D]
        gs2 = g2 * (1.0 + mod_ref[b:b + 1, 4 * D:5 * D])
        x1 = x_ref[b] + g1x * y[b * TP:(b + 1) * TP, :]
        x1_ref[b] = x1
        hx2 = _rms_mod(x1, gs2, sh2)
        for k in range(SUB_PER_TOK):
            hx2t_ref[b, pl.ds(k, TP, stride=SUB_PER_TOK), :] = hx2[:, k * LANES:(k + 1) * LANES]
        hx_hi = hx2.astype(BF16)
        hx_lo = (hx2 - hx_hi.astype(F32)).astype(BF16)
        nt = (((1,), (1,)), ((), ()))
        lg_ref[b] = (lax.dot_general(wr_hi, hx_hi, nt, preferred_element_type=F32)
                     + (lax.dot_general(wr_hi, hx_lo, nt, preferred_element_type=F32)
                        + lax.dot_general(wr_lo, hx_hi, nt, preferred_element_type=F32)))


def _mix_out(x, mod, cx, hf, sgu, gg, wg, ba, bi, lam, h0b, w_out_b, g2, wr_t):
    full = lambda shape: pl.BlockSpec(shape, lambda i: (0,) * len(shape))
    rev = lambda i: N_TT - 1 - i
    return pl.pallas_call(
        _mix_out_kernel,
        grid=(N_TT,),
        in_specs=[pl.BlockSpec((B, TP, D), lambda i: (0, rev(i), 0)),
                  full((2 * B, N_MOD * D)),
                  pl.BlockSpec((ROWS, DL), lambda i: (rev(i), 0)),
                  pl.BlockSpec((ROWS, DL), lambda i: (rev(i), 0)),
                  pl.BlockSpec((B, TP, DS), lambda i: (0, rev(i), 0)),
                  pl.BlockSpec((B, TP, DL), lambda i: (0, rev(i), 0)),
                  full((2, 2, DL // 2, DL)), full((2, DL)), full((2, DL)), full((2, DL)), full((B, DL)),
                  full((D, D)), full((1, D)), full((E, D))],
        out_specs=[pl.BlockSpec((B, TP, D), lambda i: (0, rev(i), 0)),
                   pl.BlockSpec((B, TP * SUB_PER_TOK, LANES), lambda i: (0, rev(i), 0)),
                   pl.BlockSpec((B, E, TP), lambda i: (0, 0, rev(i)))],
        out_shape=[jax.ShapeDtypeStruct((B, S, D), F32),
                   jax.ShapeDtypeStruct((B, S * SUB_PER_TOK, LANES), F32),
                   jax.ShapeDtypeStruct((B, E, S), F32)],
        scratch_shapes=[pltpu.VMEM((ROWS, DL), F32), pltpu.VMEM((ROWS, DL), F32), pltpu.VMEM((ROWS, DL), F32),
                        pltpu.VMEM((DL // LANES, ROWS, LANES), F32),
                        pltpu.VMEM((ROWS, D), BF16),
                        pltpu.VMEM((B, DL), F32)],
        compiler_params=_params(("arbitrary",), 56),
        name="mix_out",
    )(x, mod, cx, hf, sgu, gg, wg, ba, bi, lam, h0b, w_out_b, g2, wr_t)


NBLK = S // LANES
BIG = 1.0e9


def _bf16_parts(a):
    hi = a.astype(BF16).astype(F32)
    r1 = a - hi
    mid = r1.astype(BF16).astype(F32)
    lo = (r1 - mid).astype(BF16).astype(F32)
    return hi, mid, lo


def _route_kernel(lg_ref, idx_ref, gate_ref, aff_s, lci_s, offb_s, ahi_s, amid_s, alo_s, offi_s):
    for b in range(B):
        l = lg_ref[b]
        ex = jnp.exp(l - jnp.max(l, axis=0, keepdims=True))
        aff_s[b * E:(b + 1) * E, :] = ex / jnp.sum(ex, axis=0, keepdims=True)
    aff = aff_s[...]
    nrow = B * E

    def bisect(_, lohi):
        lo, hi = lohi
        mid = lo + ((hi - lo + 1) >> 1)
        cnt = jnp.sum(jnp.where(aff >= pltpu.bitcast(mid, F32), 1.0, 0.0), axis=1, keepdims=True)
        ge = cnt >= float(CAP)
        return jnp.where(ge, mid, lo), jnp.where(ge, hi, mid - 1)

    lo0 = jnp.zeros((nrow, 1), jnp.int32)
    hi0 = jnp.full((nrow, 1), 0x7F800000, jnp.int32)
    thr_bits, _ = lax.fori_loop(0, 31, bisect, (lo0, hi0))
    thr = pltpu.bitcast(thr_bits, F32)
    gt = aff > thr
    eq = aff == thr
    need = float(CAP) - jnp.sum(jnp.where(gt, 1.0, 0.0), axis=1, keepdims=True)

    qi = lax.broadcasted_iota(jnp.int32, (LANES, LANES), 0)
    ti = lax.broadcasted_iota(jnp.int32, (LANES, LANES), 1)
    tri = jnp.where(qi <= ti, 1.0, 0.0).astype(BF16)
    blocks = [slice(j * LANES, (j + 1) * LANES) for j in range(NBLK)]

    ties_before = jnp.zeros((nrow, 1), F32)
    sel = []
    for sl in blocks:
        eqb = jnp.where(eq[:, sl], 1.0, 0.0)
        incl = jnp.dot(eqb.astype(BF16), tri, preferred_element_type=F32) + ties_before
        sel.append(gt[:, sl] | (eq[:, sl] & ((incl - eqb) < need)))
        ties_before = incl[:, LANES - 1:LANES]

    lane_sq = lax.broadcasted_iota(jnp.int32, (nrow, LANES), 1)
    off = jnp.zeros((nrow, 1), F32)
    offi = jnp.full((nrow, LANES), BIG, F32)
    for j, sl in enumerate(blocks):
        rows = slice(j * nrow, (j + 1) * nrow)
        lci = jnp.dot(jnp.where(sel[j], 1.0, 0.0).astype(BF16), tri, preferred_element_type=F32)
        lci_s[rows, :] = lci
        offb_s[rows, :] = jnp.broadcast_to(off, (nrow, LANES))
        ahi_s[rows, :], amid_s[rows, :], alo_s[rows, :] = _bf16_parts(aff[:, sl])
        off = off + lci[:, LANES - 1:LANES]
        offi = jnp.where(lane_sq == j, off, offi)
    offi_s[...] = offi

    slot = lax.broadcasted_iota(jnp.int32, (CAP, LANES), 0).astype(F32)
    lane = lax.broadcasted_iota(jnp.int32, (CAP, LANES), 1)
    lane_f = lane.astype(F32)
    zpad = jnp.zeros((LANES - NBLK, 5 * LANES), BF16)

    def row_body(r, carry):
        idxm, gm = carry
        take = lambda ref: ref[pl.ds(r, NBLK, stride=nrow), :]
        table = jnp.concatenate([take(lci_s), take(offb_s), take(ahi_s), take(amid_s), take(alo_s)], axis=1)
        table = jnp.concatenate([table.astype(BF16), zpad], axis=0)
        blk = jnp.sum(jnp.where(offi_s[pl.ds(r, 1), :] <= slot, 1.0, 0.0), axis=1, keepdims=True)
        pick = jnp.where(lane_f == blk, 1.0, 0.0).astype(BF16)
        res = jnp.dot(pick, table, preferred_element_type=F32)
        rank1 = (slot + 1.0) - res[:, LANES:2 * LANES]
        tokl = jnp.sum(jnp.where(res[:, 0:LANES] < rank1, 1.0, 0.0), axis=1, keepdims=True)
        affs = (res[:, 2 * LANES:3 * LANES] + res[:, 3 * LANES:4 * LANES]) + res[:, 4 * LANES:5 * LANES]
        gv = jnp.sum(jnp.where(lane_f == tokl, affs, 0.0), axis=1, keepdims=True)
        iv = blk * float(LANES) + tokl
        put = lane == r
        return jnp.where(put, iv, idxm), jnp.where(put, gv, gm)

    z = jnp.zeros((CAP, LANES), F32)
    idxm, gm = lax.fori_loop(0, nrow, row_body, (z, z), unroll=8)
    idx_ref[...] = (idxm.T * float(SUB_PER_TOK)).astype(jnp.int32)
    gate_ref[...] = gm


def _route(logits_t):
    full = lambda shape: pl.BlockSpec(shape, lambda i: (0,) * len(shape))
    return pl.pallas_call(
        _route_kernel,
        grid=(1,),
        in_specs=[full((B, E, S))],
        out_specs=[full((B * E, CAP)), full((CAP, B * E))],
        out_shape=[jax.ShapeDtypeStruct((B * E, CAP), jnp.int32), jax.ShapeDtypeStruct((CAP, B * E), F32)],
        scratch_shapes=[pltpu.VMEM((B * E, S), F32)] + [pltpu.VMEM((NBLK * B * E, LANES), F32)] * 5
                       + [pltpu.VMEM((B * E, LANES), F32)],
        compiler_params=_params(("arbitrary",), 48),
        name="route",
    )(logits_t)


def _dispatch_kernel(idx_ref, h_ref, xg_ref, xt):
    for e in range(E):
        def ids_at(s0):
            return tuple(idx_ref[e * CAP + s0 + j] for j in range(GATHER_BATCH))

        def body(i, ids):
            s0 = i * GATHER_BATCH
            tiles = [h_ref[0, pl.ds(pl.multiple_of(ids[j], SUB_PER_TOK), SUB_PER_TOK), :]
                     for j in range(GATHER_BATCH)]
            nxt = ids_at(jnp.minimum(s0 + GATHER_BATCH, CAP - GATHER_BATCH))
            for j in range(GATHER_BATCH):
                dst = pl.multiple_of((s0 + j) * SUB_PER_TOK, SUB_PER_TOK)
                xt[pl.ds(dst, SUB_PER_TOK), :] = tiles[j]
            return nxt

        lax.fori_loop(0, CAP // GATHER_BATCH, body, ids_at(0))
        for k in range(SUB_PER_TOK):
            xg_ref[e, :, k * LANES:(k + 1) * LANES] = xt[pl.ds(k, CAP, stride=SUB_PER_TOK), :].astype(BF16)


def _dispatch(idx1, hx2t):
    return pl.pallas_call(
        _dispatch_kernel,
        grid=(B,),
        in_specs=[pl.BlockSpec((E * CAP,), lambda b: (b,), memory_space=pltpu.SMEM),
                  pl.BlockSpec((1, S * SUB_PER_TOK, LANES), lambda b: (b, 0, 0))],
        out_specs=pl.BlockSpec((E, CAP, D), lambda b: (0, b, 0)),
        out_shape=jax.ShapeDtypeStruct((E, B * CAP, D), BF16),
        scratch_shapes=[pltpu.VMEM((CAP * SUB_PER_TOK, LANES), F32)],
        compiler_params=_params(("arbitrary",), 40),
        name="dispatch",
    )(idx1, hx2t)


def _moe_kernel(x_ref, gate_ref, w1_ref, w3_ref, w2_ref, y_ref, hid_s, w2_s):
    f = pl.program_id(1)
    w1 = w1_ref[0].astype(BF16)
    w3 = w3_ref[0].astype(BF16)
    fs = pl.multiple_of(f * FN, FN)
    w2_s[pl.ds(fs, FN), :] = w2_ref[0].astype(BF16)
    for c in range(B * CAP // MC):
        rs = slice(c * MC, (c + 1) * MC)
        x = x_ref[0, rs, :]
        h1 = jnp.dot(x, w1, preferred_element_type=F32)
        h3 = jnp.dot(x, w3, preferred_element_type=F32)
        hid_s[f, rs, :] = ((h1 * jax.nn.sigmoid(h1)) * h3).astype(BF16)

    @pl.when(f == FF // FN - 1)
    def _():
        gates = gate_ref[...]
        lane = lax.broadcasted_iota(jnp.int32, gates.shape, 1)
        e = pl.program_id(0)
        for c in range(B * CAP // MC):
            rs = slice(c * MC, (c + 1) * MC)
            hid = jnp.concatenate([hid_s[j, rs, :] for j in range(FF // FN)], axis=1)
            y = jnp.dot(hid, w2_s[...], preferred_element_type=F32)
            for bb in range(MC // CAP):
                b = c * (MC // CAP) + bb
                gcol = jnp.sum(jnp.where(lane == b * E + e, gates, 0.0), axis=1, keepdims=True)
                y_ref[0, b * CAP:(b + 1) * CAP, :] = (y[bb * CAP:(bb + 1) * CAP, :] * gcol).astype(BF16)


def _moe(xg, gate_cols, w1, w3, w2):
    return pl.pallas_call(
        _moe_kernel,
        grid=(E, FF // FN),
        in_specs=[pl.BlockSpec((1, B * CAP, D), lambda e, f: (e, 0, 0)),
                  pl.BlockSpec((CAP, B * E), lambda e, f: (0, 0)),
                  pl.BlockSpec((1, D, FN), lambda e, f: (e, 0, f)),
                  pl.BlockSpec((1, D, FN), lambda e, f: (e, 0, f)),
                  pl.BlockSpec((1, FN, D), lambda e, f: (e, f, 0))],
        out_specs=pl.BlockSpec((1, B * CAP, D), lambda e, f: (e, 0, 0)),
        out_shape=jax.ShapeDtypeStruct((E, B * CAP, D), BF16),
        scratch_shapes=[pltpu.VMEM((FF // FN, B * CAP, FN), BF16), pltpu.VMEM((FF, D), BF16)],
        compiler_params=_params(("arbitrary", "arbitrary"), 56),
        name="moe",
    )(xg, gate_cols, w1, w3, w2)


def _combine_kernel(idx_ref, y_ref, x1_ref, mod_ref, fg_ref, out_ref, acc, yt):
    b = pl.program_id(0)
    g = pl.program_id(1)

    @pl.when(g == 0)
    def _():
        acc[...] = jnp.zeros(acc.shape, F32)

    @pl.when(g < NG)
    def _():
        for el in range(EG):
            ytb = yt.at[el % 2]
            for k in range(SUB_PER_TOK):
                ytb[pl.ds(k, CAP, stride=SUB_PER_TOK), :] = y_ref[el, :, k * LANES:(k + 1) * LANES].astype(F32)

            def ids_at(s0):
                return tuple(idx_ref[el * CAP + s0 + j] for j in range(SCATTER_BATCH))

            def body(i, ids):
                s0 = i * SCATTER_BATCH
                dsts, news = [], []
                for j in range(SCATTER_BATCH):
                    dst = pl.multiple_of(ids[j], SUB_PER_TOK)
                    src = pl.multiple_of((s0 + j) * SUB_PER_TOK, SUB_PER_TOK)
                    dsts.append(dst)
                    news.append(acc[pl.ds(dst, SUB_PER_TOK), :] + ytb[pl.ds(src, SUB_PER_TOK), :])
                nxt = ids_at(jnp.minimum(s0 + SCATTER_BATCH, CAP - SCATTER_BATCH))
                for dst, new in zip(dsts, news):
                    acc[pl.ds(dst, SUB_PER_TOK), :] = new
                return nxt

            lax.fori_loop(0, CAP // SCATTER_BATCH, body, ids_at(0))

    @pl.when(g >= NG)
    def _():
        row0 = pl.multiple_of((g - NG) * (FIN_ROWS * SUB_PER_TOK), FIN_ROWS * SUB_PER_TOK)
        g2x = mod_ref[pl.ds(b, 1), 5 * D:6 * D]
        ssq = jnp.zeros((FIN_ROWS, 1), F32)
        for k in range(SUB_PER_TOK):
            ls = slice(k * LANES, (k + 1) * LANES)
            xo = x1_ref[0, :, ls] + g2x[:, ls] * acc[pl.ds(row0 + k, FIN_ROWS, stride=SUB_PER_TOK), :]
            out_ref[0, :, ls] = xo
            ssq = ssq + jnp.sum(xo * xo, axis=1, keepdims=True)
        inv = lax.rsqrt(ssq * (1.0 / D) + EPS)
        out_ref[0] = (out_ref[0] * inv) * fg_ref[...]


def _combine(idx1, y, x1, mod, fg):
    fin = lambda b, g: (b, jnp.maximum(g - NG, 0), 0)
    grp = lambda g: jnp.minimum(g, NG - 1)
    return pl.pallas_call(
        _combine_kernel,
        grid=(B, NG + S // FIN_ROWS),
        in_specs=[pl.BlockSpec((EG * CAP,), lambda b, g: (b * NG + grp(g),), memory_space=pltpu.SMEM),
                  pl.BlockSpec((EG, CAP, D), lambda b, g: (grp(g), b, 0)),
                  pl.BlockSpec((1, FIN_ROWS, D), fin),
                  pl.BlockSpec((2 * B, N_MOD * D), lambda b, g: (0, 0)),
                  pl.BlockSpec((1, D), lambda b, g: (0, 0))],
        out_specs=pl.BlockSpec((1, FIN_ROWS, D), fin),
        out_shape=jax.ShapeDtypeStruct((B, S, D), F32),
        scratch_shapes=[pltpu.VMEM((S * SUB_PER_TOK, LANES), F32),
                        pltpu.VMEM((2, CAP * SUB_PER_TOK, LANES), F32)],
        compiler_params=_params(("arbitrary", "arbitrary"), 56),
        name="combine",
    )(idx1, y, x1, mod, fg)


def _pack_gate_weights(wa, wi):
    eye = jnp.eye(4, dtype=wa.dtype)

    def bdiag(w4):
        return jnp.einsum('hij,hg->higj', w4, eye).reshape(4 * LRU_HD, 4 * LRU_HD)

    dirs = []
    for d in range(2):
        halves = []
        for k in range(2):
            hs = slice(4 * k, 4 * (k + 1))
            halves.append(jnp.concatenate([bdiag(wa[d, hs]), bdiag(wi[d, hs])], axis=1))
        dirs.append(jnp.stack(halves))
    return jnp.stack(dirs).astype(BF16)


def kernel(x, c, ctx, c_ctx, w_mod, b_mod, norm1_g, norm2_g, w_in, sgu_g, sgu_w, sgu_b, conv_w, conv_b,
           rg_wa, rg_ba, rg_wi, rg_bi, rg_lam, w_out, w_router, w1, w3, w2, final_g):
    assert x.shape == (B, S, D) and ctx.shape == (B, LC, D) and w_mod.shape[0] == 1

    cc = jnp.concatenate([c, c_ctx[None, :], jnp.zeros((B - 1, D), F32)], axis=0)
    mod = _modulation(cc, w_mod[0], b_mod[0][None, :])

    g1 = norm1_g[0][None, :]
    g2 = norm2_g[0][None, :]
    w_in_b = w_in[0].astype(BF16)
    w_in_x = w_in_b[:, 2 * DS:2 * DS + DL]
    wg = _pack_gate_weights(rg_wa[0], rg_wi[0])
    cw = conv_w[0]
    cb = conv_b[0][None, :]
    ba, bi, lam = rg_ba[0], rg_bi[0], rg_lam[0]
    sbias = jnp.repeat(sgu_b[0].T, HD, axis=1)

    h0f, h0b = _context_states(ctx, mod, g1, w_in_x, cw, cb, wg, ba, bi, lam)
    sgu, gg, cx, hf = _mix_in(x, mod, g1, w_in_b, sgu_g[0][None, :], sgu_w[0].astype(BF16), sbias, cw, cb,
                              wg, ba, bi, lam, h0f)
    x1, hx2t, logits_t = _mix_out(x, mod, cx, hf, sgu, gg, wg, ba, bi, lam, h0b, w_out[0].astype(BF16), g2,
                                  w_router[0].T)
    idx, gate = _route(logits_t)
    idx1 = idx.reshape(B * E * CAP)
    xg = _dispatch(idx1, hx2t)
    y = _moe(xg, gate, w1[0], w3[0], w2[0])
    return _combine(idx1, y, x1, mod, final_g[None, :])
```

```python
import functools

import jax
import jax.numpy as jnp
from jax import lax
from jax.experimental import pallas as pl
from jax.experimental.pallas import tpu as pltpu

F32 = jnp.float32
BF16 = jnp.bfloat16
HIGHEST = lax.Precision.HIGHEST

D = 1024
B = 8
S = 2048
LC = 256
GRID_W = 64
DS = 512
NH = 4
HD = DS // NH
CHUNK = 128
DL = 512
LRU_HEADS = 8
LRU_HD = DL // LRU_HEADS
E = 16
CAP = 2 * S // E
FF = 2048
N_MOD = 6
EPS = 1e-6
RG_C = 8.0

SUBLANES = 8
LANES = 128
VMEM_LIMIT_V7X = 60000 * 1024

TP = CHUNK
ROWS = TP * B
N_TT = S // TP
SUB_PER_TOK = D // LANES
FN = 512
MC = 512
SCATTER_BATCH = 8
EG = 8
NG = E // EG
FIN_ROWS = 1024
PAD_F = SUBLANES
PAD_B = 2 * SUBLANES


def _params(sem, vmem_mb):
    return pltpu.CompilerParams(dimension_semantics=sem, vmem_limit_bytes=min(vmem_mb << 20, VMEM_LIMIT_V7X))


GELU_C1 = 0.7978845608028654
GELU_C2 = GELU_C1 * 0.044715


def _gelu(x):
    half = 0.5 * x
    return half + half * jnp.tanh(x * (GELU_C1 + GELU_C2 * (x * x)))


def _rms_mod(x, gs, sh):
    return (x * lax.rsqrt(jnp.mean(x * x, axis=-1, keepdims=True) + EPS)) * gs + sh


def _log_sigmoid(x):
    return -(jnp.maximum(-x, 0.0) + jnp.log1p(jnp.exp(-jnp.abs(x))))


def _interleave(xx, xi_ref, pos0, npos):
    for k in range(DL // LANES):
        for b in range(B):
            xi_ref[k, pl.ds(PAD_F + pos0 * B + b, npos, stride=B), :] = xx[b][:, k * LANES:(k + 1) * LANES]


def _conv(xi_ref, pos0, npos, period, cw, cb):
    rows = npos * B
    r0 = pos0 * B
    pos = (lax.broadcasted_iota(jnp.int32, (rows, LANES), 0) >> 3) + pos0
    pm = pos & (period - 1)
    m0 = pm != 0
    m2 = pm != period - 1
    m3 = pm < period - 2
    outs = []
    for k in range(DL // LANES):
        w = cw[:, k * LANES:(k + 1) * LANES]
        t0 = xi_ref[k, r0:r0 + rows, :]
        t1 = xi_ref[k, r0 + SUBLANES:r0 + SUBLANES + rows, :]
        t2 = xi_ref[k, r0 + 2 * SUBLANES:r0 + 2 * SUBLANES + rows, :]
        t3 = xi_ref[k, r0 + 3 * SUBLANES:r0 + 3 * SUBLANES + rows, :]
        acc = jnp.where(m0, t0, 0.0) * w[0:1] + t1 * w[1:2]
        acc = acc + jnp.where(m2, t2, 0.0) * w[2:3] + jnp.where(m3, t3, 0.0) * w[3:4]
        outs.append(acc + cb[:, k * LANES:(k + 1) * LANES])
    return jnp.concatenate(outs, axis=1)


def _scan_steps(a, b, h, out_ref, row0, nsteps, reverse):
    for p in (range(nsteps - 1, -1, -1) if reverse else range(nsteps)):
        rs = slice(p * SUBLANES, (p + 1) * SUBLANES)
        h = a[rs, :] * h + b[rs, :]
        out_ref[row0 + p * SUBLANES:row0 + (p + 1) * SUBLANES, :] = h
    return h


def _rglru_coeffs(cx, wg_ref, d, ba, bi, lam):
    cxb = cx.astype(BF16)
    half = DL // 2
    a_parts, b_parts = [], []
    for k in range(2):
        sl = slice(half * k, half * (k + 1))
        z = jnp.dot(cxb[:, sl], wg_ref[d, k], preferred_element_type=F32)
        r = jax.nn.sigmoid(z[:, :half] + ba[:, sl])
        i = jax.nn.sigmoid(z[:, half:] + bi[:, sl])
        log_a = (RG_C * r) * _log_sigmoid(lam[:, sl])
        a = jnp.exp(log_a)
        one_minus_a2 = -jnp.tanh(log_a) * (a * a + 1.0)
        a_parts.append(a)
        b_parts.append(jnp.sqrt(one_minus_a2) * (i * cx[:, sl]))
    return jnp.concatenate(a_parts, axis=1), jnp.concatenate(b_parts, axis=1)


def _scan(a_ref, b_ref, h_out_ref, h0, nsteps, reverse):
    def body(j, h):
        p = (nsteps - 1 - j) if reverse else j
        r0 = pl.multiple_of(p * SUBLANES, SUBLANES)
        h = a_ref[pl.ds(r0, SUBLANES), :] * h + b_ref[pl.ds(r0, SUBLANES), :]
        if h_out_ref is not None:
            h_out_ref[pl.ds(r0, SUBLANES), :] = h
        return h

    return lax.fori_loop(0, nsteps, body, h0, unroll=8)


def _mod_kernel(c_ref, w_ref, b_ref, o_ref):
    c = c_ref[...]
    s = c * jax.nn.sigmoid(c)
    o_ref[...] = jnp.dot(s, w_ref[...], precision=HIGHEST, preferred_element_type=F32) + b_ref[...]


def _modulation(cc, w_mod, b_mod):
    nt = 4
    tn = N_MOD * D // nt
    return pl.pallas_call(
        _mod_kernel,
        grid=(nt,),
        in_specs=[pl.BlockSpec((2 * B, D), lambda i: (0, 0)),
                  pl.BlockSpec((D, tn), lambda i: (0, i)),
                  pl.BlockSpec((1, tn), lambda i: (0, i))],
        out_specs=pl.BlockSpec((2 * B, tn), lambda i: (0, i)),
        out_shape=jax.ShapeDtypeStruct((2 * B, N_MOD * D), F32),
        compiler_params=_params(("arbitrary",), 32),
        name="mod",
    )(cc, w_mod, b_mod)


def _ctx_kernel(ctx_ref, mod_ref, g1_ref, win_ref, cw_ref, cb_ref, wg_ref, ba_ref, bi_ref, lam_ref,
                hf_ref, hb_ref, xi_ref, a_s, b_s):
    sh = mod_ref[B:B + 1, 0:D]
    gs = g1_ref[...] * (1.0 + mod_ref[B:B + 1, D:2 * D])
    parts = []
    for b in range(B):
        hc = _rms_mod(ctx_ref[b], gs, sh).astype(BF16)
        parts.append(jnp.dot(hc, win_ref[...], preferred_element_type=F32))
    nslab = DL // LANES
    xi_ref[:, 0:PAD_F, :] = jnp.zeros((nslab, PAD_F, LANES), F32)
    xi_ref[:, PAD_F + LC * B:PAD_F + LC * B + PAD_B, :] = jnp.zeros((nslab, PAD_B, LANES), F32)
    _interleave(parts, xi_ref, 0, LC)
    cx = _conv(xi_ref, 0, LC, LC, cw_ref[...], cb_ref[...])
    for d in range(2):
        a, dr = _rglru_coeffs(cx, wg_ref, d, ba_ref[d:d + 1, :], bi_ref[d:d + 1, :], lam_ref[d:d + 1, :])
        a_s[...] = a
        b_s[...] = dr
        h = _scan(a_s, b_s, None, jnp.zeros((B, DL), F32), LC, reverse=(d == 1))
        if d == 0:
            hf_ref[...] = h
        else:
            hb_ref[...] = h


def _context_states(ctx, mod, g1, w_in_x, cw, cb, wg, ba, bi, lam):
    rows = LC * B
    full = lambda shape: pl.BlockSpec(shape, lambda i: (0,) * len(shape))
    return pl.pallas_call(
        _ctx_kernel,
        grid=(1,),
        in_specs=[full((B, LC, D)), full((2 * B, N_MOD * D)), full((1, D)), full((D, DL)), full((4, DL)),
                  full((1, DL)), full((2, 2, DL // 2, DL)), full((2, DL)), full((2, DL)), full((2, DL))],
        out_specs=[full((B, DL)), full((B, DL))],
        out_shape=[jax.ShapeDtypeStruct((B, DL), F32)] * 2,
        scratch_shapes=[pltpu.VMEM((DL // LANES, PAD_F + rows + PAD_B, LANES), F32),
                        pltpu.VMEM((rows, DL), F32), pltpu.VMEM((rows, DL), F32)],
        compiler_params=_params(("arbitrary",), 56),
        name="ctx",
    )(ctx, mod, g1, w_in_x, cw, cb, wg, ba, bi, lam)


def _mix_in_kernel(x_ref, mod_ref, g1_ref, win_ref, sg_ref, sw_ref, sbias_ref, cw_ref, cb_ref, wg_ref,
                   ba_ref, bi_ref, lam_ref, h0_ref,
                   sgu_ref, gg_ref, cx_ref, hf_ref,
                   xi_ref, a_s, b_s, carry):
    @pl.when(pl.program_id(0) == 0)
    def _():
        carry[...] = h0_ref[...]
        xi_ref[...] = jnp.zeros(xi_ref.shape, F32)

    g1 = g1_ref[...]
    hx = []
    for b in range(B):
        sh = mod_ref[b:b + 1, 0:D]
        gs = g1 * (1.0 + mod_ref[b:b + 1, D:2 * D])
        hx.append(_rms_mod(x_ref[b], gs, sh).astype(BF16))
    z = jnp.dot(jnp.concatenate(hx, axis=0), win_ref[...], preferred_element_type=F32)
    ug = _gelu(z[:, 0:DS])
    vg = _gelu(z[:, DS:2 * DS])
    gg = _gelu(z[:, 2 * DS + DL:])
    xx = []
    for b in range(B):
        gg_ref[b] = gg[b * TP:(b + 1) * TP, :].astype(BF16)
        xx.append(z[b * TP:(b + 1) * TP, 2 * DS:2 * DS + DL])

    for hd in range(NH):
        hs = slice(hd * HD, (hd + 1) * HD)
        vh = vg[:, hs]
        vn = (vh * lax.rsqrt(jnp.mean(vh * vh, axis=-1, keepdims=True) + EPS)) * sg_ref[:, hs]
        vnb = vn.astype(BF16)
        for b in range(B):
            rs = slice(b * TP, (b + 1) * TP)
            s = jnp.dot(sw_ref[hd], vnb[rs], preferred_element_type=F32) + sbias_ref[:, hs]
            sgu_ref[b, :, hs] = (ug[rs, hs] * s).astype(BF16)

    _interleave(xx, xi_ref, 0, TP)
    cx = _conv(xi_ref, 0, TP, GRID_W, cw_ref[...], cb_ref[...])
    cx_ref[...] = cx
    a, dr = _rglru_coeffs(cx, wg_ref, 0, ba_ref[0:1, :], bi_ref[0:1, :], lam_ref[0:1, :])
    a_s[...] = a
    b_s[...] = dr
    carry[...] = _scan(a_s, b_s, hf_ref, carry[...], TP, reverse=False)


def _mix_in(x, mod, g1, w_in_b, sg, sw_b, sbias, cw, cb, wg, ba, bi, lam, h0f):
    full = lambda shape: pl.BlockSpec(shape, lambda i: (0,) * len(shape))
    return pl.pallas_call(
        _mix_in_kernel,
        grid=(N_TT,),
        in_specs=[pl.BlockSpec((B, TP, D), lambda i: (0, i, 0)),
                  full((2 * B, N_MOD * D)), full((1, D)), full((D, 2 * DS + 2 * DL)), full((1, DS)),
                  full((NH, CHUNK, CHUNK)), full((CHUNK, DS)), full((4, DL)), full((1, DL)),
                  full((2, 2, DL // 2, DL)), full((2, DL)), full((2, DL)), full((2, DL)), full((B, DL))],
        out_specs=[pl.BlockSpec((B, TP, DS), lambda i: (0, i, 0)),
                   pl.BlockSpec((B, TP, DL), lambda i: (0, i, 0)),
                   pl.BlockSpec((ROWS, DL), lambda i: (i, 0)),
                   pl.BlockSpec((ROWS, DL), lambda i: (i, 0))],
        out_shape=[jax.ShapeDtypeStruct((B, S, DS), BF16), jax.ShapeDtypeStruct((B, S, DL), BF16),
                   jax.ShapeDtypeStruct((S * B, DL), F32), jax.ShapeDtypeStruct((S * B, DL), F32)],
        scratch_shapes=[pltpu.VMEM((DL // LANES, PAD_F + ROWS + PAD_B, LANES), F32),
                        pltpu.VMEM((ROWS, DL), F32), pltpu.VMEM((ROWS, DL), F32),
                        pltpu.VMEM((B, DL), F32)],
        compiler_params=_params(("arbitrary",), 56),
        name="mix_in",
    )(x, mod, g1, w_in_b, sg, sw_b, sbias, cw, cb, wg, ba, bi, lam, h0f)


def _mix_out_kernel(x_ref, mod_ref, cx_ref, hf_ref, sgu_ref, gg_ref, wg_ref, ba_ref, bi_ref, lam_ref, h0_ref,
                    wout_ref, g2_ref, wr_ref,
                    x1_ref, hx2t_ref, lg_ref,
                    a_s, b_s, hb_s, hs_ref, mix_ref, carry):
    @pl.when(pl.program_id(0) == 0)
    def _():
        carry[...] = h0_ref[...]

    a, dr = _rglru_coeffs(cx_ref[...], wg_ref, 1, ba_ref[1:2, :], bi_ref[1:2, :], lam_ref[1:2, :])
    a_s[...] = a
    b_s[...] = dr
    carry[...] = _scan(a_s, b_s, hb_s, carry[...], TP, reverse=True)
    hsum = hf_ref[...] + hb_s[...]
    nslab = DL // LANES
    for k in range(nslab):
        hs_ref[k] = hsum[:, k * LANES:(k + 1) * LANES]
    for b in range(B):
        rs = slice(b * TP, (b + 1) * TP)
        mix_ref[rs, 0:DS] = sgu_ref[b]
        for k in range(nslab):
            hk = hs_ref[k, pl.ds(b, TP, stride=B), :]
            gk = gg_ref[b, :, k * LANES:(k + 1) * LANES].astype(F32)
            mix_ref[rs, DS + k * LANES:DS + (k + 1) * LANES] = (gk * hk).astype(BF16)

    y = jnp.dot(mix_ref[...], wout_ref[...], preferred_element_type=F32)
    g2 = g2_ref[...]
    wr = wr_ref[...]
    wr_hi = wr.astype(BF16)
    wr_lo = (wr - wr_hi.astype(F32)).astype(BF16)
    for b in range(B):
        g1x = mod_ref[b:b + 1, 2 * D:3 * D]
        sh2 = mod_ref[b:b + 1, 3 * D:4 * D]
        gs2 = g2 * (1.0 + mod_ref[b:b + 1, 4 * D:5 * D])
        x1 = x_ref[b] + g1x * y[b * TP:(b + 1) * TP, :]
        x1_ref[b] = x1
        hx2 = _rms_mod(x1, gs2, sh2)
        for k in range(SUB_PER_TOK):
            hx2t_ref[b, pl.ds(k, TP, stride=SUB_PER_TOK), :] = hx2[:, k * LANES:(k + 1) * LANES]
        hx_hi = hx2.astype(BF16)
        hx_lo = (hx2 - hx_hi.astype(F32)).astype(BF16)
        nt = (((1,), (1,)), ((), ()))
        lg_ref[b] = (lax.dot_general(wr_hi, hx_hi, nt, preferred_element_type=F32)
                     + (lax.dot_general(wr_hi, hx_lo, nt, preferred_element_type=F32)
                        + lax.dot_general(wr_lo, hx_hi, nt, preferred_element_type=F32)))


def _mix_out(x, mod, cx, hf, sgu, gg, wg, ba, bi, lam, h0b, w_out_b, g2, wr_t):
    full = lambda shape: pl.BlockSpec(shape, lambda i: (0,) * len(shape))
    rev = lambda i: N_TT - 1 - i
    return pl.pallas_call(
        _mix_out_kernel,
        grid=(N_TT,),
        in_specs=[pl.BlockSpec((B, TP, D), lambda i: (0, rev(i), 0)),
                  full((2 * B, N_MOD * D)),
                  pl.BlockSpec((ROWS, DL), lambda i: (rev(i), 0)),
                  pl.BlockSpec((ROWS, DL), lambda i: (rev(i), 0)),
                  pl.BlockSpec((B, TP, DS), lambda i: (0, rev(i), 0)),
                  pl.BlockSpec((B, TP, DL), lambda i: (0, rev(i), 0)),
                  full((2, 2, DL // 2, DL)), full((2, DL)), full((2, DL)), full((2, DL)), full((B, DL)),
                  full((D, D)), full((1, D)), full((E, D))],
        out_specs=[pl.BlockSpec((B, TP, D), lambda i: (0, rev(i), 0)),
                   pl.BlockSpec((B, TP * SUB_PER_TOK, LANES), lambda i: (0, rev(i), 0)),
                   pl.BlockSpec((B, E, TP), lambda i: (0, 0, rev(i)))],
        out_shape=[jax.ShapeDtypeStruct((B, S, D), F32),
                   jax.ShapeDtypeStruct((B, S * SUB_PER_TOK, LANES), F32),
                   jax.ShapeDtypeStruct((B, E, S), F32)],
        scratch_shapes=[pltpu.VMEM((ROWS, DL), F32), pltpu.VMEM((ROWS, DL), F32), pltpu.VMEM((ROWS, DL), F32),
                        pltpu.VMEM((DL // LANES, ROWS, LANES), F32),
                        pltpu.VMEM((ROWS, D), BF16),
                        pltpu.VMEM((B, DL), F32)],
        compiler_params=_params(("arbitrary",), 56),
        name="mix_out",
    )(x, mod, cx, hf, sgu, gg, wg, ba, bi, lam, h0b, w_out_b, g2, wr_t)


NBLK = S // LANES
BIG = 1.0e9


def _bf16_parts(a):
    hi = a.astype(BF16).astype(F32)
    r1 = a - hi
    mid = r1.astype(BF16).astype(F32)
    lo = (r1 - mid).astype(BF16).astype(F32)
    return hi, mid, lo


def _route_kernel(lg_ref, idx_ref, gate_ref, aff_s, lci_s, offb_s, ahi_s, amid_s, alo_s, offi_s):
    for b in range(B):
        l = lg_ref[b]
        ex = jnp.exp(l - jnp.max(l, axis=0, keepdims=True))
        aff_s[b * E:(b + 1) * E, :] = ex / jnp.sum(ex, axis=0, keepdims=True)
    aff = aff_s[...]
    nrow = B * E

    def bisect(_, lohi):
        lo, hi = lohi
        mid = lo + ((hi - lo + 1) >> 1)
        cnt = jnp.sum(jnp.where(aff >= pltpu.bitcast(mid, F32), 1.0, 0.0), axis=1, keepdims=True)
        ge = cnt >= float(CAP)
        return jnp.where(ge, mid, lo), jnp.where(ge, hi, mid - 1)

    lo0 = jnp.zeros((nrow, 1), jnp.int32)
    hi0 = jnp.full((nrow, 1), 0x7F800000, jnp.int32)
    thr_bits, _ = lax.fori_loop(0, 31, bisect, (lo0, hi0))
    thr = pltpu.bitcast(thr_bits, F32)
    gt = aff > thr
    eq = aff == thr
    need = float(CAP) - jnp.sum(jnp.where(gt, 1.0, 0.0), axis=1, keepdims=True)

    qi = lax.broadcasted_iota(jnp.int32, (LANES, LANES), 0)
    ti = lax.broadcasted_iota(jnp.int32, (LANES, LANES), 1)
    tri = jnp.where(qi <= ti, 1.0, 0.0).astype(BF16)
    blocks = [slice(j * LANES, (j + 1) * LANES) for j in range(NBLK)]

    ties_before = jnp.zeros((nrow, 1), F32)
    sel = []
    for sl in blocks:
        eqb = jnp.where(eq[:, sl], 1.0, 0.0)
        incl = jnp.dot(eqb.astype(BF16), tri, preferred_element_type=F32) + ties_before
        sel.append(gt[:, sl] | (eq[:, sl] & ((incl - eqb) < need)))
        ties_before = incl[:, LANES - 1:LANES]

    lane_sq = lax.broadcasted_iota(jnp.int32, (nrow, LANES), 1)
    off = jnp.zeros((nrow, 1), F32)
    offi = jnp.full((nrow, LANES), BIG, F32)
    for j, sl in enumerate(blocks):
        rows = slice(j * nrow, (j + 1) * nrow)
        lci = jnp.dot(jnp.where(sel[j], 1.0, 0.0).astype(BF16), tri, preferred_element_type=F32)
        lci_s[rows, :] = lci
        offb_s[rows, :] = jnp.broadcast_to(off, (nrow, LANES))
        ahi_s[rows, :], amid_s[rows, :], alo_s[rows, :] = _bf16_parts(aff[:, sl])
        off = off + lci[:, LANES - 1:LANES]
        offi = jnp.where(lane_sq == j, off, offi)
    offi_s[...] = offi

    slot = lax.broadcasted_iota(jnp.int32, (CAP, LANES), 0).astype(F32)
    lane = lax.broadcasted_iota(jnp.int32, (CAP, LANES), 1)
    lane_f = lane.astype(F32)
    zpad = jnp.zeros((LANES - NBLK, 5 * LANES), BF16)

    def row_body(r, carry):
        idxm, gm = carry
        take = lambda ref: ref[pl.ds(r, NBLK, stride=nrow), :]
        table = jnp.concatenate([take(lci_s), take(offb_s), take(ahi_s), take(amid_s), take(alo_s)], axis=1)
        table = jnp.concatenate([table.astype(BF16), zpad], axis=0)
        blk = jnp.sum(jnp.where(offi_s[pl.ds(r, 1), :] <= slot, 1.0, 0.0), axis=1, keepdims=True)
        pick = jnp.where(lane_f == blk, 1.0, 0.0).astype(BF16)
        res = jnp.dot(pick, table, preferred_element_type=F32)
        rank1 = (slot + 1.0) - res[:, LANES:2 * LANES]
        tokl = jnp.sum(jnp.where(res[:, 0:LANES] < rank1, 1.0, 0.0), axis=1, keepdims=True)
        affs = (res[:, 2 * LANES:3 * LANES] + res[:, 3 * LANES:4 * LANES]) + res[:, 4 * LANES:5 * LANES]
        gv = jnp.sum(jnp.where(lane_f == tokl, affs, 0.0), axis=1, keepdims=True)
        iv = blk * float(LANES) + tokl
        put = lane == r
        return jnp.where(put, iv, idxm), jnp.where(put, gv, gm)

    z = jnp.zeros((CAP, LANES), F32)
    idxm, gm = lax.fori_loop(0, nrow, row_body, (z, z), unroll=8)
    idx_ref[...] = (idxm.T * float(SUB_PER_TOK)).astype(jnp.int32)
    gate_ref[...] = gm


def _route(logits_t):
    full = lambda shape: pl.BlockSpec(shape, lambda i: (0,) * len(shape))
    return pl.pallas_call(
        _route_kernel,
        grid=(1,),
        in_specs=[full((B, E, S))],
        out_specs=[full((B * E, CAP)), full((CAP, B * E))],
        out_shape=[jax.ShapeDtypeStruct((B * E, CAP), jnp.int32), jax.ShapeDtypeStruct((CAP, B * E), F32)],
        scratch_shapes=[pltpu.VMEM((B * E, S), F32)] + [pltpu.VMEM((NBLK * B * E, LANES), F32)] * 5
                       + [pltpu.VMEM((B * E, LANES), F32)],
        compiler_params=_params(("arbitrary",), 48),
        name="route",
    )(logits_t)


def _dispatch_kernel(idx_ref, h_ref, xg_ref, xt):
    for e in range(E):
        def body(s, c):
            src = pl.multiple_of(idx_ref[e * CAP + s], SUB_PER_TOK)
            dst = pl.multiple_of(s * SUB_PER_TOK, SUB_PER_TOK)
            xt[pl.ds(dst, SUB_PER_TOK), :] = h_ref[0, pl.ds(src, SUB_PER_TOK), :]
            return c

        lax.fori_loop(0, CAP, body, 0, unroll=8)
        for k in range(SUB_PER_TOK):
            xg_ref[e, :, k * LANES:(k + 1) * LANES] = xt[pl.ds(k, CAP, stride=SUB_PER_TOK), :].astype(BF16)


def _dispatch(idx1, hx2t):
    return pl.pallas_call(
        _dispatch_kernel,
        grid=(B,),
        in_specs=[pl.BlockSpec((E * CAP,), lambda b: (b,), memory_space=pltpu.SMEM),
                  pl.BlockSpec((1, S * SUB_PER_TOK, LANES), lambda b: (b, 0, 0))],
        out_specs=pl.BlockSpec((E, CAP, D), lambda b: (0, b, 0)),
        out_shape=jax.ShapeDtypeStruct((E, B * CAP, D), BF16),
        scratch_shapes=[pltpu.VMEM((CAP * SUB_PER_TOK, LANES), F32)],
        compiler_params=_params(("arbitrary",), 40),
        name="dispatch",
    )(idx1, hx2t)


def _moe_kernel(x_ref, gate_ref, w1_ref, w3_ref, w2_ref, y_ref, hid_s, w2_s):
    f = pl.program_id(1)
    w1 = w1_ref[0].astype(BF16)
    w3 = w3_ref[0].astype(BF16)
    fs = pl.multiple_of(f * FN, FN)
    w2_s[pl.ds(fs, FN), :] = w2_ref[0].astype(BF16)
    for c in range(B * CAP // MC):
        rs = slice(c * MC, (c + 1) * MC)
        x = x_ref[0, rs, :]
        h1 = jnp.dot(x, w1, preferred_element_type=F32)
        h3 = jnp.dot(x, w3, preferred_element_type=F32)
        hid_s[f, rs, :] = ((h1 * jax.nn.sigmoid(h1)) * h3).astype(BF16)

    @pl.when(f == FF // FN - 1)
    def _():
        gates = gate_ref[...]
        lane = lax.broadcasted_iota(jnp.int32, gates.shape, 1)
        e = pl.program_id(0)
        for c in range(B * CAP // MC):
            rs = slice(c * MC, (c + 1) * MC)
            hid = jnp.concatenate([hid_s[j, rs, :] for j in range(FF // FN)], axis=1)
            y = jnp.dot(hid, w2_s[...], preferred_element_type=F32)
            for bb in range(MC // CAP):
                b = c * (MC // CAP) + bb
                gcol = jnp.sum(jnp.where(lane == b * E + e, gates, 0.0), axis=1, keepdims=True)
                y_ref[0, b * CAP:(b + 1) * CAP, :] = (y[bb * CAP:(bb + 1) * CAP, :] * gcol).astype(BF16)


def _moe(xg, gate_cols, w1, w3, w2):
    return pl.pallas_call(
        _moe_kernel,
        grid=(E, FF // FN),
        in_specs=[pl.BlockSpec((1, B * CAP, D), lambda e, f: (e, 0, 0)),
                  pl.BlockSpec((CAP, B * E), lambda e, f: (0, 0)),
                  pl.BlockSpec((1, D, FN), lambda e, f: (e, 0, f)),
                  pl.BlockSpec((1, D, FN), lambda e, f: (e, 0, f)),
                  pl.BlockSpec((1, FN, D), lambda e, f: (e, f, 0))],
        out_specs=pl.BlockSpec((1, B * CAP, D), lambda e, f: (e, 0, 0)),
        out_shape=jax.ShapeDtypeStruct((E, B * CAP, D), BF16),
        scratch_shapes=[pltpu.VMEM((FF // FN, B * CAP, FN), BF16), pltpu.VMEM((FF, D), BF16)],
        compiler_params=_params(("arbitrary", "arbitrary"), 56),
        name="moe",
    )(xg, gate_cols, w1, w3, w2)


def _combine_kernel(idx_ref, y_ref, x1_ref, mod_ref, fg_ref, out_ref, acc0, acc1, yt0, yt1):
    b = pl.program_id(0)
    g = pl.program_id(1)

    @pl.when(g == 0)
    def _():
        acc0[...] = jnp.zeros(acc0.shape, F32)
        acc1[...] = jnp.zeros(acc1.shape, F32)

    @pl.when(g < NG)
    def _():
        for pair in range(EG // 2):
            lanes = [(2 * pair, acc0, yt0), (2 * pair + 1, acc1, yt1)]
            for el, _, ytb in lanes:
                for k in range(SUB_PER_TOK):
                    ytb[pl.ds(k, CAP, stride=SUB_PER_TOK), :] = y_ref[el, :, k * LANES:(k + 1) * LANES].astype(F32)

            def ids_at(s0):
                return tuple(idx_ref[el * CAP + s0 + j] for el, _, _ in lanes for j in range(SCATTER_BATCH))

            def body(i, ids):
                s0 = i * SCATTER_BATCH
                pending = []
                for n, (el, acc, ytb) in enumerate(lanes):
                    for j in range(SCATTER_BATCH):
                        dst = pl.multiple_of(ids[n * SCATTER_BATCH + j], SUB_PER_TOK)
                        src = pl.multiple_of((s0 + j) * SUB_PER_TOK, SUB_PER_TOK)
                        pending.append((acc, dst, acc[pl.ds(dst, SUB_PER_TOK), :] + ytb[pl.ds(src, SUB_PER_TOK), :]))
                nxt = ids_at(jnp.minimum(s0 + SCATTER_BATCH, CAP - SCATTER_BATCH))
                for acc, dst, new in pending:
                    acc[pl.ds(dst, SUB_PER_TOK), :] = new
                return nxt

            lax.fori_loop(0, CAP // SCATTER_BATCH, body, ids_at(0))

    @pl.when(g >= NG)
    def _():
        row0 = pl.multiple_of((g - NG) * (FIN_ROWS * SUB_PER_TOK), FIN_ROWS * SUB_PER_TOK)
        g2x = mod_ref[pl.ds(b, 1), 5 * D:6 * D]
        ssq = jnp.zeros((FIN_ROWS, 1), F32)
        for k in range(SUB_PER_TOK):
            ls = slice(k * LANES, (k + 1) * LANES)
            moe_k = (acc0[pl.ds(row0 + k, FIN_ROWS, stride=SUB_PER_TOK), :]
                     + acc1[pl.ds(row0 + k, FIN_ROWS, stride=SUB_PER_TOK), :])
            xo = x1_ref[0, :, ls] + g2x[:, ls] * moe_k
            out_ref[0, :, ls] = xo
            ssq = ssq + jnp.sum(xo * xo, axis=1, keepdims=True)
        inv = lax.rsqrt(ssq * (1.0 / D) + EPS)
        out_ref[0] = (out_ref[0] * inv) * fg_ref[...]


def _combine(idx1, y, x1, mod, fg):
    fin = lambda b, g: (b, jnp.maximum(g - NG, 0), 0)
    grp = lambda g: jnp.minimum(g, NG - 1)
    return pl.pallas_call(
        _combine_kernel,
        grid=(B, NG + S // FIN_ROWS),
        in_specs=[pl.BlockSpec((EG * CAP,), lambda b, g: (b * NG + grp(g),), memory_space=pltpu.SMEM),
                  pl.BlockSpec((EG, CAP, D), lambda b, g: (grp(g), b, 0)),
                  pl.BlockSpec((1, FIN_ROWS, D), fin),
                  pl.BlockSpec((2 * B, N_MOD * D), lambda b, g: (0, 0)),
                  pl.BlockSpec((1, D), lambda b, g: (0, 0))],
        out_specs=pl.BlockSpec((1, FIN_ROWS, D), fin),
        out_shape=jax.ShapeDtypeStruct((B, S, D), F32),
        scratch_shapes=[pltpu.VMEM((S * SUB_PER_TOK, LANES), F32), pltpu.VMEM((S * SUB_PER_TOK, LANES), F32),
                        pltpu.VMEM((CAP * SUB_PER_TOK, LANES), F32), pltpu.VMEM((CAP * SUB_PER_TOK, LANES), F32)],
        compiler_params=_params(("arbitrary", "arbitrary"), 56),
        name="combine",
    )(idx1, y, x1, mod, fg)


def _pack_gate_weights(wa, wi):
    eye = jnp.eye(4, dtype=wa.dtype)

    def bdiag(w4):
        return jnp.einsum('hij,hg->higj', w4, eye).reshape(4 * LRU_HD, 4 * LRU_HD)

    dirs = []
    for d in range(2):
        halves = []
        for k in range(2):
            hs = slice(4 * k, 4 * (k + 1))
            halves.append(jnp.concatenate([bdiag(wa[d, hs]), bdiag(wi[d, hs])], axis=1))
        dirs.append(jnp.stack(halves))
    return jnp.stack(dirs).astype(BF16)


def kernel(x, c, ctx, c_ctx, w_mod, b_mod, norm1_g, norm2_g, w_in, sgu_g, sgu_w, sgu_b, conv_w, conv_b,
           rg_wa, rg_ba, rg_wi, rg_bi, rg_lam, w_out, w_router, w1, w3, w2, final_g):
    assert x.shape == (B, S, D) and ctx.shape == (B, LC, D) and w_mod.shape[0] == 1

    cc = jnp.concatenate([c, c_ctx[None, :], jnp.zeros((B - 1, D), F32)], axis=0)
    mod = _modulation(cc, w_mod[0], b_mod[0][None, :])

    g1 = norm1_g[0][None, :]
    g2 = norm2_g[0][None, :]
    w_in_b = w_in[0].astype(BF16)
    w_in_x = w_in_b[:, 2 * DS:2 * DS + DL]
    wg = _pack_gate_weights(rg_wa[0], rg_wi[0])
    cw = conv_w[0]
    cb = conv_b[0][None, :]
    ba, bi, lam = rg_ba[0], rg_bi[0], rg_lam[0]
    sbias = jnp.repeat(sgu_b[0].T, HD, axis=1)

    h0f, h0b = _context_states(ctx, mod, g1, w_in_x, cw, cb, wg, ba, bi, lam)
    sgu, gg, cx, hf = _mix_in(x, mod, g1, w_in_b, sgu_g[0][None, :], sgu_w[0].astype(BF16), sbias, cw, cb,
                              wg, ba, bi, lam, h0f)
    x1, hx2t, logits_t = _mix_out(x, mod, cx, hf, sgu, gg, wg, ba, bi, lam, h0b, w_out[0].astype(BF16), g2,
                                  w_router[0].T)
    idx, gate = _route(logits_t)
    idx1 = idx.reshape(B * E * CAP)
    xg = _dispatch(idx1, hx2t)
    y = _moe(xg, gate, w1[0], w3[0], w2[0])
    return _combine(idx1, y, x1, mod, final_g[None, :])
```

```python
import jax
import jax.numpy as jnp
from jax import lax
from jax.experimental import pallas as pl
from jax.experimental.pallas import tpu as pltpu

F32 = jnp.float32
BF16 = jnp.bfloat16
HIGHEST = lax.Precision.HIGHEST

D = 1024
B = 8
S = 2048
LC = 256
GRID_W = 64
DS = 512
NH = 4
HD = DS // NH
CHUNK = 128
DL = 512
LRU_HEADS = 8
LRU_HD = DL // LRU_HEADS
E = 16
CAP = 2 * S // E
FF = 2048
N_MOD = 6
EPS = 1e-6
RG_C = 8.0

SUBLANES = 8
LANES = 128
VMEM_LIMIT_V7X = 60000 * 1024

TP = CHUNK
ROWS = TP * B
N_TT = S // TP
SUB_PER_TOK = D // LANES
FN = 512
MC = 512
SCATTER_BATCH = 16
EG = 8
NG = E // EG
FIN_ROWS = 1024
PAD_F = SUBLANES
PAD_B = 2 * SUBLANES


def _params(sem, vmem_mb):
    return pltpu.CompilerParams(dimension_semantics=sem, vmem_limit_bytes=min(vmem_mb << 20, VMEM_LIMIT_V7X))


GELU_C1 = 0.7978845608028654
GELU_C2 = GELU_C1 * 0.044715


def _gelu(x):
    half = 0.5 * x
    return half + half * jnp.tanh(x * (GELU_C1 + GELU_C2 * (x * x)))


def _rms_mod(x, gs, sh):
    return (x * lax.rsqrt(jnp.mean(x * x, axis=-1, keepdims=True) + EPS)) * gs + sh


def _log_sigmoid(x):
    return -(jnp.maximum(-x, 0.0) + jnp.log1p(jnp.exp(-jnp.abs(x))))


def _conv_interleaved(xx, xi_ref, tp, period, cw, cb):
    rows = tp * B
    nslab = DL // LANES
    xi_ref[:, 0:PAD_F, :] = jnp.zeros((nslab, PAD_F, LANES), F32)
    xi_ref[:, PAD_F + rows:PAD_F + rows + PAD_B, :] = jnp.zeros((nslab, PAD_B, LANES), F32)
    for k in range(nslab):
        for b in range(B):
            xi_ref[k, pl.ds(PAD_F + b, tp, stride=B), :] = xx[b * tp:(b + 1) * tp, k * LANES:(k + 1) * LANES]
    pos = lax.broadcasted_iota(jnp.int32, (rows, LANES), 0) >> 3
    pm = pos & (period - 1)
    m0 = pm != 0
    m2 = pm != period - 1
    m3 = pm < period - 2
    outs = []
    for k in range(nslab):
        w = cw[:, k * LANES:(k + 1) * LANES]
        t0 = xi_ref[k, 0:rows, :]
        t1 = xi_ref[k, SUBLANES:SUBLANES + rows, :]
        t2 = xi_ref[k, 2 * SUBLANES:2 * SUBLANES + rows, :]
        t3 = xi_ref[k, 3 * SUBLANES:3 * SUBLANES + rows, :]
        acc = jnp.where(m0, t0, 0.0) * w[0:1] + t1 * w[1:2]
        acc = acc + jnp.where(m2, t2, 0.0) * w[2:3] + jnp.where(m3, t3, 0.0) * w[3:4]
        outs.append(acc + cb[:, k * LANES:(k + 1) * LANES])
    return jnp.concatenate(outs, axis=1)


def _rglru_coeffs(cx, wg_ref, d, ba, bi, lam):
    cxb = cx.astype(BF16)
    half = DL // 2
    a_parts, b_parts = [], []
    for k in range(2):
        sl = slice(half * k, half * (k + 1))
        z = jnp.dot(cxb[:, sl], wg_ref[d, k], preferred_element_type=F32)
        r = jax.nn.sigmoid(z[:, :half] + ba[:, sl])
        i = jax.nn.sigmoid(z[:, half:] + bi[:, sl])
        neg_log_a = r * (RG_C * -_log_sigmoid(lam[:, sl]))
        a = jnp.exp(-neg_log_a)
        one_minus_a2 = jnp.tanh(neg_log_a) * (a * a + 1.0)
        root = jnp.where(one_minus_a2 > 0.0, one_minus_a2 * lax.rsqrt(one_minus_a2), 0.0)
        a_parts.append(a)
        b_parts.append(root * (i * cx[:, sl]))
    return jnp.concatenate(a_parts, axis=1), jnp.concatenate(b_parts, axis=1)


def _scan(a_ref, b_ref, h_out_ref, h0, nsteps, reverse):
    def body(j, h):
        p = (nsteps - 1 - j) if reverse else j
        r0 = pl.multiple_of(p * SUBLANES, SUBLANES)
        h = a_ref[pl.ds(r0, SUBLANES), :] * h + b_ref[pl.ds(r0, SUBLANES), :]
        if h_out_ref is not None:
            h_out_ref[pl.ds(r0, SUBLANES), :] = h
        return h

    return lax.fori_loop(0, nsteps, body, h0, unroll=8)


def _mod_kernel(c_ref, w_ref, b_ref, o_ref):
    c = c_ref[...]
    s = c * jax.nn.sigmoid(c)
    o_ref[...] = jnp.dot(s, w_ref[...], precision=HIGHEST, preferred_element_type=F32) + b_ref[...]


def _modulation(cc, w_mod, b_mod):
    nt = 4
    tn = N_MOD * D // nt
    return pl.pallas_call(
        _mod_kernel,
        grid=(nt,),
        in_specs=[pl.BlockSpec((2 * B, D), lambda i: (0, 0)),
                  pl.BlockSpec((D, tn), lambda i: (0, i)),
                  pl.BlockSpec((1, tn), lambda i: (0, i))],
        out_specs=pl.BlockSpec((2 * B, tn), lambda i: (0, i)),
        out_shape=jax.ShapeDtypeStruct((2 * B, N_MOD * D), F32),
        compiler_params=_params(("arbitrary",), 32),
        name="mod",
    )(cc, w_mod, b_mod)


def _ctx_kernel(ctx_ref, mod_ref, g1_ref, win_ref, cw_ref, cb_ref, wg_ref, ba_ref, bi_ref, lam_ref,
                hf_ref, hb_ref, xi_ref, a_s, b_s):
    sh = mod_ref[B:B + 1, 0:D]
    gs = g1_ref[...] * (1.0 + mod_ref[B:B + 1, D:2 * D])
    parts = []
    for b in range(B):
        hc = _rms_mod(ctx_ref[b], gs, sh).astype(BF16)
        parts.append(jnp.dot(hc, win_ref[...], preferred_element_type=F32))
    xx = jnp.concatenate(parts, axis=0)
    cx = _conv_interleaved(xx, xi_ref, LC, LC, cw_ref[...], cb_ref[...])
    for d in range(2):
        a, dr = _rglru_coeffs(cx, wg_ref, d, ba_ref[d:d + 1, :], bi_ref[d:d + 1, :], lam_ref[d:d + 1, :])
        a_s[...] = a
        b_s[...] = dr
        h = _scan(a_s, b_s, None, jnp.zeros((B, DL), F32), LC, reverse=(d == 1))
        if d == 0:
            hf_ref[...] = h
        else:
            hb_ref[...] = h


def _context_states(ctx, mod, g1, w_in_x, cw, cb, wg, ba, bi, lam):
    rows = LC * B
    full = lambda shape: pl.BlockSpec(shape, lambda i: (0,) * len(shape))
    return pl.pallas_call(
        _ctx_kernel,
        grid=(1,),
        in_specs=[full((B, LC, D)), full((2 * B, N_MOD * D)), full((1, D)), full((D, DL)), full((4, DL)),
                  full((1, DL)), full((2, 2, DL // 2, DL)), full((2, DL)), full((2, DL)), full((2, DL))],
        out_specs=[full((B, DL)), full((B, DL))],
        out_shape=[jax.ShapeDtypeStruct((B, DL), F32)] * 2,
        scratch_shapes=[pltpu.VMEM((DL // LANES, PAD_F + rows + PAD_B, LANES), F32),
                        pltpu.VMEM((rows, DL), F32), pltpu.VMEM((rows, DL), F32)],
        compiler_params=_params(("arbitrary",), 56),
        name="ctx",
    )(ctx, mod, g1, w_in_x, cw, cb, wg, ba, bi, lam)


def _mix_in_kernel(x_ref, mod_ref, g1_ref, win_ref, sg_ref, sw_ref, sbias_ref, cw_ref, cb_ref, wg_ref,
                   ba_ref, bi_ref, lam_ref, h0_ref,
                   sgu_ref, gg_ref, cx_ref, hf_ref,
                   hbuf, xi_ref, a_s, b_s, carry):
    @pl.when(pl.program_id(0) == 0)
    def _():
        carry[...] = h0_ref[...]

    g1 = g1_ref[...]
    for b in range(B):
        sh = mod_ref[b:b + 1, 0:D]
        gs = g1 * (1.0 + mod_ref[b:b + 1, D:2 * D])
        hbuf[b * TP:(b + 1) * TP, :] = _rms_mod(x_ref[b], gs, sh).astype(BF16)
    z = jnp.dot(hbuf[...], win_ref[...], preferred_element_type=F32)
    ug = _gelu(z[:, 0:DS])
    vg = _gelu(z[:, DS:2 * DS])
    xx = z[:, 2 * DS:2 * DS + DL]
    gg = _gelu(z[:, 2 * DS + DL:])
    for b in range(B):
        gg_ref[b] = gg[b * TP:(b + 1) * TP, :].astype(BF16)

    for h in range(NH):
        hs = slice(h * HD, (h + 1) * HD)
        vh = vg[:, hs]
        vn = (vh * lax.rsqrt(jnp.mean(vh * vh, axis=-1, keepdims=True) + EPS)) * sg_ref[:, hs]
        vnb = vn.astype(BF16)
        for b in range(B):
            rs = slice(b * TP, (b + 1) * TP)
            s = jnp.dot(sw_ref[h], vnb[rs], preferred_element_type=F32) + sbias_ref[:, hs]
            sgu_ref[b, :, hs] = (ug[rs, hs] * s).astype(BF16)

    cx = _conv_interleaved(xx, xi_ref, TP, GRID_W, cw_ref[...], cb_ref[...])
    cx_ref[...] = cx
    a, dr = _rglru_coeffs(cx, wg_ref, 0, ba_ref[0:1, :], bi_ref[0:1, :], lam_ref[0:1, :])
    a_s[...] = a
    b_s[...] = dr
    carry[...] = _scan(a_s, b_s, hf_ref, carry[...], TP, reverse=False)


def _mix_in(x, mod, g1, w_in_b, sg, sw_b, sbias, cw, cb, wg, ba, bi, lam, h0f):
    full = lambda shape: pl.BlockSpec(shape, lambda i: (0,) * len(shape))
    return pl.pallas_call(
        _mix_in_kernel,
        grid=(N_TT,),
        in_specs=[pl.BlockSpec((B, TP, D), lambda i: (0, i, 0)),
                  full((2 * B, N_MOD * D)), full((1, D)), full((D, 2 * DS + 2 * DL)), full((1, DS)),
                  full((NH, CHUNK, CHUNK)), full((CHUNK, DS)), full((4, DL)), full((1, DL)),
                  full((2, 2, DL // 2, DL)), full((2, DL)), full((2, DL)), full((2, DL)), full((B, DL))],
        out_specs=[pl.BlockSpec((B, TP, DS), lambda i: (0, i, 0)),
                   pl.BlockSpec((B, TP, DL), lambda i: (0, i, 0)),
                   pl.BlockSpec((ROWS, DL), lambda i: (i, 0)),
                   pl.BlockSpec((ROWS, DL), lambda i: (i, 0))],
        out_shape=[jax.ShapeDtypeStruct((B, S, DS), BF16), jax.ShapeDtypeStruct((B, S, DL), BF16),
                   jax.ShapeDtypeStruct((S * B, DL), F32), jax.ShapeDtypeStruct((S * B, DL), F32)],
        scratch_shapes=[pltpu.VMEM((ROWS, D), BF16),
                        pltpu.VMEM((DL // LANES, PAD_F + ROWS + PAD_B, LANES), F32),
                        pltpu.VMEM((ROWS, DL), F32), pltpu.VMEM((ROWS, DL), F32),
                        pltpu.VMEM((B, DL), F32)],
        compiler_params=_params(("arbitrary",), 56),
        name="mix_in",
    )(x, mod, g1, w_in_b, sg, sw_b, sbias, cw, cb, wg, ba, bi, lam, h0f)


def _mix_out_kernel(x_ref, mod_ref, cx_ref, hf_ref, sgu_ref, gg_ref, wg_ref, ba_ref, bi_ref, lam_ref, h0_ref,
                    wout_ref, g2_ref, wr_ref,
                    x1_ref, hx2t_ref, lg_ref,
                    a_s, b_s, hb_s, hs_ref, mix_ref, carry):
    @pl.when(pl.program_id(0) == 0)
    def _():
        carry[...] = h0_ref[...]

    a, dr = _rglru_coeffs(cx_ref[...], wg_ref, 1, ba_ref[1:2, :], bi_ref[1:2, :], lam_ref[1:2, :])
    a_s[...] = a
    b_s[...] = dr
    carry[...] = _scan(a_s, b_s, hb_s, carry[...], TP, reverse=True)
    hsum = hf_ref[...] + hb_s[...]
    nslab = DL // LANES
    for k in range(nslab):
        hs_ref[k] = hsum[:, k * LANES:(k + 1) * LANES]
    for b in range(B):
        rs = slice(b * TP, (b + 1) * TP)
        mix_ref[rs, 0:DS] = sgu_ref[b]
        for k in range(nslab):
            hk = hs_ref[k, pl.ds(b, TP, stride=B), :]
            gk = gg_ref[b, :, k * LANES:(k + 1) * LANES].astype(F32)
            mix_ref[rs, DS + k * LANES:DS + (k + 1) * LANES] = (gk * hk).astype(BF16)

    y = jnp.dot(mix_ref[...], wout_ref[...], preferred_element_type=F32)
    g2 = g2_ref[...]
    wr = wr_ref[...]
    wr_hi = wr.astype(BF16)
    wr_lo = (wr - wr_hi.astype(F32)).astype(BF16)
    for b in range(B):
        g1x = mod_ref[b:b + 1, 2 * D:3 * D]
        sh2 = mod_ref[b:b + 1, 3 * D:4 * D]
        gs2 = g2 * (1.0 + mod_ref[b:b + 1, 4 * D:5 * D])
        x1 = x_ref[b] + g1x * y[b * TP:(b + 1) * TP, :]
        x1_ref[b] = x1
        hx2 = _rms_mod(x1, gs2, sh2)
        for k in range(SUB_PER_TOK):
            hx2t_ref[b, pl.ds(k, TP, stride=SUB_PER_TOK), :] = hx2[:, k * LANES:(k + 1) * LANES]
        hx_hi = hx2.astype(BF16)
        hx_lo = (hx2 - hx_hi.astype(F32)).astype(BF16)
        nt = (((1,), (1,)), ((), ()))
        lg_ref[b] = (lax.dot_general(wr_hi, hx_hi, nt, preferred_element_type=F32)
                     + (lax.dot_general(wr_hi, hx_lo, nt, preferred_element_type=F32)
                        + lax.dot_general(wr_lo, hx_hi, nt, preferred_element_type=F32)))


def _mix_out(x, mod, cx, hf, sgu, gg, wg, ba, bi, lam, h0b, w_out_b, g2, wr_t):
    full = lambda shape: pl.BlockSpec(shape, lambda i: (0,) * len(shape))
    rev = lambda i: N_TT - 1 - i
    return pl.pallas_call(
        _mix_out_kernel,
        grid=(N_TT,),
        in_specs=[pl.BlockSpec((B, TP, D), lambda i: (0, rev(i), 0)),
                  full((2 * B, N_MOD * D)),
                  pl.BlockSpec((ROWS, DL), lambda i: (rev(i), 0)),
                  pl.BlockSpec((ROWS, DL), lambda i: (rev(i), 0)),
                  pl.BlockSpec((B, TP, DS), lambda i: (0, rev(i), 0)),
                  pl.BlockSpec((B, TP, DL), lambda i: (0, rev(i), 0)),
                  full((2, 2, DL // 2, DL)), full((2, DL)), full((2, DL)), full((2, DL)), full((B, DL)),
                  full((D, D)), full((1, D)), full((E, D))],
        out_specs=[pl.BlockSpec((B, TP, D), lambda i: (0, rev(i), 0)),
                   pl.BlockSpec((B, TP * SUB_PER_TOK, LANES), lambda i: (0, rev(i), 0)),
                   pl.BlockSpec((B, E, TP), lambda i: (0, 0, rev(i)))],
        out_shape=[jax.ShapeDtypeStruct((B, S, D), F32),
                   jax.ShapeDtypeStruct((B, S * SUB_PER_TOK, LANES), F32),
                   jax.ShapeDtypeStruct((B, E, S), F32)],
        scratch_shapes=[pltpu.VMEM((ROWS, DL), F32), pltpu.VMEM((ROWS, DL), F32), pltpu.VMEM((ROWS, DL), F32),
                        pltpu.VMEM((DL // LANES, ROWS, LANES), F32),
                        pltpu.VMEM((ROWS, D), BF16),
                        pltpu.VMEM((B, DL), F32)],
        compiler_params=_params(("arbitrary",), 56),
        name="mix_out",
    )(x, mod, cx, hf, sgu, gg, wg, ba, bi, lam, h0b, w_out_b, g2, wr_t)


NBLK = S // LANES
BIG = 1.0e9


def _bf16_parts(a):
    hi = a.astype(BF16).astype(F32)
    r1 = a - hi
    mid = r1.astype(BF16).astype(F32)
    lo = (r1 - mid).astype(BF16).astype(F32)
    return hi, mid, lo


def _route_kernel(lg_ref, idx_ref, gate_ref, aff_s, lci_s, offb_s, ahi_s, amid_s, alo_s, offi_s):
    for b in range(B):
        l = lg_ref[b]
        ex = jnp.exp(l - jnp.max(l, axis=0, keepdims=True))
        aff_s[b * E:(b + 1) * E, :] = ex / jnp.sum(ex, axis=0, keepdims=True)
    aff = aff_s[...]
    nrow = B * E

    def bisect(_, lohi):
        lo, hi = lohi
        mid = lo + ((hi - lo + 1) >> 1)
        cnt = jnp.sum(jnp.where(aff >= pltpu.bitcast(mid, F32), 1.0, 0.0), axis=1, keepdims=True)
        ge = cnt >= float(CAP)
        return jnp.where(ge, mid, lo), jnp.where(ge, hi, mid - 1)

    lo0 = jnp.zeros((nrow, 1), jnp.int32)
    hi0 = jnp.full((nrow, 1), 0x7F800000, jnp.int32)
    thr_bits, _ = lax.fori_loop(0, 31, bisect, (lo0, hi0))
    thr = pltpu.bitcast(thr_bits, F32)
    gt = aff > thr
    eq = aff == thr
    need = float(CAP) - jnp.sum(jnp.where(gt, 1.0, 0.0), axis=1, keepdims=True)

    qi = lax.broadcasted_iota(jnp.int32, (LANES, LANES), 0)
    ti = lax.broadcasted_iota(jnp.int32, (LANES, LANES), 1)
    tri = jnp.where(qi <= ti, 1.0, 0.0).astype(BF16)
    blocks = [slice(j * LANES, (j + 1) * LANES) for j in range(NBLK)]

    ties_before = jnp.zeros((nrow, 1), F32)
    sel = []
    for sl in blocks:
        eqb = jnp.where(eq[:, sl], 1.0, 0.0)
        incl = jnp.dot(eqb.astype(BF16), tri, preferred_element_type=F32) + ties_before
        sel.append(gt[:, sl] | (eq[:, sl] & ((incl - eqb) < need)))
        ties_before = incl[:, LANES - 1:LANES]

    lane_sq = lax.broadcasted_iota(jnp.int32, (nrow, LANES), 1)
    off = jnp.zeros((nrow, 1), F32)
    offi = jnp.full((nrow, LANES), BIG, F32)
    for j, sl in enumerate(blocks):
        rows = slice(j * nrow, (j + 1) * nrow)
        lci = jnp.dot(jnp.where(sel[j], 1.0, 0.0).astype(BF16), tri, preferred_element_type=F32)
        lci_s[rows, :] = lci
        offb_s[rows, :] = jnp.broadcast_to(off, (nrow, LANES))
        ahi_s[rows, :], amid_s[rows, :], alo_s[rows, :] = _bf16_parts(aff[:, sl])
        off = off + lci[:, LANES - 1:LANES]
        offi = jnp.where(lane_sq == j, off, offi)
    offi_s[...] = offi

    slot = lax.broadcasted_iota(jnp.int32, (CAP, LANES), 0).astype(F32)
    lane = lax.broadcasted_iota(jnp.int32, (CAP, LANES), 1)
    lane_f = lane.astype(F32)
    zpad = jnp.zeros((LANES - NBLK, 5 * LANES), BF16)

    def row_body(r, carry):
        idxm, gm = carry
        take = lambda ref: ref[pl.ds(r, NBLK, stride=nrow), :]
        table = jnp.concatenate([take(lci_s), take(offb_s), take(ahi_s), take(amid_s), take(alo_s)], axis=1)
        table = jnp.concatenate([table.astype(BF16), zpad], axis=0)
        blk = jnp.sum(jnp.where(offi_s[pl.ds(r, 1), :] <= slot, 1.0, 0.0), axis=1, keepdims=True)
        pick = jnp.where(lane_f == blk, 1.0, 0.0).astype(BF16)
        res = jnp.dot(pick, table, preferred_element_type=F32)
        rank1 = (slot + 1.0) - res[:, LANES:2 * LANES]
        tokl = jnp.sum(jnp.where(res[:, 0:LANES] < rank1, 1.0, 0.0), axis=1, keepdims=True)
        affs = (res[:, 2 * LANES:3 * LANES] + res[:, 3 * LANES:4 * LANES]) + res[:, 4 * LANES:5 * LANES]
        gv = jnp.sum(jnp.where(lane_f == tokl, affs, 0.0), axis=1, keepdims=True)
        iv = blk * float(LANES) + tokl
        put = lane == r
        return jnp.where(put, iv, idxm), jnp.where(put, gv, gm)

    z = jnp.zeros((CAP, LANES), F32)
    idxm, gm = lax.fori_loop(0, nrow, row_body, (z, z), unroll=8)
    idx_ref[...] = (idxm.T * float(SUB_PER_TOK)).astype(jnp.int32)
    gate_ref[...] = gm


def _route(logits_t):
    full = lambda shape: pl.BlockSpec(shape, lambda i: (0,) * len(shape))
    return pl.pallas_call(
        _route_kernel,
        grid=(1,),
        in_specs=[full((B, E, S))],
        out_specs=[full((B * E, CAP)), full((CAP, B * E))],
        out_shape=[jax.ShapeDtypeStruct((B * E, CAP), jnp.int32), jax.ShapeDtypeStruct((CAP, B * E), F32)],
        scratch_shapes=[pltpu.VMEM((B * E, S), F32)] + [pltpu.VMEM((NBLK * B * E, LANES), F32)] * 5
                       + [pltpu.VMEM((B * E, LANES), F32)],
        compiler_params=_params(("arbitrary",), 48),
        name="route",
    )(logits_t)


def _dispatch_kernel(idx_ref, h_ref, xg_ref, xt0, xt1):
    for e in range(E):
        xt = xt0 if e % 2 == 0 else xt1
        for s in range(CAP):
            src = pl.multiple_of(idx_ref[e * CAP + s], SUB_PER_TOK)
            xt[s * SUB_PER_TOK:(s + 1) * SUB_PER_TOK, :] = h_ref[0, pl.ds(src, SUB_PER_TOK), :]
        for k in range(SUB_PER_TOK):
            xg_ref[e, :, k * LANES:(k + 1) * LANES] = xt[pl.ds(k, CAP, stride=SUB_PER_TOK), :].astype(BF16)


def _dispatch(idx1, hx2t):
    return pl.pallas_call(
        _dispatch_kernel,
        grid=(B,),
        in_specs=[pl.BlockSpec((E * CAP,), lambda b: (b,), memory_space=pltpu.SMEM),
                  pl.BlockSpec((1, S * SUB_PER_TOK, LANES), lambda b: (b, 0, 0))],
        out_specs=pl.BlockSpec((E, CAP, D), lambda b: (0, b, 0)),
        out_shape=jax.ShapeDtypeStruct((E, B * CAP, D), BF16),
        scratch_shapes=[pltpu.VMEM((CAP * SUB_PER_TOK, LANES), F32), pltpu.VMEM((CAP * SUB_PER_TOK, LANES), F32)],
        compiler_params=_params(("arbitrary",), 40),
        name="dispatch",
    )(idx1, hx2t)


def _moe_kernel(x_ref, gate_ref, w1_ref, w3_ref, w2_ref, y_ref, hid_s, w2_s):
    f = pl.program_id(1)
    w1 = w1_ref[0].astype(BF16)
    w3 = w3_ref[0].astype(BF16)
    fs = pl.multiple_of(f * FN, FN)
    w2_s[pl.ds(fs, FN), :] = w2_ref[0].astype(BF16)
    for c in range(B * CAP // MC):
        rs = slice(c * MC, (c + 1) * MC)
        x = x_ref[0, rs, :]
        h1 = jnp.dot(x, w1, preferred_element_type=F32)
        h3 = jnp.dot(x, w3, preferred_element_type=F32)
        hid_s[f, rs, :] = ((h1 * jax.nn.sigmoid(h1)) * h3).astype(BF16)

    @pl.when(f == FF // FN - 1)
    def _():
        gates = gate_ref[...]
        lane = lax.broadcasted_iota(jnp.int32, gates.shape, 1)
        e = pl.program_id(0)
        for c in range(B * CAP // MC):
            rs = slice(c * MC, (c + 1) * MC)
            hid = jnp.concatenate([hid_s[j, rs, :] for j in range(FF // FN)], axis=1)
            y = jnp.dot(hid, w2_s[...], preferred_element_type=F32)
            for bb in range(MC // CAP):
                b = c * (MC // CAP) + bb
                gcol = jnp.sum(jnp.where(lane == b * E + e, gates, 0.0), axis=1, keepdims=True)
                yb = y[bb * CAP:(bb + 1) * CAP, :] * gcol
                for k in range(SUB_PER_TOK):
                    y_ref[0, pl.ds(b * CAP * SUB_PER_TOK + k, CAP, stride=SUB_PER_TOK), :] = yb[:, k * LANES:(k + 1) * LANES]


def _moe(xg, gate_cols, w1, w3, w2):
    return pl.pallas_call(
        _moe_kernel,
        grid=(E, FF // FN),
        in_specs=[pl.BlockSpec((1, B * CAP, D), lambda e, f: (e, 0, 0)),
                  pl.BlockSpec((CAP, B * E), lambda e, f: (0, 0)),
                  pl.BlockSpec((1, D, FN), lambda e, f: (e, 0, f)),
                  pl.BlockSpec((1, D, FN), lambda e, f: (e, 0, f)),
                  pl.BlockSpec((1, FN, D), lambda e, f: (e, f, 0))],
        out_specs=pl.BlockSpec((1, B * CAP * SUB_PER_TOK, LANES), lambda e, f: (e, 0, 0)),
        out_shape=jax.ShapeDtypeStruct((E, B * CAP * SUB_PER_TOK, LANES), F32),
        scratch_shapes=[pltpu.VMEM((FF // FN, B * CAP, FN), BF16), pltpu.VMEM((FF, D), BF16)],
        compiler_params=_params(("arbitrary", "arbitrary"), 58),
        name="moe",
    )(xg, gate_cols, w1, w3, w2)


def _combine_kernel(idx_ref, y_ref, x1_ref, mod_ref, fg_ref, out_ref, acc):
    b = pl.program_id(0)
    g = pl.program_id(1)

    @pl.when(g == 0)
    def _():
        acc[...] = jnp.zeros(acc.shape, F32)

    @pl.when(g < NG)
    def _():
        for el in range(EG):
            def ids_at(s0):
                return tuple(idx_ref[el * CAP + s0 + j] for j in range(SCATTER_BATCH))

            def body(i, ids):
                s0 = i * SCATTER_BATCH
                dsts, news = [], []
                for j in range(SCATTER_BATCH):
                    dst = pl.multiple_of(ids[j], SUB_PER_TOK)
                    src = pl.multiple_of((s0 + j) * SUB_PER_TOK, SUB_PER_TOK)
                    dsts.append(dst)
                    news.append(acc[pl.ds(dst, SUB_PER_TOK), :] + y_ref[el, pl.ds(src, SUB_PER_TOK), :])
                nxt = ids_at(jnp.minimum(s0 + SCATTER_BATCH, CAP - SCATTER_BATCH))
                for dst, new in zip(dsts, news):
                    acc[pl.ds(dst, SUB_PER_TOK), :] = new
                return nxt

            lax.fori_loop(0, CAP // SCATTER_BATCH, body, ids_at(0))

    @pl.when(g >= NG)
    def _():
        row0 = pl.multiple_of((g - NG) * (FIN_ROWS * SUB_PER_TOK), FIN_ROWS * SUB_PER_TOK)
        g2x = mod_ref[pl.ds(b, 1), 5 * D:6 * D]
        ssq = jnp.zeros((FIN_ROWS, 1), F32)
        for k in range(SUB_PER_TOK):
            ls = slice(k * LANES, (k + 1) * LANES)
            xo = x1_ref[0, :, ls] + g2x[:, ls] * acc[pl.ds(row0 + k, FIN_ROWS, stride=SUB_PER_TOK), :]
            out_ref[0, :, ls] = xo
            ssq = ssq + jnp.sum(xo * xo, axis=1, keepdims=True)
        inv = lax.rsqrt(ssq * (1.0 / D) + EPS)
        out_ref[0] = (out_ref[0] * inv) * fg_ref[...]


def _combine(idx1, y, x1, mod, fg):
    fin = lambda b, g: (b, jnp.maximum(g - NG, 0), 0)
    grp = lambda g: jnp.minimum(g, NG - 1)
    return pl.pallas_call(
        _combine_kernel,
        grid=(B, NG + S // FIN_ROWS),
        in_specs=[pl.BlockSpec((EG * CAP,), lambda b, g: (b * NG + grp(g),), memory_space=pltpu.SMEM),
                  pl.BlockSpec((EG, CAP * SUB_PER_TOK, LANES), lambda b, g: (grp(g), b, 0)),
                  pl.BlockSpec((1, FIN_ROWS, D), fin),
                  pl.BlockSpec((2 * B, N_MOD * D), lambda b, g: (0, 0)),
                  pl.BlockSpec((1, D), lambda b, g: (0, 0))],
        out_specs=pl.BlockSpec((1, FIN_ROWS, D), fin),
        out_shape=jax.ShapeDtypeStruct((B, S, D), F32),
        scratch_shapes=[pltpu.VMEM((S * SUB_PER_TOK, LANES), F32)],
        compiler_params=_params(("arbitrary", "arbitrary"), 56),
        name="combine",
    )(idx1, y, x1, mod, fg)


def _pack_gate_weights(wa, wi):
    eye = jnp.eye(4, dtype=wa.dtype)

    def bdiag(w4):
        return jnp.einsum('hij,hg->higj', w4, eye).reshape(4 * LRU_HD, 4 * LRU_HD)

    dirs = []
    for d in range(2):
        halves = []
        for k in range(2):
            hs = slice(4 * k, 4 * (k + 1))
            halves.append(jnp.concatenate([bdiag(wa[d, hs]), bdiag(wi[d, hs])], axis=1))
        dirs.append(jnp.stack(halves))
    return jnp.stack(dirs).astype(BF16)


def kernel(x, c, ctx, c_ctx, w_mod, b_mod, norm1_g, norm2_g, w_in, sgu_g, sgu_w, sgu_b, conv_w, conv_b,
           rg_wa, rg_ba, rg_wi, rg_bi, rg_lam, w_out, w_router, w1, w3, w2, final_g):
    assert x.shape == (B, S, D) and ctx.shape == (B, LC, D) and w_mod.shape[0] == 1

    cc = jnp.concatenate([c, c_ctx[None, :], jnp.zeros((B - 1, D), F32)], axis=0)
    mod = _modulation(cc, w_mod[0], b_mod[0][None, :])

    g1 = norm1_g[0][None, :]
    g2 = norm2_g[0][None, :]
    w_in_b = w_in[0].astype(BF16)
    w_in_x = w_in_b[:, 2 * DS:2 * DS + DL]
    wg = _pack_gate_weights(rg_wa[0], rg_wi[0])
    cw = conv_w[0]
    cb = conv_b[0][None, :]
    ba, bi, lam = rg_ba[0], rg_bi[0], rg_lam[0]
    sbias = jnp.repeat(sgu_b[0].T, HD, axis=1)

    h0f, h0b = _context_states(ctx, mod, g1, w_in_x, cw, cb, wg, ba, bi, lam)
    sgu, gg, cx, hf = _mix_in(x, mod, g1, w_in_b, sgu_g[0][None, :], sgu_w[0].astype(BF16), sbias, cw, cb,
                              wg, ba, bi, lam, h0f)
    x1, hx2t, logits_t = _mix_out(x, mod, cx, hf, sgu, gg, wg, ba, bi, lam, h0b, w_out[0].astype(BF16), g2,
                                  w_router[0].T)
    idx, gate = _route(logits_t)
    idx1 = idx.reshape(B * E * CAP)
    xg = _dispatch(idx1, hx2t)
    y = _moe(xg, gate, w1[0], w3[0], w2[0])
    return _combine(idx1, y, x1, mod, final_g[None, :])
```

```python
import jax
import jax.numpy as jnp
from jax import lax
from jax.experimental import pallas as pl
from jax.experimental.pallas import tpu as pltpu

F32 = jnp.float32
BF16 = jnp.bfloat16
HIGHEST = lax.Precision.HIGHEST

D = 1024
B = 8
S = 2048
LC = 256
GRID_W = 64
DS = 512
NH = 4
HD = DS // NH
CHUNK = 128
DL = 512
LRU_HEADS = 8
LRU_HD = DL // LRU_HEADS
E = 16
CAP = 2 * S // E
FF = 2048
N_MOD = 6
EPS = 1e-6
RG_C = 8.0

SUBLANES = 8
LANES = 128
VMEM_LIMIT_V7X = 60000 * 1024

TP = CHUNK
ROWS = TP * B
N_TT = S // TP
SUB_PER_TOK = D // LANES
FN = 512
MC = 512
SCATTER_BATCH = 16
EG = 8
NG = E // EG
FIN_ROWS = 1024
PAD_F = SUBLANES
PAD_B = 2 * SUBLANES


def _params(sem, vmem_mb):
    return pltpu.CompilerParams(dimension_semantics=sem, vmem_limit_bytes=min(vmem_mb << 20, VMEM_LIMIT_V7X))


GELU_C1 = 0.7978845608028654
GELU_C2 = GELU_C1 * 0.044715


def _gelu(x):
    half = 0.5 * x
    return half + half * jnp.tanh(x * (GELU_C1 + GELU_C2 * (x * x)))


def _rms_mod(x, gs, sh):
    return (x * lax.rsqrt(jnp.mean(x * x, axis=-1, keepdims=True) + EPS)) * gs + sh


def _log_sigmoid(x):
    return -(jnp.maximum(-x, 0.0) + jnp.log1p(jnp.exp(-jnp.abs(x))))


def _conv_interleaved(xx, xi_ref, tp, period, cw, cb):
    rows = tp * B
    nslab = DL // LANES
    xi_ref[:, 0:PAD_F, :] = jnp.zeros((nslab, PAD_F, LANES), F32)
    xi_ref[:, PAD_F + rows:PAD_F + rows + PAD_B, :] = jnp.zeros((nslab, PAD_B, LANES), F32)
    for k in range(nslab):
        for b in range(B):
            xi_ref[k, pl.ds(PAD_F + b, tp, stride=B), :] = xx[b * tp:(b + 1) * tp, k * LANES:(k + 1) * LANES]
    pos = lax.broadcasted_iota(jnp.int32, (rows, LANES), 0) >> 3
    pm = pos & (period - 1)
    m0 = pm != 0
    m2 = pm != period - 1
    m3 = pm < period - 2
    outs = []
    for k in range(nslab):
        w = cw[:, k * LANES:(k + 1) * LANES]
        t0 = xi_ref[k, 0:rows, :]
        t1 = xi_ref[k, SUBLANES:SUBLANES + rows, :]
        t2 = xi_ref[k, 2 * SUBLANES:2 * SUBLANES + rows, :]
        t3 = xi_ref[k, 3 * SUBLANES:3 * SUBLANES + rows, :]
        acc = jnp.where(m0, t0, 0.0) * w[0:1] + t1 * w[1:2]
        acc = acc + jnp.where(m2, t2, 0.0) * w[2:3] + jnp.where(m3, t3, 0.0) * w[3:4]
        outs.append(acc + cb[:, k * LANES:(k + 1) * LANES])
    return jnp.concatenate(outs, axis=1)


def _rglru_coeffs(cx, wg_ref, d, ba, bi, lam):
    cxb = cx.astype(BF16)
    half = DL // 2
    a_parts, b_parts = [], []
    for k in range(2):
        sl = slice(half * k, half * (k + 1))
        z = jnp.dot(cxb[:, sl], wg_ref[d, k], preferred_element_type=F32)
        r = jax.nn.sigmoid(z[:, :half] + ba[:, sl])
        i = jax.nn.sigmoid(z[:, half:] + bi[:, sl])
        neg_log_a = r * (RG_C * -_log_sigmoid(lam[:, sl]))
        a = jnp.exp(-neg_log_a)
        one_minus_a2 = jnp.tanh(neg_log_a) * (a * a + 1.0)
        root = jnp.where(one_minus_a2 > 0.0, one_minus_a2 * lax.rsqrt(one_minus_a2), 0.0)
        a_parts.append(a)
        b_parts.append(root * (i * cx[:, sl]))
    return jnp.concatenate(a_parts, axis=1), jnp.concatenate(b_parts, axis=1)


def _scan(a_ref, b_ref, h_out_ref, h0, nsteps, reverse):
    def body(j, h):
        p = (nsteps - 1 - j) if reverse else j
        r0 = pl.multiple_of(p * SUBLANES, SUBLANES)
        h = a_ref[pl.ds(r0, SUBLANES), :] * h + b_ref[pl.ds(r0, SUBLANES), :]
        if h_out_ref is not None:
            h_out_ref[pl.ds(r0, SUBLANES), :] = h
        return h

    return lax.fori_loop(0, nsteps, body, h0, unroll=8)


def _mod_kernel(c_ref, w_ref, b_ref, o_ref):
    c = c_ref[...]
    s = c * jax.nn.sigmoid(c)
    o_ref[...] = jnp.dot(s, w_ref[...], precision=HIGHEST, preferred_element_type=F32) + b_ref[...]


def _modulation(cc, w_mod, b_mod):
    nt = 4
    tn = N_MOD * D // nt
    return pl.pallas_call(
        _mod_kernel,
        grid=(nt,),
        in_specs=[pl.BlockSpec((2 * B, D), lambda i: (0, 0)),
                  pl.BlockSpec((D, tn), lambda i: (0, i)),
                  pl.BlockSpec((1, tn), lambda i: (0, i))],
        out_specs=pl.BlockSpec((2 * B, tn), lambda i: (0, i)),
        out_shape=jax.ShapeDtypeStruct((2 * B, N_MOD * D), F32),
        compiler_params=_params(("arbitrary",), 32),
        name="mod",
    )(cc, w_mod, b_mod)


def _ctx_kernel(ctx_ref, mod_ref, g1_ref, win_ref, cw_ref, cb_ref, wg_ref, ba_ref, bi_ref, lam_ref,
                hf_ref, hb_ref, xi_ref, a_s, b_s):
    sh = mod_ref[B:B + 1, 0:D]
    gs = g1_ref[...] * (1.0 + mod_ref[B:B + 1, D:2 * D])
    parts = []
    for b in range(B):
        hc = _rms_mod(ctx_ref[b], gs, sh).astype(BF16)
        parts.append(jnp.dot(hc, win_ref[...], preferred_element_type=F32))
    xx = jnp.concatenate(parts, axis=0)
    cx = _conv_interleaved(xx, xi_ref, LC, LC, cw_ref[...], cb_ref[...])
    for d in range(2):
        a, dr = _rglru_coeffs(cx, wg_ref, d, ba_ref[d:d + 1, :], bi_ref[d:d + 1, :], lam_ref[d:d + 1, :])
        a_s[...] = a
        b_s[...] = dr
        h = _scan(a_s, b_s, None, jnp.zeros((B, DL), F32), LC, reverse=(d == 1))
        if d == 0:
            hf_ref[...] = h
        else:
            hb_ref[...] = h


def _context_states(ctx, mod, g1, w_in_x, cw, cb, wg, ba, bi, lam):
    rows = LC * B
    full = lambda shape: pl.BlockSpec(shape, lambda i: (0,) * len(shape))
    return pl.pallas_call(
        _ctx_kernel,
        grid=(1,),
        in_specs=[full((B, LC, D)), full((2 * B, N_MOD * D)), full((1, D)), full((D, DL)), full((4, DL)),
                  full((1, DL)), full((2, 2, DL // 2, DL)), full((2, DL)), full((2, DL)), full((2, DL))],
        out_specs=[full((B, DL)), full((B, DL))],
        out_shape=[jax.ShapeDtypeStruct((B, DL), F32)] * 2,
        scratch_shapes=[pltpu.VMEM((DL // LANES, PAD_F + rows + PAD_B, LANES), F32),
                        pltpu.VMEM((rows, DL), F32), pltpu.VMEM((rows, DL), F32)],
        compiler_params=_params(("arbitrary",), 56),
        name="ctx",
    )(ctx, mod, g1, w_in_x, cw, cb, wg, ba, bi, lam)


def _mix_in_kernel(x_ref, mod_ref, g1_ref, win_ref, sg_ref, sw_ref, sbias_ref, cw_ref, cb_ref, wg_ref,
                   ba_ref, bi_ref, lam_ref, h0_ref,
                   sgu_ref, gg_ref, cx_ref, hf_ref,
                   hbuf, xi_ref, a_s, b_s, carry):
    @pl.when(pl.program_id(0) == 0)
    def _():
        carry[...] = h0_ref[...]

    g1 = g1_ref[...]
    for b in range(B):
        sh = mod_ref[b:b + 1, 0:D]
        gs = g1 * (1.0 + mod_ref[b:b + 1, D:2 * D])
        hbuf[b * TP:(b + 1) * TP, :] = _rms_mod(x_ref[b], gs, sh).astype(BF16)
    z = jnp.dot(hbuf[...], win_ref[...], preferred_element_type=F32)
    ug = _gelu(z[:, 0:DS])
    vg = _gelu(z[:, DS:2 * DS])
    xx = z[:, 2 * DS:2 * DS + DL]
    gg = _gelu(z[:, 2 * DS + DL:])
    for b in range(B):
        gg_ref[b] = gg[b * TP:(b + 1) * TP, :].astype(BF16)

    for h in range(NH):
        hs = slice(h * HD, (h + 1) * HD)
        vh = vg[:, hs]
        vn = (vh * lax.rsqrt(jnp.mean(vh * vh, axis=-1, keepdims=True) + EPS)) * sg_ref[:, hs]
        vnb = vn.astype(BF16)
        for b in range(B):
            rs = slice(b * TP, (b + 1) * TP)
            s = jnp.dot(sw_ref[h], vnb[rs], preferred_element_type=F32) + sbias_ref[:, hs]
            sgu_ref[b, :, hs] = (ug[rs, hs] * s).astype(BF16)

    cx = _conv_interleaved(xx, xi_ref, TP, GRID_W, cw_ref[...], cb_ref[...])
    cx_ref[...] = cx
    a, dr = _rglru_coeffs(cx, wg_ref, 0, ba_ref[0:1, :], bi_ref[0:1, :], lam_ref[0:1, :])
    a_s[...] = a
    b_s[...] = dr
    carry[...] = _scan(a_s, b_s, hf_ref, carry[...], TP, reverse=False)


def _mix_in(x, mod, g1, w_in_b, sg, sw_b, sbias, cw, cb, wg, ba, bi, lam, h0f):
    full = lambda shape: pl.BlockSpec(shape, lambda i: (0,) * len(shape))
    return pl.pallas_call(
        _mix_in_kernel,
        grid=(N_TT,),
        in_specs=[pl.BlockSpec((B, TP, D), lambda i: (0, i, 0)),
                  full((2 * B, N_MOD * D)), full((1, D)), full((D, 2 * DS + 2 * DL)), full((1, DS)),
                  full((NH, CHUNK, CHUNK)), full((CHUNK, DS)), full((4, DL)), full((1, DL)),
                  full((2, 2, DL // 2, DL)), full((2, DL)), full((2, DL)), full((2, DL)), full((B, DL))],
        out_specs=[pl.BlockSpec((B, TP, DS), lambda i: (0, i, 0)),
                   pl.BlockSpec((B, TP, DL), lambda i: (0, i, 0)),
                   pl.BlockSpec((ROWS, DL), lambda i: (i, 0)),
                   pl.BlockSpec((ROWS, DL), lambda i: (i, 0))],
        out_shape=[jax.ShapeDtypeStruct((B, S, DS), BF16), jax.ShapeDtypeStruct((B, S, DL), BF16),
                   jax.ShapeDtypeStruct((S * B, DL), F32), jax.ShapeDtypeStruct((S * B, DL), F32)],
        scratch_shapes=[pltpu.VMEM((ROWS, D), BF16),
                        pltpu.VMEM((DL // LANES, PAD_F + ROWS + PAD_B, LANES), F32),
                        pltpu.VMEM((ROWS, DL), F32), pltpu.VMEM((ROWS, DL), F32),
                        pltpu.VMEM((B, DL), F32)],
        compiler_params=_params(("arbitrary",), 56),
        name="mix_in",
    )(x, mod, g1, w_in_b, sg, sw_b, sbias, cw, cb, wg, ba, bi, lam, h0f)


def _mix_out_kernel(x_ref, mod_ref, cx_ref, hf_ref, sgu_ref, gg_ref, wg_ref, ba_ref, bi_ref, lam_ref, h0_ref,
                    wout_ref, g2_ref, wr_ref,
                    x1_ref, hx2t_ref, lg_ref,
                    a_s, b_s, hb_s, hs_ref, mix_ref, carry):
    @pl.when(pl.program_id(0) == 0)
    def _():
        carry[...] = h0_ref[...]

    a, dr = _rglru_coeffs(cx_ref[...], wg_ref, 1, ba_ref[1:2, :], bi_ref[1:2, :], lam_ref[1:2, :])
    a_s[...] = a
    b_s[...] = dr
    carry[...] = _scan(a_s, b_s, hb_s, carry[...], TP, reverse=True)
    hsum = hf_ref[...] + hb_s[...]
    nslab = DL // LANES
    for k in range(nslab):
        hs_ref[k] = hsum[:, k * LANES:(k + 1) * LANES]
    for b in range(B):
        rs = slice(b * TP, (b + 1) * TP)
        mix_ref[rs, 0:DS] = sgu_ref[b]
        for k in range(nslab):
            hk = hs_ref[k, pl.ds(b, TP, stride=B), :]
            gk = gg_ref[b, :, k * LANES:(k + 1) * LANES].astype(F32)
            mix_ref[rs, DS + k * LANES:DS + (k + 1) * LANES] = (gk * hk).astype(BF16)

    y = jnp.dot(mix_ref[...], wout_ref[...], preferred_element_type=F32)
    g2 = g2_ref[...]
    wr = wr_ref[...]
    wr_hi = wr.astype(BF16)
    wr_lo = (wr - wr_hi.astype(F32)).astype(BF16)
    for b in range(B):
        g1x = mod_ref[b:b + 1, 2 * D:3 * D]
        sh2 = mod_ref[b:b + 1, 3 * D:4 * D]
        gs2 = g2 * (1.0 + mod_ref[b:b + 1, 4 * D:5 * D])
        x1 = x_ref[b] + g1x * y[b * TP:(b + 1) * TP, :]
        x1_ref[b] = x1
        hx2 = _rms_mod(x1, gs2, sh2)
        for k in range(SUB_PER_TOK):
            hx2t_ref[b, pl.ds(k, TP, stride=SUB_PER_TOK), :] = hx2[:, k * LANES:(k + 1) * LANES]
        hx_hi = hx2.astype(BF16)
        hx_lo = (hx2 - hx_hi.astype(F32)).astype(BF16)
        nt = (((1,), (1,)), ((), ()))
        lg_ref[b] = (lax.dot_general(wr_hi, hx_hi, nt, preferred_element_type=F32)
                     + (lax.dot_general(wr_hi, hx_lo, nt, preferred_element_type=F32)
                        + lax.dot_general(wr_lo, hx_hi, nt, preferred_element_type=F32)))


def _mix_out(x, mod, cx, hf, sgu, gg, wg, ba, bi, lam, h0b, w_out_b, g2, wr_t):
    full = lambda shape: pl.BlockSpec(shape, lambda i: (0,) * len(shape))
    rev = lambda i: N_TT - 1 - i
    return pl.pallas_call(
        _mix_out_kernel,
        grid=(N_TT,),
        in_specs=[pl.BlockSpec((B, TP, D), lambda i: (0, rev(i), 0)),
                  full((2 * B, N_MOD * D)),
                  pl.BlockSpec((ROWS, DL), lambda i: (rev(i), 0)),
                  pl.BlockSpec((ROWS, DL), lambda i: (rev(i), 0)),
                  pl.BlockSpec((B, TP, DS), lambda i: (0, rev(i), 0)),
                  pl.BlockSpec((B, TP, DL), lambda i: (0, rev(i), 0)),
                  full((2, 2, DL // 2, DL)), full((2, DL)), full((2, DL)), full((2, DL)), full((B, DL)),
                  full((D, D)), full((1, D)), full((E, D))],
        out_specs=[pl.BlockSpec((B, TP, D), lambda i: (0, rev(i), 0)),
                   pl.BlockSpec((B, TP * SUB_PER_TOK, LANES), lambda i: (0, rev(i), 0)),
                   pl.BlockSpec((B, E, TP), lambda i: (0, 0, rev(i)))],
        out_shape=[jax.ShapeDtypeStruct((B, S, D), F32),
                   jax.ShapeDtypeStruct((B, S * SUB_PER_TOK, LANES), F32),
                   jax.ShapeDtypeStruct((B, E, S), F32)],
        scratch_shapes=[pltpu.VMEM((ROWS, DL), F32), pltpu.VMEM((ROWS, DL), F32), pltpu.VMEM((ROWS, DL), F32),
                        pltpu.VMEM((DL // LANES, ROWS, LANES), F32),
                        pltpu.VMEM((ROWS, D), BF16),
                        pltpu.VMEM((B, DL), F32)],
        compiler_params=_params(("arbitrary",), 56),
        name="mix_out",
    )(x, mod, cx, hf, sgu, gg, wg, ba, bi, lam, h0b, w_out_b, g2, wr_t)


NBLK = S // LANES
BIG = 1.0e9


def _bf16_parts(a):
    hi = a.astype(BF16).astype(F32)
    r1 = a - hi
    mid = r1.astype(BF16).astype(F32)
    lo = (r1 - mid).astype(BF16).astype(F32)
    return hi, mid, lo


def _route_kernel(lg_ref, idx_ref, gate_ref, aff_s, lci_s, offb_s, ahi_s, amid_s, alo_s, offi_s):
    for b in range(B):
        l = lg_ref[b]
        ex = jnp.exp(l - jnp.max(l, axis=0, keepdims=True))
        aff_s[b * E:(b + 1) * E, :] = ex / jnp.sum(ex, axis=0, keepdims=True)
    aff = aff_s[...]
    nrow = B * E

    def bisect(_, lohi):
        lo, hi = lohi
        mid = lo + ((hi - lo + 1) >> 1)
        cnt = jnp.sum(jnp.where(aff >= pltpu.bitcast(mid, F32), 1.0, 0.0), axis=1, keepdims=True)
        ge = cnt >= float(CAP)
        return jnp.where(ge, mid, lo), jnp.where(ge, hi, mid - 1)

    lo0 = jnp.zeros((nrow, 1), jnp.int32)
    hi0 = jnp.full((nrow, 1), 0x7F800000, jnp.int32)
    thr_bits, _ = lax.fori_loop(0, 31, bisect, (lo0, hi0))
    thr = pltpu.bitcast(thr_bits, F32)
    gt = aff > thr
    eq = aff == thr
    need = float(CAP) - jnp.sum(jnp.where(gt, 1.0, 0.0), axis=1, keepdims=True)

    qi = lax.broadcasted_iota(jnp.int32, (LANES, LANES), 0)
    ti = lax.broadcasted_iota(jnp.int32, (LANES, LANES), 1)
    tri = jnp.where(qi <= ti, 1.0, 0.0).astype(BF16)
    blocks = [slice(j * LANES, (j + 1) * LANES) for j in range(NBLK)]

    ties_before = jnp.zeros((nrow, 1), F32)
    sel = []
    for sl in blocks:
        eqb = jnp.where(eq[:, sl], 1.0, 0.0)
        incl = jnp.dot(eqb.astype(BF16), tri, preferred_element_type=F32) + ties_before
        sel.append(gt[:, sl] | (eq[:, sl] & ((incl - eqb) < need)))
        ties_before = incl[:, LANES - 1:LANES]

    lane_sq = lax.broadcasted_iota(jnp.int32, (nrow, LANES), 1)
    off = jnp.zeros((nrow, 1), F32)
    offi = jnp.full((nrow, LANES), BIG, F32)
    for j, sl in enumerate(blocks):
        rows = slice(j * nrow, (j + 1) * nrow)
        lci = jnp.dot(jnp.where(sel[j], 1.0, 0.0).astype(BF16), tri, preferred_element_type=F32)
        lci_s[rows, :] = lci
        offb_s[rows, :] = jnp.broadcast_to(off, (nrow, LANES))
        ahi_s[rows, :], amid_s[rows, :], alo_s[rows, :] = _bf16_parts(aff[:, sl])
        off = off + lci[:, LANES - 1:LANES]
        offi = jnp.where(lane_sq == j, off, offi)
    offi_s[...] = offi

    slot = lax.broadcasted_iota(jnp.int32, (CAP, LANES), 0).astype(F32)
    lane = lax.broadcasted_iota(jnp.int32, (CAP, LANES), 1)
    lane_f = lane.astype(F32)
    zpad = jnp.zeros((LANES - NBLK, 5 * LANES), BF16)

    def row_body(r, carry):
        idxm, gm = carry
        take = lambda ref: ref[pl.ds(r, NBLK, stride=nrow), :]
        table = jnp.concatenate([take(lci_s), take(offb_s), take(ahi_s), take(amid_s), take(alo_s)], axis=1)
        table = jnp.concatenate([table.astype(BF16), zpad], axis=0)
        blk = jnp.sum(jnp.where(offi_s[pl.ds(r, 1), :] <= slot, 1.0, 0.0), axis=1, keepdims=True)
        pick = jnp.where(lane_f == blk, 1.0, 0.0).astype(BF16)
        res = jnp.dot(pick, table, preferred_element_type=F32)
        rank1 = (slot + 1.0) - res[:, LANES:2 * LANES]
        tokl = jnp.sum(jnp.where(res[:, 0:LANES] < rank1, 1.0, 0.0), axis=1, keepdims=True)
        affs = (res[:, 2 * LANES:3 * LANES] + res[:, 3 * LANES:4 * LANES]) + res[:, 4 * LANES:5 * LANES]
        gv = jnp.sum(jnp.where(lane_f == tokl, affs, 0.0), axis=1, keepdims=True)
        iv = blk * float(LANES) + tokl
        put = lane == r
        return jnp.where(put, iv, idxm), jnp.where(put, gv, gm)

    z = jnp.zeros((CAP, LANES), F32)
    idxm, gm = lax.fori_loop(0, nrow, row_body, (z, z), unroll=8)
    idx_ref[...] = (idxm.T * float(SUB_PER_TOK)).astype(jnp.int32)
    gate_ref[...] = gm


def _route(logits_t):
    full = lambda shape: pl.BlockSpec(shape, lambda i: (0,) * len(shape))
    return pl.pallas_call(
        _route_kernel,
        grid=(1,),
        in_specs=[full((B, E, S))],
        out_specs=[full((B * E, CAP)), full((CAP, B * E))],
        out_shape=[jax.ShapeDtypeStruct((B * E, CAP), jnp.int32), jax.ShapeDtypeStruct((CAP, B * E), F32)],
        scratch_shapes=[pltpu.VMEM((B * E, S), F32)] + [pltpu.VMEM((NBLK * B * E, LANES), F32)] * 5
                       + [pltpu.VMEM((B * E, LANES), F32)],
        compiler_params=_params(("arbitrary",), 48),
        name="route",
    )(logits_t)


def _dispatch_kernel(idx_ref, h_ref, xg_ref, xt0, xt1):
    for e in range(E):
        xt = xt0 if e % 2 == 0 else xt1
        for s in range(CAP):
            src = pl.multiple_of(idx_ref[e * CAP + s], SUB_PER_TOK)
            xt[s * SUB_PER_TOK:(s + 1) * SUB_PER_TOK, :] = h_ref[0, pl.ds(src, SUB_PER_TOK), :]
        for k in range(SUB_PER_TOK):
            xg_ref[e, :, k * LANES:(k + 1) * LANES] = xt[pl.ds(k, CAP, stride=SUB_PER_TOK), :].astype(BF16)


def _dispatch(idx1, hx2t):
    return pl.pallas_call(
        _dispatch_kernel,
        grid=(B,),
        in_specs=[pl.BlockSpec((E * CAP,), lambda b: (b,), memory_space=pltpu.SMEM),
                  pl.BlockSpec((1, S * SUB_PER_TOK, LANES), lambda b: (b, 0, 0))],
        out_specs=pl.BlockSpec((E, CAP, D), lambda b: (0, b, 0)),
        out_shape=jax.ShapeDtypeStruct((E, B * CAP, D), BF16),
        scratch_shapes=[pltpu.VMEM((CAP * SUB_PER_TOK, LANES), F32), pltpu.VMEM((CAP * SUB_PER_TOK, LANES), F32)],
        compiler_params=_params(("arbitrary",), 40),
        name="dispatch",
    )(idx1, hx2t)


def _moe_kernel(x_ref, gate_ref, w1_ref, w3_ref, w2_ref, y_ref, hid_s, w2_s):
    f = pl.program_id(1)
    w1 = w1_ref[0].astype(BF16)
    w3 = w3_ref[0].astype(BF16)
    fs = pl.multiple_of(f * FN, FN)
    w2_s[pl.ds(fs, FN), :] = w2_ref[0].astype(BF16)
    for c in range(B * CAP // MC):
        rs = slice(c * MC, (c + 1) * MC)
        x = x_ref[0, rs, :]
        h1 = jnp.dot(x, w1, preferred_element_type=F32)
        h3 = jnp.dot(x, w3, preferred_element_type=F32)
        hid_s[f, rs, :] = ((h1 * jax.nn.sigmoid(h1)) * h3).astype(BF16)

    @pl.when(f == FF // FN - 1)
    def _():
        gates = gate_ref[...]
        lane = lax.broadcasted_iota(jnp.int32, gates.shape, 1)
        e = pl.program_id(0)
        for c in range(B * CAP // MC):
            rs = slice(c * MC, (c + 1) * MC)
            hid = jnp.concatenate([hid_s[j, rs, :] for j in range(FF // FN)], axis=1)
            y = jnp.dot(hid, w2_s[...], preferred_element_type=F32)
            for bb in range(MC // CAP):
                b = c * (MC // CAP) + bb
                gcol = jnp.sum(jnp.where(lane == b * E + e, gates, 0.0), axis=1, keepdims=True)
                yb = y[bb * CAP:(bb + 1) * CAP, :] * gcol
                for k in range(SUB_PER_TOK):
                    y_ref[0, pl.ds(b * CAP * SUB_PER_TOK + k, CAP, stride=SUB_PER_TOK), :] = yb[:, k * LANES:(k + 1) * LANES]


def _moe(xg, gate_cols, w1, w3, w2):
    return pl.pallas_call(
        _moe_kernel,
        grid=(E, FF // FN),
        in_specs=[pl.BlockSpec((1, B * CAP, D), lambda e, f: (e, 0, 0)),
                  pl.BlockSpec((CAP, B * E), lambda e, f: (0, 0)),
                  pl.BlockSpec((1, D, FN), lambda e, f: (e, 0, f)),
                  pl.BlockSpec((1, D, FN), lambda e, f: (e, 0, f)),
                  pl.BlockSpec((1, FN, D), lambda e, f: (e, f, 0))],
        out_specs=pl.BlockSpec((1, B * CAP * SUB_PER_TOK, LANES), lambda e, f: (e, 0, 0)),
        out_shape=jax.ShapeDtypeStruct((E, B * CAP * SUB_PER_TOK, LANES), F32),
        scratch_shapes=[pltpu.VMEM((FF // FN, B * CAP, FN), BF16), pltpu.VMEM((FF, D), BF16)],
        compiler_params=_params(("arbitrary", "arbitrary"), 58),
        name="moe",
    )(xg, gate_cols, w1, w3, w2)


def _combine_kernel(idx_ref, y_ref, x1_ref, mod_ref, fg_ref, out_ref, acc):
    b = pl.program_id(0)
    g = pl.program_id(1)

    @pl.when(g == 0)
    def _():
        acc[...] = jnp.zeros(acc.shape, F32)

    @pl.when(g < NG)
    def _():
        for el in range(EG):
            for s0 in range(0, CAP, SCATTER_BATCH):
                dsts, news = [], []
                for j in range(SCATTER_BATCH):
                    dst = pl.multiple_of(idx_ref[el * CAP + s0 + j], SUB_PER_TOK)
                    src = (s0 + j) * SUB_PER_TOK
                    dsts.append(dst)
                    news.append(acc[pl.ds(dst, SUB_PER_TOK), :] + y_ref[el, src:src + SUB_PER_TOK, :])
                for dst, new in zip(dsts, news):
                    acc[pl.ds(dst, SUB_PER_TOK), :] = new

    @pl.when(g >= NG)
    def _():
        row0 = pl.multiple_of((g - NG) * (FIN_ROWS * SUB_PER_TOK), FIN_ROWS * SUB_PER_TOK)
        g2x = mod_ref[pl.ds(b, 1), 5 * D:6 * D]
        ssq = jnp.zeros((FIN_ROWS, 1), F32)
        for k in range(SUB_PER_TOK):
            ls = slice(k * LANES, (k + 1) * LANES)
            xo = x1_ref[0, :, ls] + g2x[:, ls] * acc[pl.ds(row0 + k, FIN_ROWS, stride=SUB_PER_TOK), :]
            out_ref[0, :, ls] = xo
            ssq = ssq + jnp.sum(xo * xo, axis=1, keepdims=True)
        inv = lax.rsqrt(ssq * (1.0 / D) + EPS)
        out_ref[0] = (out_ref[0] * inv) * fg_ref[...]


def _combine(idx1, y, x1, mod, fg):
    fin = lambda b, g: (b, jnp.maximum(g - NG, 0), 0)
    grp = lambda g: jnp.minimum(g, NG - 1)
    return pl.pallas_call(
        _combine_kernel,
        grid=(B, NG + S // FIN_ROWS),
        in_specs=[pl.BlockSpec((EG * CAP,), lambda b, g: (b * NG + grp(g),), memory_space=pltpu.SMEM),
                  pl.BlockSpec((EG, CAP * SUB_PER_TOK, LANES), lambda b, g: (grp(g), b, 0)),
                  pl.BlockSpec((1, FIN_ROWS, D), fin),
                  pl.BlockSpec((2 * B, N_MOD * D), lambda b, g: (0, 0)),
                  pl.BlockSpec((1, D), lambda b, g: (0, 0))],
        out_specs=pl.BlockSpec((1, FIN_ROWS, D), fin),
        out_shape=jax.ShapeDtypeStruct((B, S, D), F32),
        scratch_shapes=[pltpu.VMEM((S * SUB_PER_TOK, LANES), F32)],
        compiler_params=_params(("arbitrary", "arbitrary"), 56),
        name="combine",
    )(idx1, y, x1, mod, fg)


def _pack_gate_weights(wa, wi):
    eye = jnp.eye(4, dtype=wa.dtype)

    def bdiag(w4):
        return jnp.einsum('hij,hg->higj', w4, eye).reshape(4 * LRU_HD, 4 * LRU_HD)

    dirs = []
    for d in range(2):
        halves = []
        for k in range(2):
            hs = slice(4 * k, 4 * (k + 1))
            halves.append(jnp.concatenate([bdiag(wa[d, hs]), bdiag(wi[d, hs])], axis=1))
        dirs.append(jnp.stack(halves))
    return jnp.stack(dirs).astype(BF16)


def kernel(x, c, ctx, c_ctx, w_mod, b_mod, norm1_g, norm2_g, w_in, sgu_g, sgu_w, sgu_b, conv_w, conv_b,
           rg_wa, rg_ba, rg_wi, rg_bi, rg_lam, w_out, w_router, w1, w3, w2, final_g):
    assert x.shape == (B, S, D) and ctx.shape == (B, LC, D) and w_mod.shape[0] == 1

    cc = jnp.concatenate([c, c_ctx[None, :], jnp.zeros((B - 1, D), F32)], axis=0)
    mod = _modulation(cc, w_mod[0], b_mod[0][None, :])

    g1 = norm1_g[0][None, :]
    g2 = norm2_g[0][None, :]
    w_in_b = w_in[0].astype(BF16)
    w_in_x = w_in_b[:, 2 * DS:2 * DS + DL]
    wg = _pack_gate_weights(rg_wa[0], rg_wi[0])
    cw = conv_w[0]
    cb = conv_b[0][None, :]
    ba, bi, lam = rg_ba[0], rg_bi[0], rg_lam[0]
    sbias = jnp.repeat(sgu_b[0].T, HD, axis=1)

    h0f, h0b = _context_states(ctx, mod, g1, w_in_x, cw, cb, wg, ba, bi, lam)
    sgu, gg, cx, hf = _mix_in(x, mod, g1, w_in_b, sgu_g[0][None, :], sgu_w[0].astype(BF16), sbias, cw, cb,
                              wg, ba, bi, lam, h0f)
    x1, hx2t, logits_t = _mix_out(x, mod, cx, hf, sgu, gg, wg, ba, bi, lam, h0b, w_out[0].astype(BF16), g2,
                                  w_router[0].T)
    idx, gate = _route(logits_t)
    idx1 = idx.reshape(B * E * CAP)
    xg = _dispatch(idx1, hx2t)
    y = _moe(xg, gate, w1[0], w3[0], w2[0])
    return _combine(idx1, y, x1, mod, final_g[None, :])
```

```python
import jax
import jax.numpy as jnp
from jax import lax
from jax.experimental import pallas as pl
from jax.experimental.pallas import tpu as pltpu

F32 = jnp.float32
BF16 = jnp.bfloat16
HIGHEST = lax.Precision.HIGHEST

D = 1024
B = 8
S = 2048
LC = 256
GRID_W = 64
DS = 512
NH = 4
HD = DS // NH
CHUNK = 128
DL = 512
LRU_HEADS = 8
LRU_HD = DL // LRU_HEADS
E = 16
CAP = 2 * S // E
FF = 2048
N_MOD = 6
EPS = 1e-6
RG_C = 8.0

SUBLANES = 8
LANES = 128
VMEM_LIMIT_V7X = 60000 * 1024

TP = CHUNK
ROWS = TP * B
N_TT = S // TP
SUB_PER_TOK = D // LANES
PROJ_GROUPS = 2
FN = 512
MC = 512
SCATTER_BATCH = 16
EG = 8
NG = E // EG
FIN_ROWS = 1024
PAD_F = SUBLANES
PAD_B = 2 * SUBLANES


def _params(sem, vmem_mb):
    return pltpu.CompilerParams(dimension_semantics=sem, vmem_limit_bytes=min(vmem_mb << 20, VMEM_LIMIT_V7X))


GELU_C1 = 0.7978845608028654
GELU_C2 = GELU_C1 * 0.044715


def _gelu(x):
    half = 0.5 * x
    return half + half * jnp.tanh(x * (GELU_C1 + GELU_C2 * (x * x)))


def _rms_mod(x, gs, sh):
    return (x * lax.rsqrt(jnp.mean(x * x, axis=-1, keepdims=True) + EPS)) * gs + sh


def _log_sigmoid(x):
    return -(jnp.maximum(-x, 0.0) + jnp.log1p(jnp.exp(-jnp.abs(x))))


def _conv_interleaved(xx, xi_ref, tp, period, cw, cb):
    rows = tp * B
    nslab = DL // LANES
    xi_ref[:, 0:PAD_F, :] = jnp.zeros((nslab, PAD_F, LANES), F32)
    xi_ref[:, PAD_F + rows:PAD_F + rows + PAD_B, :] = jnp.zeros((nslab, PAD_B, LANES), F32)
    for k in range(nslab):
        for b in range(B):
            xi_ref[k, pl.ds(PAD_F + b, tp, stride=B), :] = xx[b * tp:(b + 1) * tp, k * LANES:(k + 1) * LANES]
    pos = lax.broadcasted_iota(jnp.int32, (rows, LANES), 0) >> 3
    pm = pos & (period - 1)
    m0 = pm != 0
    m2 = pm != period - 1
    m3 = pm < period - 2
    outs = []
    for k in range(nslab):
        w = cw[:, k * LANES:(k + 1) * LANES]
        t0 = xi_ref[k, 0:rows, :]
        t1 = xi_ref[k, SUBLANES:SUBLANES + rows, :]
        t2 = xi_ref[k, 2 * SUBLANES:2 * SUBLANES + rows, :]
        t3 = xi_ref[k, 3 * SUBLANES:3 * SUBLANES + rows, :]
        acc = jnp.where(m0, t0, 0.0) * w[0:1] + t1 * w[1:2]
        acc = acc + jnp.where(m2, t2, 0.0) * w[2:3] + jnp.where(m3, t3, 0.0) * w[3:4]
        outs.append(acc + cb[:, k * LANES:(k + 1) * LANES])
    return jnp.concatenate(outs, axis=1)


def _rglru_coeffs(cx, wg_ref, d, ba, bi, lam):
    cxb = cx.astype(BF16)
    half = DL // 2
    a_parts, b_parts = [], []
    for k in range(2):
        sl = slice(half * k, half * (k + 1))
        z = jnp.dot(cxb[:, sl], wg_ref[d, k], preferred_element_type=F32)
        r = jax.nn.sigmoid(z[:, :half] + ba[:, sl])
        i = jax.nn.sigmoid(z[:, half:] + bi[:, sl])
        neg_log_a = r * (RG_C * -_log_sigmoid(lam[:, sl]))
        a = jnp.exp(-neg_log_a)
        one_minus_a2 = jnp.tanh(neg_log_a) * (a * a + 1.0)
        root = jnp.where(one_minus_a2 > 0.0, one_minus_a2 * lax.rsqrt(one_minus_a2), 0.0)
        a_parts.append(a)
        b_parts.append(root * (i * cx[:, sl]))
    return jnp.concatenate(a_parts, axis=1), jnp.concatenate(b_parts, axis=1)


def _scan(a, b, h_out_ref, h0, nsteps, reverse):
    h = h0
    for p in (range(nsteps - 1, -1, -1) if reverse else range(nsteps)):
        rs = slice(p * SUBLANES, (p + 1) * SUBLANES)
        h = a[rs, :] * h + b[rs, :]
        if h_out_ref is not None:
            h_out_ref[rs, :] = h
    return h


def _mod_kernel(c_ref, w_ref, b_ref, o_ref):
    c = c_ref[...]
    s = c * jax.nn.sigmoid(c)
    o_ref[...] = jnp.dot(s, w_ref[...], precision=HIGHEST, preferred_element_type=F32) + b_ref[...]


def _modulation(cc, w_mod, b_mod):
    nt = 4
    tn = N_MOD * D // nt
    return pl.pallas_call(
        _mod_kernel,
        grid=(nt,),
        in_specs=[pl.BlockSpec((2 * B, D), lambda i: (0, 0)),
                  pl.BlockSpec((D, tn), lambda i: (0, i)),
                  pl.BlockSpec((1, tn), lambda i: (0, i))],
        out_specs=pl.BlockSpec((2 * B, tn), lambda i: (0, i)),
        out_shape=jax.ShapeDtypeStruct((2 * B, N_MOD * D), F32),
        compiler_params=_params(("arbitrary",), 32),
        name="mod",
    )(cc, w_mod, b_mod)


def _ctx_kernel(ctx_ref, mod_ref, g1_ref, win_ref, cw_ref, cb_ref, wg_ref, ba_ref, bi_ref, lam_ref,
                hf_ref, hb_ref, xi_ref):
    sh = mod_ref[B:B + 1, 0:D]
    gs = g1_ref[...] * (1.0 + mod_ref[B:B + 1, D:2 * D])
    parts = []
    for b in range(B):
        hc = _rms_mod(ctx_ref[b], gs, sh).astype(BF16)
        parts.append(jnp.dot(hc, win_ref[...], preferred_element_type=F32))
    xx = jnp.concatenate(parts, axis=0)
    cx = _conv_interleaved(xx, xi_ref, LC, LC, cw_ref[...], cb_ref[...])
    for d in range(2):
        a, dr = _rglru_coeffs(cx, wg_ref, d, ba_ref[d:d + 1, :], bi_ref[d:d + 1, :], lam_ref[d:d + 1, :])
        h = _scan(a, dr, None, jnp.zeros((B, DL), F32), LC, reverse=(d == 1))
        if d == 0:
            hf_ref[...] = h
        else:
            hb_ref[...] = h


def _context_states(ctx, mod, g1, w_in_x, cw, cb, wg, ba, bi, lam):
    rows = LC * B
    full = lambda shape: pl.BlockSpec(shape, lambda i: (0,) * len(shape))
    return pl.pallas_call(
        _ctx_kernel,
        grid=(1,),
        in_specs=[full((B, LC, D)), full((2 * B, N_MOD * D)), full((1, D)), full((D, DL)), full((4, DL)),
                  full((1, DL)), full((2, 2, DL // 2, DL)), full((2, DL)), full((2, DL)), full((2, DL))],
        out_specs=[full((B, DL)), full((B, DL))],
        out_shape=[jax.ShapeDtypeStruct((B, DL), F32)] * 2,
        scratch_shapes=[pltpu.VMEM((DL // LANES, PAD_F + rows + PAD_B, LANES), F32)],
        compiler_params=_params(("arbitrary",), 56),
        name="ctx",
    )(ctx, mod, g1, w_in_x, cw, cb, wg, ba, bi, lam)


def _mix_in_kernel(x_ref, mod_ref, g1_ref, win_ref, sg_ref, sw_ref, sbias_ref, cw_ref, cb_ref, wg_ref,
                   ba_ref, bi_ref, lam_ref, h0_ref,
                   sgu_ref, gg_ref, cx_ref, hf_ref,
                   hbuf, xi_ref, carry):
    @pl.when(pl.program_id(0) == 0)
    def _():
        carry[...] = h0_ref[...]

    g1 = g1_ref[...]
    for b in range(B):
        sh = mod_ref[b:b + 1, 0:D]
        gs = g1 * (1.0 + mod_ref[b:b + 1, D:2 * D])
        hbuf[b * TP:(b + 1) * TP, :] = _rms_mod(x_ref[b], gs, sh).astype(BF16)
    z = jnp.dot(hbuf[...], win_ref[...], preferred_element_type=F32)
    ug = _gelu(z[:, 0:DS])
    vg = _gelu(z[:, DS:2 * DS])
    xx = z[:, 2 * DS:2 * DS + DL]
    gg = _gelu(z[:, 2 * DS + DL:])
    for b in range(B):
        gg_ref[b] = gg[b * TP:(b + 1) * TP, :].astype(BF16)

    for h in range(NH):
        hs = slice(h * HD, (h + 1) * HD)
        vh = vg[:, hs]
        vn = (vh * lax.rsqrt(jnp.mean(vh * vh, axis=-1, keepdims=True) + EPS)) * sg_ref[:, hs]
        vnb = vn.astype(BF16)
        for b in range(B):
            rs = slice(b * TP, (b + 1) * TP)
            s = jnp.dot(sw_ref[h], vnb[rs], preferred_element_type=F32) + sbias_ref[:, hs]
            sgu_ref[b, :, hs] = (ug[rs, hs] * s).astype(BF16)

    cx = _conv_interleaved(xx, xi_ref, TP, GRID_W, cw_ref[...], cb_ref[...])
    cx_ref[...] = cx
    a, dr = _rglru_coeffs(cx, wg_ref, 0, ba_ref[0:1, :], bi_ref[0:1, :], lam_ref[0:1, :])
    carry[...] = _scan(a, dr, hf_ref, carry[...], TP, reverse=False)


def _mix_in(x, mod, g1, w_in_b, sg, sw_b, sbias, cw, cb, wg, ba, bi, lam, h0f):
    full = lambda shape: pl.BlockSpec(shape, lambda i: (0,) * len(shape))
    return pl.pallas_call(
        _mix_in_kernel,
        grid=(N_TT,),
        in_specs=[pl.BlockSpec((B, TP, D), lambda i: (0, i, 0)),
                  full((2 * B, N_MOD * D)), full((1, D)), full((D, 2 * DS + 2 * DL)), full((1, DS)),
                  full((NH, CHUNK, CHUNK)), full((CHUNK, DS)), full((4, DL)), full((1, DL)),
                  full((2, 2, DL // 2, DL)), full((2, DL)), full((2, DL)), full((2, DL)), full((B, DL))],
        out_specs=[pl.BlockSpec((B, TP, DS), lambda i: (0, i, 0)),
                   pl.BlockSpec((B, TP, DL), lambda i: (0, i, 0)),
                   pl.BlockSpec((ROWS, DL), lambda i: (i, 0)),
                   pl.BlockSpec((ROWS, DL), lambda i: (i, 0))],
        out_shape=[jax.ShapeDtypeStruct((B, S, DS), BF16), jax.ShapeDtypeStruct((B, S, DL), BF16),
                   jax.ShapeDtypeStruct((S * B, DL), F32), jax.ShapeDtypeStruct((S * B, DL), F32)],
        scratch_shapes=[pltpu.VMEM((ROWS, D), BF16),
                        pltpu.VMEM((DL // LANES, PAD_F + ROWS + PAD_B, LANES), F32),
                        pltpu.VMEM((B, DL), F32)],
        compiler_params=_params(("arbitrary",), 56),
        name="mix_in",
    )(x, mod, g1, w_in_b, sg, sw_b, sbias, cw, cb, wg, ba, bi, lam, h0f)


def _mix_out_kernel(x_ref, mod_ref, cx_ref, hf_ref, sgu_ref, gg_ref, wg_ref, ba_ref, bi_ref, lam_ref, h0_ref,
                    wout_ref, g2_ref, wr_ref,
                    x1_ref, hx2t_ref, lg_ref,
                    hb_s, hs_ref, carry):
    @pl.when(pl.program_id(0) == 0)
    def _():
        carry[...] = h0_ref[...]

    a, dr = _rglru_coeffs(cx_ref[...], wg_ref, 1, ba_ref[1:2, :], bi_ref[1:2, :], lam_ref[1:2, :])
    carry[...] = _scan(a, dr, hb_s, carry[...], TP, reverse=True)
    hsum = hf_ref[...] + hb_s[...]
    nslab = DL // LANES
    for k in range(nslab):
        hs_ref[k] = hsum[:, k * LANES:(k + 1) * LANES]
    rows = []
    for b in range(B):
        rec = [(gg_ref[b, :, k * LANES:(k + 1) * LANES].astype(F32)
                * hs_ref[k, pl.ds(b, TP, stride=B), :]).astype(BF16) for k in range(nslab)]
        rows.append(jnp.concatenate([sgu_ref[b]] + rec, axis=1))
    gb = B // PROJ_GROUPS
    ys = [jnp.dot(jnp.concatenate(rows[g * gb:(g + 1) * gb], axis=0), wout_ref[...], preferred_element_type=F32)
          for g in range(PROJ_GROUPS)]

    g2 = g2_ref[...]
    wr = wr_ref[...]
    wr_hi = wr.astype(BF16)
    wr_lo = (wr - wr_hi.astype(F32)).astype(BF16)
    for b in range(B):
        g1x = mod_ref[b:b + 1, 2 * D:3 * D]
        sh2 = mod_ref[b:b + 1, 3 * D:4 * D]
        gs2 = g2 * (1.0 + mod_ref[b:b + 1, 4 * D:5 * D])
        x1 = x_ref[b] + g1x * ys[b // gb][(b % gb) * TP:(b % gb + 1) * TP, :]
        x1_ref[b] = x1
        hx2 = _rms_mod(x1, gs2, sh2)
        for k in range(SUB_PER_TOK):
            hx2t_ref[b, pl.ds(k, TP, stride=SUB_PER_TOK), :] = hx2[:, k * LANES:(k + 1) * LANES]
        hx_hi = hx2.astype(BF16)
        hx_lo = (hx2 - hx_hi.astype(F32)).astype(BF16)
        nt = (((1,), (1,)), ((), ()))
        lg_ref[b] = (lax.dot_general(wr_hi, hx_hi, nt, preferred_element_type=F32)
                     + (lax.dot_general(wr_hi, hx_lo, nt, preferred_element_type=F32)
                        + lax.dot_general(wr_lo, hx_hi, nt, preferred_element_type=F32)))


def _mix_out(x, mod, cx, hf, sgu, gg, wg, ba, bi, lam, h0b, w_out_b, g2, wr_t):
    full = lambda shape: pl.BlockSpec(shape, lambda i: (0,) * len(shape))
    rev = lambda i: N_TT - 1 - i
    return pl.pallas_call(
        _mix_out_kernel,
        grid=(N_TT,),
        in_specs=[pl.BlockSpec((B, TP, D), lambda i: (0, rev(i), 0)),
                  full((2 * B, N_MOD * D)),
                  pl.BlockSpec((ROWS, DL), lambda i: (rev(i), 0)),
                  pl.BlockSpec((ROWS, DL), lambda i: (rev(i), 0)),
                  pl.BlockSpec((B, TP, DS), lambda i: (0, rev(i), 0)),
                  pl.BlockSpec((B, TP, DL), lambda i: (0, rev(i), 0)),
                  full((2, 2, DL // 2, DL)), full((2, DL)), full((2, DL)), full((2, DL)), full((B, DL)),
                  full((D, D)), full((1, D)), full((E, D))],
        out_specs=[pl.BlockSpec((B, TP, D), lambda i: (0, rev(i), 0)),
                   pl.BlockSpec((B, TP * SUB_PER_TOK, LANES), lambda i: (0, rev(i), 0)),
                   pl.BlockSpec((B, E, TP), lambda i: (0, 0, rev(i)))],
        out_shape=[jax.ShapeDtypeStruct((B, S, D), F32),
                   jax.ShapeDtypeStruct((B, S * SUB_PER_TOK, LANES), F32),
                   jax.ShapeDtypeStruct((B, E, S), F32)],
        scratch_shapes=[pltpu.VMEM((ROWS, DL), F32),
                        pltpu.VMEM((DL // LANES, ROWS, LANES), F32),
                        pltpu.VMEM((B, DL), F32)],
        compiler_params=_params(("arbitrary",), 56),
        name="mix_out",
    )(x, mod, cx, hf, sgu, gg, wg, ba, bi, lam, h0b, w_out_b, g2, wr_t)


NBLK = S // LANES
BIG = 1.0e9


def _bf16_parts(a):
    hi = a.astype(BF16).astype(F32)
    r1 = a - hi
    mid = r1.astype(BF16).astype(F32)
    lo = (r1 - mid).astype(BF16).astype(F32)
    return hi, mid, lo


def _route_kernel(lg_ref, idx_ref, gate_ref, aff_s, lci_s, offb_s, ahi_s, amid_s, alo_s, offi_s):
    for b in range(B):
        l = lg_ref[b]
        ex = jnp.exp(l - jnp.max(l, axis=0, keepdims=True))
        aff_s[b * E:(b + 1) * E, :] = ex / jnp.sum(ex, axis=0, keepdims=True)
    aff = aff_s[...]
    nrow = B * E

    def bisect(_, lohi):
        lo, hi = lohi
        mid = lo + ((hi - lo + 1) >> 1)
        cnt = jnp.sum(jnp.where(aff >= pltpu.bitcast(mid, F32), 1.0, 0.0), axis=1, keepdims=True)
        ge = cnt >= float(CAP)
        return jnp.where(ge, mid, lo), jnp.where(ge, hi, mid - 1)

    lo0 = jnp.zeros((nrow, 1), jnp.int32)
    hi0 = jnp.full((nrow, 1), 0x7F800000, jnp.int32)
    thr_bits, _ = lax.fori_loop(0, 31, bisect, (lo0, hi0))
    thr = pltpu.bitcast(thr_bits, F32)
    gt = aff > thr
    eq = aff == thr
    need = float(CAP) - jnp.sum(jnp.where(gt, 1.0, 0.0), axis=1, keepdims=True)

    qi = lax.broadcasted_iota(jnp.int32, (LANES, LANES), 0)
    ti = lax.broadcasted_iota(jnp.int32, (LANES, LANES), 1)
    tri = jnp.where(qi <= ti, 1.0, 0.0).astype(BF16)
    blocks = [slice(j * LANES, (j + 1) * LANES) for j in range(NBLK)]

    ties_before = jnp.zeros((nrow, 1), F32)
    sel = []
    for sl in blocks:
        eqb = jnp.where(eq[:, sl], 1.0, 0.0)
        incl = jnp.dot(eqb.astype(BF16), tri, preferred_element_type=F32) + ties_before
        sel.append(gt[:, sl] | (eq[:, sl] & ((incl - eqb) < need)))
        ties_before = incl[:, LANES - 1:LANES]

    lane_sq = lax.broadcasted_iota(jnp.int32, (nrow, LANES), 1)
    off = jnp.zeros((nrow, 1), F32)
    offi = jnp.full((nrow, LANES), BIG, F32)
    for j, sl in enumerate(blocks):
        rows = slice(j * nrow, (j + 1) * nrow)
        lci = jnp.dot(jnp.where(sel[j], 1.0, 0.0).astype(BF16), tri, preferred_element_type=F32)
        lci_s[rows, :] = lci
        offb_s[rows, :] = jnp.broadcast_to(off, (nrow, LANES))
        ahi_s[rows, :], amid_s[rows, :], alo_s[rows, :] = _bf16_parts(aff[:, sl])
        off = off + lci[:, LANES - 1:LANES]
        offi = jnp.where(lane_sq == j, off, offi)
    offi_s[...] = offi

    slot = lax.broadcasted_iota(jnp.int32, (CAP, LANES), 0).astype(F32)
    lane = lax.broadcasted_iota(jnp.int32, (CAP, LANES), 1)
    lane_f = lane.astype(F32)
    zpad = jnp.zeros((LANES - NBLK, 5 * LANES), BF16)

    def row_body(r, carry):
        idxm, gm = carry
        take = lambda ref: ref[pl.ds(r, NBLK, stride=nrow), :]
        table = jnp.concatenate([take(lci_s), take(offb_s), take(ahi_s), take(amid_s), take(alo_s)], axis=1)
        table = jnp.concatenate([table.astype(BF16), zpad], axis=0)
        blk = jnp.sum(jnp.where(offi_s[pl.ds(r, 1), :] <= slot, 1.0, 0.0), axis=1, keepdims=True)
        pick = jnp.where(lane_f == blk, 1.0, 0.0).astype(BF16)
        res = jnp.dot(pick, table, preferred_element_type=F32)
        rank1 = (slot + 1.0) - res[:, LANES:2 * LANES]
        tokl = jnp.sum(jnp.where(res[:, 0:LANES] < rank1, 1.0, 0.0), axis=1, keepdims=True)
        affs = (res[:, 2 * LANES:3 * LANES] + res[:, 3 * LANES:4 * LANES]) + res[:, 4 * LANES:5 * LANES]
        gv = jnp.sum(jnp.where(lane_f == tokl, affs, 0.0), axis=1, keepdims=True)
        iv = blk * float(LANES) + tokl
        put = lane == r
        return jnp.where(put, iv, idxm), jnp.where(put, gv, gm)

    z = jnp.zeros((CAP, LANES), F32)
    idxm, gm = lax.fori_loop(0, nrow, row_body, (z, z), unroll=8)
    idx_ref[...] = (idxm.T * float(SUB_PER_TOK)).astype(jnp.int32)
    gate_ref[...] = gm


def _route(logits_t):
    full = lambda shape: pl.BlockSpec(shape, lambda i: (0,) * len(shape))
    return pl.pallas_call(
        _route_kernel,
        grid=(1,),
        in_specs=[full((B, E, S))],
        out_specs=[full((B * E, CAP)), full((CAP, B * E))],
        out_shape=[jax.ShapeDtypeStruct((B * E, CAP), jnp.int32), jax.ShapeDtypeStruct((CAP, B * E), F32)],
        scratch_shapes=[pltpu.VMEM((B * E, S), F32)] + [pltpu.VMEM((NBLK * B * E, LANES), F32)] * 5
                       + [pltpu.VMEM((B * E, LANES), F32)],
        compiler_params=_params(("arbitrary",), 48),
        name="route",
    )(logits_t)


def _dispatch_kernel(idx_ref, h_ref, xg_ref, xt0, xt1):
    for e in range(E):
        xt = xt0 if e % 2 == 0 else xt1
        for s in range(CAP):
            src = pl.multiple_of(idx_ref[e * CAP + s], SUB_PER_TOK)
            xt[s * SUB_PER_TOK:(s + 1) * SUB_PER_TOK, :] = h_ref[0, pl.ds(src, SUB_PER_TOK), :]
        for k in range(SUB_PER_TOK):
            xg_ref[e, :, k * LANES:(k + 1) * LANES] = xt[pl.ds(k, CAP, stride=SUB_PER_TOK), :].astype(BF16)


def _dispatch(idx1, hx2t):
    return pl.pallas_call(
        _dispatch_kernel,
        grid=(B,),
        in_specs=[pl.BlockSpec((E * CAP,), lambda b: (b,), memory_space=pltpu.SMEM),
                  pl.BlockSpec((1, S * SUB_PER_TOK, LANES), lambda b: (b, 0, 0))],
        out_specs=pl.BlockSpec((E, CAP, D), lambda b: (0, b, 0)),
        out_shape=jax.ShapeDtypeStruct((E, B * CAP, D), BF16),
        scratch_shapes=[pltpu.VMEM((CAP * SUB_PER_TOK, LANES), F32), pltpu.VMEM((CAP * SUB_PER_TOK, LANES), F32)],
        compiler_params=_params(("arbitrary",), 40),
        name="dispatch",
    )(idx1, hx2t)


def _moe_kernel(x_ref, gate_ref, w1_ref, w3_ref, w2_ref, y_ref, hid_s, w2_s):
    f = pl.program_id(1)
    w1 = w1_ref[0].astype(BF16)
    w3 = w3_ref[0].astype(BF16)
    fs = pl.multiple_of(f * FN, FN)
    w2_s[pl.ds(fs, FN), :] = w2_ref[0].astype(BF16)
    for c in range(B * CAP // MC):
        rs = slice(c * MC, (c + 1) * MC)
        x = x_ref[0, rs, :]
        h1 = jnp.dot(x, w1, preferred_element_type=F32)
        h3 = jnp.dot(x, w3, preferred_element_type=F32)
        hid_s[f, rs, :] = ((h1 * jax.nn.sigmoid(h1)) * h3).astype(BF16)

    @pl.when(f == FF // FN - 1)
    def _():
        gates = gate_ref[...]
        lane = lax.broadcasted_iota(jnp.int32, gates.shape, 1)
        e = pl.program_id(0)
        for c in range(B * CAP // MC):
            rs = slice(c * MC, (c + 1) * MC)
            hid = jnp.concatenate([hid_s[j, rs, :] for j in range(FF // FN)], axis=1)
            y = jnp.dot(hid, w2_s[...], preferred_element_type=F32)
            for bb in range(MC // CAP):
                b = c * (MC // CAP) + bb
                gcol = jnp.sum(jnp.where(lane == b * E + e, gates, 0.0), axis=1, keepdims=True)
                yb = y[bb * CAP:(bb + 1) * CAP, :] * gcol
                for k in range(SUB_PER_TOK):
                    y_ref[0, pl.ds(b * CAP * SUB_PER_TOK + k, CAP, stride=SUB_PER_TOK), :] = yb[:, k * LANES:(k + 1) * LANES]


def _moe(xg, gate_cols, w1, w3, w2):
    return pl.pallas_call(
        _moe_kernel,
        grid=(E, FF // FN),
        in_specs=[pl.BlockSpec((1, B * CAP, D), lambda e, f: (e, 0, 0)),
                  pl.BlockSpec((CAP, B * E), lambda e, f: (0, 0)),
                  pl.BlockSpec((1, D, FN), lambda e, f: (e, 0, f)),
                  pl.BlockSpec((1, D, FN), lambda e, f: (e, 0, f)),
                  pl.BlockSpec((1, FN, D), lambda e, f: (e, f, 0))],
        out_specs=pl.BlockSpec((1, B * CAP * SUB_PER_TOK, LANES), lambda e, f: (e, 0, 0)),
        out_shape=jax.ShapeDtypeStruct((E, B * CAP * SUB_PER_TOK, LANES), F32),
        scratch_shapes=[pltpu.VMEM((FF // FN, B * CAP, FN), BF16), pltpu.VMEM((FF, D), BF16)],
        compiler_params=_params(("arbitrary", "arbitrary"), 58),
        name="moe",
    )(xg, gate_cols, w1, w3, w2)


def _combine_kernel(idx_ref, y_ref, x1_ref, mod_ref, fg_ref, out_ref, acc):
    b = pl.program_id(0)
    g = pl.program_id(1)

    @pl.when(g == 0)
    def _():
        acc[...] = jnp.zeros(acc.shape, F32)

    @pl.when(g < NG)
    def _():
        for el in range(EG):
            for s0 in range(0, CAP, SCATTER_BATCH):
                dsts, news = [], []
                for j in range(SCATTER_BATCH):
                    dst = pl.multiple_of(idx_ref[el * CAP + s0 + j], SUB_PER_TOK)
                    src = (s0 + j) * SUB_PER_TOK
                    dsts.append(dst)
                    news.append(acc[pl.ds(dst, SUB_PER_TOK), :] + y_ref[el, src:src + SUB_PER_TOK, :])
                for dst, new in zip(dsts, news):
                    acc[pl.ds(dst, SUB_PER_TOK), :] = new

    @pl.when(g >= NG)
    def _():
        row0 = pl.multiple_of((g - NG) * (FIN_ROWS * SUB_PER_TOK), FIN_ROWS * SUB_PER_TOK)
        g2x = mod_ref[pl.ds(b, 1), 5 * D:6 * D]
        ssq = jnp.zeros((FIN_ROWS, 1), F32)
        for k in range(SUB_PER_TOK):
            ls = slice(k * LANES, (k + 1) * LANES)
            xo = x1_ref[0, :, ls] + g2x[:, ls] * acc[pl.ds(row0 + k, FIN_ROWS, stride=SUB_PER_TOK), :]
            out_ref[0, :, ls] = xo
            ssq = ssq + jnp.sum(xo * xo, axis=1, keepdims=True)
        inv = lax.rsqrt(ssq * (1.0 / D) + EPS)
        out_ref[0] = (out_ref[0] * inv) * fg_ref[...]


def _combine(idx1, y, x1, mod, fg):
    fin = lambda b, g: (b, jnp.maximum(g - NG, 0), 0)
    grp = lambda g: jnp.minimum(g, NG - 1)
    return pl.pallas_call(
        _combine_kernel,
        grid=(B, NG + S // FIN_ROWS),
        in_specs=[pl.BlockSpec((EG * CAP,), lambda b, g: (b * NG + grp(g),), memory_space=pltpu.SMEM),
                  pl.BlockSpec((EG, CAP * SUB_PER_TOK, LANES), lambda b, g: (grp(g), b, 0)),
                  pl.BlockSpec((1, FIN_ROWS, D), fin),
                  pl.BlockSpec((2 * B, N_MOD * D), lambda b, g: (0, 0)),
                  pl.BlockSpec((1, D), lambda b, g: (0, 0))],
        out_specs=pl.BlockSpec((1, FIN_ROWS, D), fin),
        out_shape=jax.ShapeDtypeStruct((B, S, D), F32),
        scratch_shapes=[pltpu.VMEM((S * SUB_PER_TOK, LANES), F32)],
        compiler_params=_params(("arbitrary", "arbitrary"), 56),
        name="combine",
    )(idx1, y, x1, mod, fg)


def _pack_gate_weights(wa, wi):
    eye = jnp.eye(4, dtype=wa.dtype)

    def bdiag(w4):
        return jnp.einsum('hij,hg->higj', w4, eye).reshape(4 * LRU_HD, 4 * LRU_HD)

    dirs = []
    for d in range(2):
        halves = []
        for k in range(2):
            hs = slice(4 * k, 4 * (k + 1))
            halves.append(jnp.concatenate([bdiag(wa[d, hs]), bdiag(wi[d, hs])], axis=1))
        dirs.append(jnp.stack(halves))
    return jnp.stack(dirs).astype(BF16)


def kernel(x, c, ctx, c_ctx, w_mod, b_mod, norm1_g, norm2_g, w_in, sgu_g, sgu_w, sgu_b, conv_w, conv_b,
           rg_wa, rg_ba, rg_wi, rg_bi, rg_lam, w_out, w_router, w1, w3, w2, final_g):
    assert x.shape == (B, S, D) and ctx.shape == (B, LC, D) and w_mod.shape[0] == 1

    cc = jnp.concatenate([c, c_ctx[None, :], jnp.zeros((B - 1, D), F32)], axis=0)
    mod = _modulation(cc, w_mod[0], b_mod[0][None, :])

    g1 = norm1_g[0][None, :]
    g2 = norm2_g[0][None, :]
    w_in_b = w_in[0].astype(BF16)
    w_in_x = w_in_b[:, 2 * DS:2 * DS + DL]
    wg = _pack_gate_weights(rg_wa[0], rg_wi[0])
    cw = conv_w[0]
    cb = conv_b[0][None, :]
    ba, bi, lam = rg_ba[0], rg_bi[0], rg_lam[0]
    sbias = jnp.repeat(sgu_b[0].T, HD, axis=1)

    h0f, h0b = _context_states(ctx, mod, g1, w_in_x, cw, cb, wg, ba, bi, lam)
    sgu, gg, cx, hf = _mix_in(x, mod, g1, w_in_b, sgu_g[0][None, :], sgu_w[0].astype(BF16), sbias, cw, cb,
                              wg, ba, bi, lam, h0f)
    x1, hx2t, logits_t = _mix_out(x, mod, cx, hf, sgu, gg, wg, ba, bi, lam, h0b, w_out[0].astype(BF16), g2,
                                  w_router[0].T)
    idx, gate = _route(logits_t)
    idx1 = idx.reshape(B * E * CAP)
    xg = _dispatch(idx1, hx2t)
    y = _moe(xg, gate, w1[0], w3[0], w2[0])
    return _combine(idx1, y, x1, mod, final_g[None, :])
```

```python
import jax
import jax.numpy as jnp
from jax import lax
from jax.experimental import pallas as pl
from jax.experimental.pallas import tpu as pltpu

F32 = jnp.float32
BF16 = jnp.bfloat16

D = 1024
B = 8
S = 2048
LC = 256
GRID_W = 64
DS = 512
NH = 4
HD = DS // NH
CHUNK = 128
DL = 512
LRU_HEADS = 8
LRU_HD = DL // LRU_HEADS
E = 16
CAP = 2 * S // E
FF = 2048
N_MOD = 6
EPS = 1e-6
RG_C = 8.0

SUBLANES = 8
LANES = 128
VMEM_LIMIT_V7X = 60000 * 1024

TP = CHUNK
ROWS = TP * B
N_TT = S // TP
SUB_PER_TOK = D // LANES
PROJ_GROUPS = 2
FN = 512
MC = 512
SCATTER_BATCH = 16
EG = 8
NG = E // EG
FIN_ROWS = 1024
PAD_F = SUBLANES
PAD_B = 2 * SUBLANES


def _params(sem, vmem_mb):
    return pltpu.CompilerParams(dimension_semantics=sem, vmem_limit_bytes=min(vmem_mb << 20, VMEM_LIMIT_V7X))


GELU_C1 = 0.7978845608028654
GELU_C2 = GELU_C1 * 0.044715


def _gelu(x):
    half = 0.5 * x
    return half + half * jnp.tanh(x * (GELU_C1 + GELU_C2 * (x * x)))


def _rms_mod(x, gs, sh):
    return (x * lax.rsqrt(jnp.mean(x * x, axis=-1, keepdims=True) + EPS)) * gs + sh


def _log_sigmoid(x):
    return -(jnp.maximum(-x, 0.0) + jnp.log1p(jnp.exp(-jnp.abs(x))))


def _conv_interleaved(xx, xi_ref, tp, period, cw, cb):
    rows = tp * B
    nslab = DL // LANES
    xi_ref[:, 0:PAD_F, :] = jnp.zeros((nslab, PAD_F, LANES), F32)
    xi_ref[:, PAD_F + rows:PAD_F + rows + PAD_B, :] = jnp.zeros((nslab, PAD_B, LANES), F32)
    for k in range(nslab):
        for b in range(B):
            xi_ref[k, pl.ds(PAD_F + b, tp, stride=B), :] = xx[b * tp:(b + 1) * tp, k * LANES:(k + 1) * LANES]
    pos = lax.broadcasted_iota(jnp.int32, (rows, LANES), 0) >> 3
    pm = pos & (period - 1)
    m0 = pm != 0
    m2 = pm != period - 1
    m3 = pm < period - 2
    outs = []
    for k in range(nslab):
        w = cw[:, k * LANES:(k + 1) * LANES]
        t0 = xi_ref[k, 0:rows, :]
        t1 = xi_ref[k, SUBLANES:SUBLANES + rows, :]
        t2 = xi_ref[k, 2 * SUBLANES:2 * SUBLANES + rows, :]
        t3 = xi_ref[k, 3 * SUBLANES:3 * SUBLANES + rows, :]
        acc = jnp.where(m0, t0, 0.0) * w[0:1] + t1 * w[1:2]
        acc = acc + jnp.where(m2, t2, 0.0) * w[2:3] + jnp.where(m3, t3, 0.0) * w[3:4]
        outs.append(acc + cb[:, k * LANES:(k + 1) * LANES])
    return jnp.concatenate(outs, axis=1)


def _rglru_coeffs(cx, wg_ref, d, ba, bi, lam):
    cxb = cx.astype(BF16)
    half = DL // 2
    a_parts, b_parts = [], []
    for k in range(2):
        sl = slice(half * k, half * (k + 1))
        z = jnp.dot(cxb[:, sl], wg_ref[d, k], preferred_element_type=F32)
        r = jax.nn.sigmoid(z[:, :half] + ba[:, sl])
        i = jax.nn.sigmoid(z[:, half:] + bi[:, sl])
        neg_log_a = r * (RG_C * -_log_sigmoid(lam[:, sl]))
        a = jnp.exp(-neg_log_a)
        one_minus_a2 = jnp.tanh(neg_log_a) * (a * a + 1.0)
        root = jnp.where(one_minus_a2 > 0.0, one_minus_a2 * lax.rsqrt(one_minus_a2), 0.0)
        a_parts.append(a)
        b_parts.append(root * (i * cx[:, sl]))
    return jnp.concatenate(a_parts, axis=1), jnp.concatenate(b_parts, axis=1)


def _scan(a, b, h_out_ref, h0, nsteps, reverse):
    h = h0
    for p in (range(nsteps - 1, -1, -1) if reverse else range(nsteps)):
        rs = slice(p * SUBLANES, (p + 1) * SUBLANES)
        h = a[rs, :] * h + b[rs, :]
        if h_out_ref is not None:
            h_out_ref[rs, :] = h
    return h


def _mod_kernel(c_ref, w_ref, b_ref, o_ref):
    c = c_ref[...]
    s = c * jax.nn.sigmoid(c)
    s_hi = s.astype(BF16)
    s_lo = (s - s_hi.astype(F32)).astype(BF16)
    w = w_ref[...]
    w_hi = w.astype(BF16)
    w_lo = (w - w_hi.astype(F32)).astype(BF16)
    nrow = s.shape[0]
    both = jnp.dot(jnp.concatenate([s_hi, s_lo], axis=0), w_hi, preferred_element_type=F32)
    o_ref[...] = (both[0:nrow] + (both[nrow:] + jnp.dot(s_hi, w_lo, preferred_element_type=F32))) + b_ref[...]


def _modulation(cc, w_mod, b_mod):
    nt = 4
    tn = N_MOD * D // nt
    return pl.pallas_call(
        _mod_kernel,
        grid=(nt,),
        in_specs=[pl.BlockSpec((2 * B, D), lambda i: (0, 0)),
                  pl.BlockSpec((D, tn), lambda i: (0, i)),
                  pl.BlockSpec((1, tn), lambda i: (0, i))],
        out_specs=pl.BlockSpec((2 * B, tn), lambda i: (0, i)),
        out_shape=jax.ShapeDtypeStruct((2 * B, N_MOD * D), F32),
        compiler_params=_params(("arbitrary",), 32),
        name="mod",
    )(cc, w_mod, b_mod)


def _ctx_kernel(ctx_ref, mod_ref, g1_ref, win_ref, cw_ref, cb_ref, wg_ref, ba_ref, bi_ref, lam_ref,
                hf_ref, hb_ref, xi_ref):
    sh = mod_ref[B:B + 1, 0:D]
    gs = g1_ref[...] * (1.0 + mod_ref[B:B + 1, D:2 * D])
    parts = []
    for b in range(B):
        hc = _rms_mod(ctx_ref[b], gs, sh).astype(BF16)
        parts.append(jnp.dot(hc, win_ref[...], preferred_element_type=F32))
    xx = jnp.concatenate(parts, axis=0)
    cx = _conv_interleaved(xx, xi_ref, LC, LC, cw_ref[...], cb_ref[...])
    for d in range(2):
        a, dr = _rglru_coeffs(cx, wg_ref, d, ba_ref[d:d + 1, :], bi_ref[d:d + 1, :], lam_ref[d:d + 1, :])
        h = _scan(a, dr, None, jnp.zeros((B, DL), F32), LC, reverse=(d == 1))
        if d == 0:
            hf_ref[...] = h
        else:
            hb_ref[...] = h


def _context_states(ctx, mod, g1, w_in_x, cw, cb, wg, ba, bi, lam):
    rows = LC * B
    full = lambda shape: pl.BlockSpec(shape, lambda i: (0,) * len(shape))
    return pl.pallas_call(
        _ctx_kernel,
        grid=(1,),
        in_specs=[full((B, LC, D)), full((2 * B, N_MOD * D)), full((1, D)), full((D, DL)), full((4, DL)),
                  full((1, DL)), full((2, 2, DL // 2, DL)), full((2, DL)), full((2, DL)), full((2, DL))],
        out_specs=[full((B, DL)), full((B, DL))],
        out_shape=[jax.ShapeDtypeStruct((B, DL), F32)] * 2,
        scratch_shapes=[pltpu.VMEM((DL // LANES, PAD_F + rows + PAD_B, LANES), F32)],
        compiler_params=_params(("arbitrary",), 56),
        name="ctx",
    )(ctx, mod, g1, w_in_x, cw, cb, wg, ba, bi, lam)


def _mix_in_kernel(x_ref, mod_ref, g1_ref, win_ref, sg_ref, sw_ref, sbias_ref, cw_ref, cb_ref, wg_ref,
                   ba_ref, bi_ref, lam_ref, h0_ref,
                   sgu_ref, gg_ref, cx_ref, hf_ref,
                   hbuf, xi_ref, carry):
    @pl.when(pl.program_id(0) == 0)
    def _():
        carry[...] = h0_ref[...]

    g1 = g1_ref[...]
    for b in range(B):
        sh = mod_ref[b:b + 1, 0:D]
        gs = g1 * (1.0 + mod_ref[b:b + 1, D:2 * D])
        hbuf[b * TP:(b + 1) * TP, :] = _rms_mod(x_ref[b], gs, sh).astype(BF16)
    z = jnp.dot(hbuf[...], win_ref[...], preferred_element_type=F32)
    ug = _gelu(z[:, 0:DS])
    vg = _gelu(z[:, DS:2 * DS])
    xx = z[:, 2 * DS:2 * DS + DL]
    gg = _gelu(z[:, 2 * DS + DL:])
    for b in range(B):
        gg_ref[b] = gg[b * TP:(b + 1) * TP, :].astype(BF16)

    for h in range(NH):
        hs = slice(h * HD, (h + 1) * HD)
        vh = vg[:, hs]
        vn = (vh * lax.rsqrt(jnp.mean(vh * vh, axis=-1, keepdims=True) + EPS)) * sg_ref[:, hs]
        vnb = vn.astype(BF16)
        for b in range(B):
            rs = slice(b * TP, (b + 1) * TP)
            s = jnp.dot(sw_ref[h], vnb[rs], preferred_element_type=F32) + sbias_ref[:, hs]
            sgu_ref[b, :, hs] = (ug[rs, hs] * s).astype(BF16)

    cx = _conv_interleaved(xx, xi_ref, TP, GRID_W, cw_ref[...], cb_ref[...])
    cx_ref[...] = cx
    a, dr = _rglru_coeffs(cx, wg_ref, 0, ba_ref[0:1, :], bi_ref[0:1, :], lam_ref[0:1, :])
    carry[...] = _scan(a, dr, hf_ref, carry[...], TP, reverse=False)


def _mix_in(x, mod, g1, w_in_b, sg, sw_b, sbias, cw, cb, wg, ba, bi, lam, h0f):
    full = lambda shape: pl.BlockSpec(shape, lambda i: (0,) * len(shape))
    return pl.pallas_call(
        _mix_in_kernel,
        grid=(N_TT,),
        in_specs=[pl.BlockSpec((B, TP, D), lambda i: (0, i, 0)),
                  full((2 * B, N_MOD * D)), full((1, D)), full((D, 2 * DS + 2 * DL)), full((1, DS)),
                  full((NH, CHUNK, CHUNK)), full((CHUNK, DS)), full((4, DL)), full((1, DL)),
                  full((2, 2, DL // 2, DL)), full((2, DL)), full((2, DL)), full((2, DL)), full((B, DL))],
        out_specs=[pl.BlockSpec((B, TP, DS), lambda i: (0, i, 0)),
                   pl.BlockSpec((B, TP, DL), lambda i: (0, i, 0)),
                   pl.BlockSpec((ROWS, DL), lambda i: (i, 0)),
                   pl.BlockSpec((ROWS, DL), lambda i: (i, 0))],
        out_shape=[jax.ShapeDtypeStruct((B, S, DS), BF16), jax.ShapeDtypeStruct((B, S, DL), BF16),
                   jax.ShapeDtypeStruct((S * B, DL), F32), jax.ShapeDtypeStruct((S * B, DL), F32)],
        scratch_shapes=[pltpu.VMEM((ROWS, D), BF16),
                        pltpu.VMEM((DL // LANES, PAD_F + ROWS + PAD_B, LANES), F32),
                        pltpu.VMEM((B, DL), F32)],
        compiler_params=_params(("arbitrary",), 56),
        name="mix_in",
    )(x, mod, g1, w_in_b, sg, sw_b, sbias, cw, cb, wg, ba, bi, lam, h0f)


def _mix_out_kernel(x_ref, mod_ref, cx_ref, hf_ref, sgu_ref, gg_ref, wg_ref, ba_ref, bi_ref, lam_ref, h0_ref,
                    wout_ref, g2_ref, wr_ref,
                    x1_ref, hx2t_ref, lg_ref,
                    hb_s, hs_ref, carry):
    @pl.when(pl.program_id(0) == 0)
    def _():
        carry[...] = h0_ref[...]

    a, dr = _rglru_coeffs(cx_ref[...], wg_ref, 1, ba_ref[1:2, :], bi_ref[1:2, :], lam_ref[1:2, :])
    carry[...] = _scan(a, dr, hb_s, carry[...], TP, reverse=True)
    hsum = hf_ref[...] + hb_s[...]
    nslab = DL // LANES
    for k in range(nslab):
        hs_ref[k] = hsum[:, k * LANES:(k + 1) * LANES]
    rows = []
    for b in range(B):
        rec = [(gg_ref[b, :, k * LANES:(k + 1) * LANES].astype(F32)
                * hs_ref[k, pl.ds(b, TP, stride=B), :]).astype(BF16) for k in range(nslab)]
        rows.append(jnp.concatenate([sgu_ref[b]] + rec, axis=1))
    gb = B // PROJ_GROUPS
    ys = [jnp.dot(jnp.concatenate(rows[g * gb:(g + 1) * gb], axis=0), wout_ref[...], preferred_element_type=F32)
          for g in range(PROJ_GROUPS)]

    g2 = g2_ref[...]
    wr = wr_ref[...]
    wr_hi = wr.astype(BF16)
    wr_lo = (wr - wr_hi.astype(F32)).astype(BF16)
    for b in range(B):
        g1x = mod_ref[b:b + 1, 2 * D:3 * D]
        sh2 = mod_ref[b:b + 1, 3 * D:4 * D]
        gs2 = g2 * (1.0 + mod_ref[b:b + 1, 4 * D:5 * D])
        x1 = x_ref[b] + g1x * ys[b // gb][(b % gb) * TP:(b % gb + 1) * TP, :]
        x1_ref[b] = x1
        hx2 = _rms_mod(x1, gs2, sh2)
        for k in range(SUB_PER_TOK):
            hx2t_ref[b, pl.ds(k, TP, stride=SUB_PER_TOK), :] = hx2[:, k * LANES:(k + 1) * LANES]
        hx_hi = hx2.astype(BF16)
        hx_lo = (hx2 - hx_hi.astype(F32)).astype(BF16)
        nt = (((1,), (1,)), ((), ()))
        lg_ref[b] = (lax.dot_general(wr_hi, hx_hi, nt, preferred_element_type=F32)
                     + (lax.dot_general(wr_hi, hx_lo, nt, preferred_element_type=F32)
                        + lax.dot_general(wr_lo, hx_hi, nt, preferred_element_type=F32)))


def _mix_out(x, mod, cx, hf, sgu, gg, wg, ba, bi, lam, h0b, w_out_b, g2, wr_t):
    full = lambda shape: pl.BlockSpec(shape, lambda i: (0,) * len(shape))
    rev = lambda i: N_TT - 1 - i
    return pl.pallas_call(
        _mix_out_kernel,
        grid=(N_TT,),
        in_specs=[pl.BlockSpec((B, TP, D), lambda i: (0, rev(i), 0)),
                  full((2 * B, N_MOD * D)),
                  pl.BlockSpec((ROWS, DL), lambda i: (rev(i), 0)),
                  pl.BlockSpec((ROWS, DL), lambda i: (rev(i), 0)),
                  pl.BlockSpec((B, TP, DS), lambda i: (0, rev(i), 0)),
                  pl.BlockSpec((B, TP, DL), lambda i: (0, rev(i), 0)),
                  full((2, 2, DL // 2, DL)), full((2, DL)), full((2, DL)), full((2, DL)), full((B, DL)),
                  full((D, D)), full((1, D)), full((E, D))],
        out_specs=[pl.BlockSpec((B, TP, D), lambda i: (0, rev(i), 0)),
                   pl.BlockSpec((B, TP * SUB_PER_TOK, LANES), lambda i: (0, rev(i), 0)),
                   pl.BlockSpec((B, E, TP), lambda i: (0, 0, rev(i)))],
        out_shape=[jax.ShapeDtypeStruct((B, S, D), F32),
                   jax.ShapeDtypeStruct((B, S * SUB_PER_TOK, LANES), F32),
                   jax.ShapeDtypeStruct((B, E, S), F32)],
        scratch_shapes=[pltpu.VMEM((ROWS, DL), F32),
                        pltpu.VMEM((DL // LANES, ROWS, LANES), F32),
                        pltpu.VMEM((B, DL), F32)],
        compiler_params=_params(("arbitrary",), 56),
        name="mix_out",
    )(x, mod, cx, hf, sgu, gg, wg, ba, bi, lam, h0b, w_out_b, g2, wr_t)


NBLK = S // LANES
BIG = 1.0e9


def _bf16_parts(a):
    hi = a.astype(BF16).astype(F32)
    r1 = a - hi
    mid = r1.astype(BF16).astype(F32)
    lo = (r1 - mid).astype(BF16).astype(F32)
    return hi, mid, lo


def _route_kernel(lg_ref, idx_ref, gate_ref, aff_s, lci_s, offb_s, ahi_s, amid_s, alo_s, offi_s):
    for b in range(B):
        l = lg_ref[b]
        ex = jnp.exp(l - jnp.max(l, axis=0, keepdims=True))
        aff_s[b * E:(b + 1) * E, :] = ex / jnp.sum(ex, axis=0, keepdims=True)
    aff = aff_s[...]
    nrow = B * E

    def bisect(_, lohi):
        lo, hi = lohi
        mid = lo + ((hi - lo + 1) >> 1)
        cnt = jnp.sum(jnp.where(aff >= pltpu.bitcast(mid, F32), 1.0, 0.0), axis=1, keepdims=True)
        ge = cnt >= float(CAP)
        return jnp.where(ge, mid, lo), jnp.where(ge, hi, mid - 1)

    lo0 = jnp.zeros((nrow, 1), jnp.int32)
    hi0 = jnp.full((nrow, 1), 0x7F800000, jnp.int32)
    thr_bits, _ = lax.fori_loop(0, 31, bisect, (lo0, hi0))
    thr = pltpu.bitcast(thr_bits, F32)
    gt = aff > thr
    eq = aff == thr
    need = float(CAP) - jnp.sum(jnp.where(gt, 1.0, 0.0), axis=1, keepdims=True)

    qi = lax.broadcasted_iota(jnp.int32, (LANES, LANES), 0)
    ti = lax.broadcasted_iota(jnp.int32, (LANES, LANES), 1)
    tri = jnp.where(qi <= ti, 1.0, 0.0).astype(BF16)
    blocks = [slice(j * LANES, (j + 1) * LANES) for j in range(NBLK)]

    ties_before = jnp.zeros((nrow, 1), F32)
    sel = []
    for sl in blocks:
        eqb = jnp.where(eq[:, sl], 1.0, 0.0)
        incl = jnp.dot(eqb.astype(BF16), tri, preferred_element_type=F32) + ties_before
        sel.append(gt[:, sl] | (eq[:, sl] & ((incl - eqb) < need)))
        ties_before = incl[:, LANES - 1:LANES]

    lane_sq = lax.broadcasted_iota(jnp.int32, (nrow, LANES), 1)
    off = jnp.zeros((nrow, 1), F32)
    offi = jnp.full((nrow, LANES), BIG, F32)
    for j, sl in enumerate(blocks):
        rows = slice(j * nrow, (j + 1) * nrow)
        lci = jnp.dot(jnp.where(sel[j], 1.0, 0.0).astype(BF16), tri, preferred_element_type=F32)
        lci_s[rows, :] = lci
        offb_s[rows, :] = jnp.broadcast_to(off, (nrow, LANES))
        ahi_s[rows, :], amid_s[rows, :], alo_s[rows, :] = _bf16_parts(aff[:, sl])
        off = off + lci[:, LANES - 1:LANES]
        offi = jnp.where(lane_sq == j, off, offi)
    offi_s[...] = offi

    slot = lax.broadcasted_iota(jnp.int32, (CAP, LANES), 0).astype(F32)
    lane = lax.broadcasted_iota(jnp.int32, (CAP, LANES), 1)
    lane_f = lane.astype(F32)
    zpad = jnp.zeros((LANES - NBLK, 5 * LANES), BF16)

    def row_body(r, carry):
        idxm, gm = carry
        take = lambda ref: ref[pl.ds(r, NBLK, stride=nrow), :]
        table = jnp.concatenate([take(lci_s), take(offb_s), take(ahi_s), take(amid_s), take(alo_s)], axis=1)
        table = jnp.concatenate([table.astype(BF16), zpad], axis=0)
        blk = jnp.sum(jnp.where(offi_s[pl.ds(r, 1), :] <= slot, 1.0, 0.0), axis=1, keepdims=True)
        pick = jnp.where(lane_f == blk, 1.0, 0.0).astype(BF16)
        res = jnp.dot(pick, table, preferred_element_type=F32)
        rank1 = (slot + 1.0) - res[:, LANES:2 * LANES]
        tokl = jnp.sum(jnp.where(res[:, 0:LANES] < rank1, 1.0, 0.0), axis=1, keepdims=True)
        affs = (res[:, 2 * LANES:3 * LANES] + res[:, 3 * LANES:4 * LANES]) + res[:, 4 * LANES:5 * LANES]
        gv = jnp.sum(jnp.where(lane_f == tokl, affs, 0.0), axis=1, keepdims=True)
        iv = blk * float(LANES) + tokl
        put = lane == r
        return jnp.where(put, iv, idxm), jnp.where(put, gv, gm)

    z = jnp.zeros((CAP, LANES), F32)
    idxm, gm = lax.fori_loop(0, nrow, row_body, (z, z), unroll=8)
    idx_ref[...] = (idxm.T * float(SUB_PER_TOK)).astype(jnp.int32)
    gate_ref[...] = gm


def _route(logits_t):
    full = lambda shape: pl.BlockSpec(shape, lambda i: (0,) * len(shape))
    return pl.pallas_call(
        _route_kernel,
        grid=(1,),
        in_specs=[full((B, E, S))],
        out_specs=[full((B * E, CAP)), full((CAP, B * E))],
        out_shape=[jax.ShapeDtypeStruct((B * E, CAP), jnp.int32), jax.ShapeDtypeStruct((CAP, B * E), F32)],
        scratch_shapes=[pltpu.VMEM((B * E, S), F32)] + [pltpu.VMEM((NBLK * B * E, LANES), F32)] * 5
                       + [pltpu.VMEM((B * E, LANES), F32)],
        compiler_params=_params(("arbitrary",), 48),
        name="route",
    )(logits_t)


def _dispatch_kernel(idx_ref, h_ref, xg_ref, xt0, xt1):
    for e in range(E):
        xt = xt0 if e % 2 == 0 else xt1
        for s in range(CAP):
            src = pl.multiple_of(idx_ref[e * CAP + s], SUB_PER_TOK)
            xt[s * SUB_PER_TOK:(s + 1) * SUB_PER_TOK, :] = h_ref[0, pl.ds(src, SUB_PER_TOK), :]
        for k in range(SUB_PER_TOK):
            xg_ref[e, :, k * LANES:(k + 1) * LANES] = xt[pl.ds(k, CAP, stride=SUB_PER_TOK), :].astype(BF16)


def _dispatch(idx1, hx2t):
    return pl.pallas_call(
        _dispatch_kernel,
        grid=(B,),
        in_specs=[pl.BlockSpec((E * CAP,), lambda b: (b,), memory_space=pltpu.SMEM),
                  pl.BlockSpec((1, S * SUB_PER_TOK, LANES), lambda b: (b, 0, 0))],
        out_specs=pl.BlockSpec((E, CAP, D), lambda b: (0, b, 0)),
        out_shape=jax.ShapeDtypeStruct((E, B * CAP, D), BF16),
        scratch_shapes=[pltpu.VMEM((CAP * SUB_PER_TOK, LANES), F32), pltpu.VMEM((CAP * SUB_PER_TOK, LANES), F32)],
        compiler_params=_params(("arbitrary",), 40),
        name="dispatch",
    )(idx1, hx2t)


def _moe_kernel(x_ref, gate_ref, w1_ref, w3_ref, w2_ref, y_ref, hid_s, w2_s):
    f = pl.program_id(1)
    w1 = w1_ref[0].astype(BF16)
    w3 = w3_ref[0].astype(BF16)
    fs = pl.multiple_of(f * FN, FN)
    w2_s[pl.ds(fs, FN), :] = w2_ref[0].astype(BF16)
    for c in range(B * CAP // MC):
        rs = slice(c * MC, (c + 1) * MC)
        x = x_ref[0, rs, :]
        h1 = jnp.dot(x, w1, preferred_element_type=F32)
        h3 = jnp.dot(x, w3, preferred_element_type=F32)
        hid_s[f, rs, :] = ((h1 * jax.nn.sigmoid(h1)) * h3).astype(BF16)

    @pl.when(f == FF // FN - 1)
    def _():
        gates = gate_ref[...]
        lane = lax.broadcasted_iota(jnp.int32, gates.shape, 1)
        e = pl.program_id(0)
        for c in range(B * CAP // MC):
            rs = slice(c * MC, (c + 1) * MC)
            hid = jnp.concatenate([hid_s[j, rs, :] for j in range(FF // FN)], axis=1)
            y = jnp.dot(hid, w2_s[...], preferred_element_type=F32)
            for bb in range(MC // CAP):
                b = c * (MC // CAP) + bb
                gcol = jnp.sum(jnp.where(lane == b * E + e, gates, 0.0), axis=1, keepdims=True)
                yb = y[bb * CAP:(bb + 1) * CAP, :] * gcol
                for k in range(SUB_PER_TOK):
                    y_ref[0, pl.ds(b * CAP * SUB_PER_TOK + k, CAP, stride=SUB_PER_TOK), :] = yb[:, k * LANES:(k + 1) * LANES]


def _moe(xg, gate_cols, w1, w3, w2):
    return pl.pallas_call(
        _moe_kernel,
        grid=(E, FF // FN),
        in_specs=[pl.BlockSpec((1, B * CAP, D), lambda e, f: (e, 0, 0)),
                  pl.BlockSpec((CAP, B * E), lambda e, f: (0, 0)),
                  pl.BlockSpec((1, D, FN), lambda e, f: (e, 0, f)),
                  pl.BlockSpec((1, D, FN), lambda e, f: (e, 0, f)),
                  pl.BlockSpec((1, FN, D), lambda e, f: (e, f, 0))],
        out_specs=pl.BlockSpec((1, B * CAP * SUB_PER_TOK, LANES), lambda e, f: (e, 0, 0)),
        out_shape=jax.ShapeDtypeStruct((E, B * CAP * SUB_PER_TOK, LANES), F32),
        scratch_shapes=[pltpu.VMEM((FF // FN, B * CAP, FN), BF16), pltpu.VMEM((FF, D), BF16)],
        compiler_params=_params(("arbitrary", "arbitrary"), 58),
        name="moe",
    )(xg, gate_cols, w1, w3, w2)


def _combine_kernel(idx_ref, y_ref, x1_ref, mod_ref, fg_ref, out_ref, acc):
    b = pl.program_id(0)
    g = pl.program_id(1)

    @pl.when(g == 0)
    def _():
        acc[...] = jnp.zeros(acc.shape, F32)

    @pl.when(g < NG)
    def _():
        for el in range(EG):
            for s0 in range(0, CAP, SCATTER_BATCH):
                dsts, news = [], []
                for j in range(SCATTER_BATCH):
                    dst = pl.multiple_of(idx_ref[el * CAP + s0 + j], SUB_PER_TOK)
                    src = (s0 + j) * SUB_PER_TOK
                    dsts.append(dst)
                    news.append(acc[pl.ds(dst, SUB_PER_TOK), :] + y_ref[el, src:src + SUB_PER_TOK, :])
                for dst, new in zip(dsts, news):
                    acc[pl.ds(dst, SUB_PER_TOK), :] = new

    @pl.when(g >= NG)
    def _():
        row0 = pl.multiple_of((g - NG) * (FIN_ROWS * SUB_PER_TOK), FIN_ROWS * SUB_PER_TOK)
        g2x = mod_ref[pl.ds(b, 1), 5 * D:6 * D]
        ssq = jnp.zeros((FIN_ROWS, 1), F32)
        for k in range(SUB_PER_TOK):
            ls = slice(k * LANES, (k + 1) * LANES)
            xo = x1_ref[0, :, ls] + g2x[:, ls] * acc[pl.ds(row0 + k, FIN_ROWS, stride=SUB_PER_TOK), :]
            out_ref[0, :, ls] = xo
            ssq = ssq + jnp.sum(xo * xo, axis=1, keepdims=True)
        inv = lax.rsqrt(ssq * (1.0 / D) + EPS)
        out_ref[0] = (out_ref[0] * inv) * fg_ref[...]


def _combine(idx1, y, x1, mod, fg):
    fin = lambda b, g: (b, jnp.maximum(g - NG, 0), 0)
    grp = lambda g: jnp.minimum(g, NG - 1)
    return pl.pallas_call(
        _combine_kernel,
        grid=(B, NG + S // FIN_ROWS),
        in_specs=[pl.BlockSpec((EG * CAP,), lambda b, g: (b * NG + grp(g),), memory_space=pltpu.SMEM),
                  pl.BlockSpec((EG, CAP * SUB_PER_TOK, LANES), lambda b, g: (grp(g), b, 0)),
                  pl.BlockSpec((1, FIN_ROWS, D), fin),
                  pl.BlockSpec((2 * B, N_MOD * D), lambda b, g: (0, 0)),
                  pl.BlockSpec((1, D), lambda b, g: (0, 0))],
        out_specs=pl.BlockSpec((1, FIN_ROWS, D), fin),
        out_shape=jax.ShapeDtypeStruct((B, S, D), F32),
        scratch_shapes=[pltpu.VMEM((S * SUB_PER_TOK, LANES), F32)],
        compiler_params=_params(("arbitrary", "arbitrary"), 56),
        name="combine",
    )(idx1, y, x1, mod, fg)


def _pack_gate_weights(wa, wi):
    eye = jnp.eye(4, dtype=wa.dtype)

    def bdiag(w4):
        return jnp.einsum('hij,hg->higj', w4, eye).reshape(4 * LRU_HD, 4 * LRU_HD)

    dirs = []
    for d in range(2):
        halves = []
        for k in range(2):
            hs = slice(4 * k, 4 * (k + 1))
            halves.append(jnp.concatenate([bdiag(wa[d, hs]), bdiag(wi[d, hs])], axis=1))
        dirs.append(jnp.stack(halves))
    return jnp.stack(dirs).astype(BF16)


def kernel(x, c, ctx, c_ctx, w_mod, b_mod, norm1_g, norm2_g, w_in, sgu_g, sgu_w, sgu_b, conv_w, conv_b,
           rg_wa, rg_ba, rg_wi, rg_bi, rg_lam, w_out, w_router, w1, w3, w2, final_g):
    assert x.shape == (B, S, D) and ctx.shape == (B, LC, D) and w_mod.shape[0] == 1

    cc = jnp.concatenate([c, c_ctx[None, :], jnp.zeros((B - 1, D), F32)], axis=0)
    mod = _modulation(cc, w_mod[0], b_mod[0][None, :])

    g1 = norm1_g[0][None, :]
    g2 = norm2_g[0][None, :]
    w_in_b = w_in[0].astype(BF16)
    w_in_x = w_in_b[:, 2 * DS:2 * DS + DL]
    wg = _pack_gate_weights(rg_wa[0], rg_wi[0])
    cw = conv_w[0]
    cb = conv_b[0][None, :]
    ba, bi, lam = rg_ba[0], rg_bi[0], rg_lam[0]
    sbias = jnp.repeat(sgu_b[0].T, HD, axis=1)

    h0f, h0b = _context_states(ctx, mod, g1, w_in_x, cw, cb, wg, ba, bi, lam)
    sgu, gg, cx, hf = _mix_in(x, mod, g1, w_in_b, sgu_g[0][None, :], sgu_w[0].astype(BF16), sbias, cw, cb,
                              wg, ba, bi, lam, h0f)
    x1, hx2t, logits_t = _mix_out(x, mod, cx, hf, sgu, gg, wg, ba, bi, lam, h0b, w_out[0].astype(BF16), g2,
                                  w_router[0].T)
    idx, gate = _route(logits_t)
    idx1 = idx.reshape(B * E * CAP)
    xg = _dispatch(idx1, hx2t)
    y = _moe(xg, gate, w1[0], w3[0], w2[0])
    return _combine(idx1, y, x1, mod, final_g[None, :])
```

```python
import jax
import jax.numpy as jnp
from jax import lax
from jax.experimental import pallas as pl
from jax.experimental.pallas import tpu as pltpu

F32 = jnp.float32
BF16 = jnp.bfloat16

D = 1024
B = 8
S = 2048
LC = 256
GRID_W = 64
DS = 512
NH = 4
HD = DS // NH
CHUNK = 128
DL = 512
LRU_HEADS = 8
LRU_HD = DL // LRU_HEADS
E = 16
CAP = 2 * S // E
FF = 2048
N_MOD = 6
EPS = 1e-6
RG_C = 8.0

SUBLANES = 8
LANES = 128
VMEM_LIMIT_V7X = 60000 * 1024

TP = CHUNK
ROWS = TP * B
N_TT = S // TP
SUB_PER_TOK = D // LANES
PROJ_GROUPS = 2
FN = 512
MC = 512
SCATTER_BATCH = 16
EG = 8
NG = E // EG
FIN_ROWS = 1024
PAD_F = SUBLANES
PAD_B = 2 * SUBLANES


def _params(sem, vmem_mb):
    return pltpu.CompilerParams(dimension_semantics=sem, vmem_limit_bytes=min(vmem_mb << 20, VMEM_LIMIT_V7X))


GELU_C1 = 0.7978845608028654
GELU_C2 = GELU_C1 * 0.044715


def _gelu(x):
    half = 0.5 * x
    return half + half * jnp.tanh(x * (GELU_C1 + GELU_C2 * (x * x)))


def _rms_mod(x, gs, sh):
    return (x * lax.rsqrt(jnp.mean(x * x, axis=-1, keepdims=True) + EPS)) * gs + sh


def _log_sigmoid(x):
    return -(jnp.maximum(-x, 0.0) + jnp.log1p(jnp.exp(-jnp.abs(x))))


def _conv_interleaved(xx, xi_ref, tp, period, cw, cb):
    rows = tp * B
    nslab = DL // LANES
    xi_ref[:, 0:PAD_F, :] = jnp.zeros((nslab, PAD_F, LANES), F32)
    xi_ref[:, PAD_F + rows:PAD_F + rows + PAD_B, :] = jnp.zeros((nslab, PAD_B, LANES), F32)
    for k in range(nslab):
        for b in range(B):
            xi_ref[k, pl.ds(PAD_F + b, tp, stride=B), :] = xx[b * tp:(b + 1) * tp, k * LANES:(k + 1) * LANES]
    pos = lax.broadcasted_iota(jnp.int32, (rows, LANES), 0) >> 3
    pm = pos & (period - 1)
    m0 = pm != 0
    m2 = pm != period - 1
    m3 = pm < period - 2
    outs = []
    for k in range(nslab):
        w = cw[:, k * LANES:(k + 1) * LANES]
        t0 = xi_ref[k, 0:rows, :]
        t1 = xi_ref[k, SUBLANES:SUBLANES + rows, :]
        t2 = xi_ref[k, 2 * SUBLANES:2 * SUBLANES + rows, :]
        t3 = xi_ref[k, 3 * SUBLANES:3 * SUBLANES + rows, :]
        acc = jnp.where(m0, t0, 0.0) * w[0:1] + t1 * w[1:2]
        acc = acc + jnp.where(m2, t2, 0.0) * w[2:3] + jnp.where(m3, t3, 0.0) * w[3:4]
        outs.append(acc + cb[:, k * LANES:(k + 1) * LANES])
    return jnp.concatenate(outs, axis=1)


def _rglru_coeffs(cx, wg_ref, d, ba_half, bi_half, lam):
    cxb = cx.astype(BF16)
    half = DL // 2
    a_parts, b_parts = [], []
    for k in range(2):
        sl = slice(half * k, half * (k + 1))
        zh = jnp.dot(cxb[:, sl], wg_ref[d, k], preferred_element_type=F32)
        t_r = jnp.tanh(zh[:, :half] + ba_half[:, sl])
        t_i = jnp.tanh(zh[:, half:] + bi_half[:, sl])
        ch = (0.5 * RG_C) * -_log_sigmoid(lam[:, sl])
        neg_log_a = ch + ch * t_r
        a = jnp.exp(-neg_log_a)
        one_minus_a2 = jnp.tanh(neg_log_a) * (a * a + 1.0)
        root = jnp.where(one_minus_a2 > 0.0, one_minus_a2 * lax.rsqrt(one_minus_a2), 0.0)
        xh = 0.5 * cx[:, sl]
        a_parts.append(a)
        b_parts.append(root * (xh + xh * t_i))
    return jnp.concatenate(a_parts, axis=1), jnp.concatenate(b_parts, axis=1)


def _scan(a, b, h_out_ref, h0, nsteps, reverse):
    h = h0
    for p in (range(nsteps - 1, -1, -1) if reverse else range(nsteps)):
        rs = slice(p * SUBLANES, (p + 1) * SUBLANES)
        h = a[rs, :] * h + b[rs, :]
        if h_out_ref is not None:
            h_out_ref[rs, :] = h
    return h


def _mod_kernel(c_ref, w_ref, b_ref, o_ref):
    c = c_ref[...]
    s = c * jax.nn.sigmoid(c)
    s_hi = s.astype(BF16)
    s_lo = (s - s_hi.astype(F32)).astype(BF16)
    w = w_ref[...]
    w_hi = w.astype(BF16)
    w_lo = (w - w_hi.astype(F32)).astype(BF16)
    nrow = s.shape[0]
    both = jnp.dot(jnp.concatenate([s_hi, s_lo], axis=0), w_hi, preferred_element_type=F32)
    o_ref[...] = (both[0:nrow] + (both[nrow:] + jnp.dot(s_hi, w_lo, preferred_element_type=F32))) + b_ref[...]


def _modulation(cc, w_mod, b_mod):
    nt = 4
    tn = N_MOD * D // nt
    return pl.pallas_call(
        _mod_kernel,
        grid=(nt,),
        in_specs=[pl.BlockSpec((2 * B, D), lambda i: (0, 0)),
                  pl.BlockSpec((D, tn), lambda i: (0, i)),
                  pl.BlockSpec((1, tn), lambda i: (0, i))],
        out_specs=pl.BlockSpec((2 * B, tn), lambda i: (0, i)),
        out_shape=jax.ShapeDtypeStruct((2 * B, N_MOD * D), F32),
        compiler_params=_params(("arbitrary",), 32),
        name="mod",
    )(cc, w_mod, b_mod)


def _ctx_kernel(ctx_ref, mod_ref, g1_ref, win_ref, cw_ref, cb_ref, wg_ref, ba_ref, bi_ref, lam_ref,
                hf_ref, hb_ref, xi_ref):
    sh = mod_ref[B:B + 1, 0:D]
    gs = g1_ref[...] * (1.0 + mod_ref[B:B + 1, D:2 * D])
    parts = []
    for b in range(B):
        hc = _rms_mod(ctx_ref[b], gs, sh).astype(BF16)
        parts.append(jnp.dot(hc, win_ref[...], preferred_element_type=F32))
    xx = jnp.concatenate(parts, axis=0)
    cx = _conv_interleaved(xx, xi_ref, LC, LC, cw_ref[...], cb_ref[...])
    for d in range(2):
        a, dr = _rglru_coeffs(cx, wg_ref, d, ba_ref[d:d + 1, :], bi_ref[d:d + 1, :], lam_ref[d:d + 1, :])
        h = _scan(a, dr, None, jnp.zeros((B, DL), F32), LC, reverse=(d == 1))
        if d == 0:
            hf_ref[...] = h
        else:
            hb_ref[...] = h


def _context_states(ctx, mod, g1, w_in_x, cw, cb, wg, ba, bi, lam):
    rows = LC * B
    full = lambda shape: pl.BlockSpec(shape, lambda i: (0,) * len(shape))
    return pl.pallas_call(
        _ctx_kernel,
        grid=(1,),
        in_specs=[full((B, LC, D)), full((2 * B, N_MOD * D)), full((1, D)), full((D, DL)), full((4, DL)),
                  full((1, DL)), full((2, 2, DL // 2, DL)), full((2, DL)), full((2, DL)), full((2, DL))],
        out_specs=[full((B, DL)), full((B, DL))],
        out_shape=[jax.ShapeDtypeStruct((B, DL), F32)] * 2,
        scratch_shapes=[pltpu.VMEM((DL // LANES, PAD_F + rows + PAD_B, LANES), F32)],
        compiler_params=_params(("arbitrary",), 56),
        name="ctx",
    )(ctx, mod, g1, w_in_x, cw, cb, wg, ba, bi, lam)


def _mix_in_kernel(x_ref, mod_ref, g1_ref, win_ref, sg_ref, sw_ref, sbias_ref, cw_ref, cb_ref, wg_ref,
                   ba_ref, bi_ref, lam_ref, h0_ref,
                   sgu_ref, gg_ref, cx_ref, hf_ref,
                   hbuf, xi_ref, carry):
    @pl.when(pl.program_id(0) == 0)
    def _():
        carry[...] = h0_ref[...]

    g1 = g1_ref[...]
    for b in range(B):
        sh = mod_ref[b:b + 1, 0:D]
        gs = g1 * (1.0 + mod_ref[b:b + 1, D:2 * D])
        hbuf[b * TP:(b + 1) * TP, :] = _rms_mod(x_ref[b], gs, sh).astype(BF16)
    z = jnp.dot(hbuf[...], win_ref[...], preferred_element_type=F32)
    ug = _gelu(z[:, 0:DS])
    vg = _gelu(z[:, DS:2 * DS])
    xx = z[:, 2 * DS:2 * DS + DL]
    gg = _gelu(z[:, 2 * DS + DL:])
    for b in range(B):
        gg_ref[b] = gg[b * TP:(b + 1) * TP, :].astype(BF16)

    for h in range(NH):
        hs = slice(h * HD, (h + 1) * HD)
        vh = vg[:, hs]
        vn = (vh * lax.rsqrt(jnp.mean(vh * vh, axis=-1, keepdims=True) + EPS)) * sg_ref[:, hs]
        vnb = vn.astype(BF16)
        for b in range(B):
            rs = slice(b * TP, (b + 1) * TP)
            s = jnp.dot(sw_ref[h], vnb[rs], preferred_element_type=F32) + sbias_ref[:, hs]
            sgu_ref[b, :, hs] = (ug[rs, hs] * s).astype(BF16)

    cx = _conv_interleaved(xx, xi_ref, TP, GRID_W, cw_ref[...], cb_ref[...])
    cx_ref[...] = cx
    a, dr = _rglru_coeffs(cx, wg_ref, 0, ba_ref[0:1, :], bi_ref[0:1, :], lam_ref[0:1, :])
    carry[...] = _scan(a, dr, hf_ref, carry[...], TP, reverse=False)


def _mix_in(x, mod, g1, w_in_b, sg, sw_b, sbias, cw, cb, wg, ba, bi, lam, h0f):
    full = lambda shape: pl.BlockSpec(shape, lambda i: (0,) * len(shape))
    return pl.pallas_call(
        _mix_in_kernel,
        grid=(N_TT,),
        in_specs=[pl.BlockSpec((B, TP, D), lambda i: (0, i, 0)),
                  full((2 * B, N_MOD * D)), full((1, D)), full((D, 2 * DS + 2 * DL)), full((1, DS)),
                  full((NH, CHUNK, CHUNK)), full((CHUNK, DS)), full((4, DL)), full((1, DL)),
                  full((2, 2, DL // 2, DL)), full((2, DL)), full((2, DL)), full((2, DL)), full((B, DL))],
        out_specs=[pl.BlockSpec((B, TP, DS), lambda i: (0, i, 0)),
                   pl.BlockSpec((B, TP, DL), lambda i: (0, i, 0)),
                   pl.BlockSpec((ROWS, DL), lambda i: (i, 0)),
                   pl.BlockSpec((ROWS, DL), lambda i: (i, 0))],
        out_shape=[jax.ShapeDtypeStruct((B, S, DS), BF16), jax.ShapeDtypeStruct((B, S, DL), BF16),
                   jax.ShapeDtypeStruct((S * B, DL), F32), jax.ShapeDtypeStruct((S * B, DL), F32)],
        scratch_shapes=[pltpu.VMEM((ROWS, D), BF16),
                        pltpu.VMEM((DL // LANES, PAD_F + ROWS + PAD_B, LANES), F32),
                        pltpu.VMEM((B, DL), F32)],
        compiler_params=_params(("arbitrary",), 56),
        name="mix_in",
    )(x, mod, g1, w_in_b, sg, sw_b, sbias, cw, cb, wg, ba, bi, lam, h0f)


def _mix_out_kernel(x_ref, mod_ref, cx_ref, hf_ref, sgu_ref, gg_ref, wg_ref, ba_ref, bi_ref, lam_ref, h0_ref,
                    wout_ref, g2_ref, wr_ref,
                    x1_ref, hx2t_ref, lg_ref,
                    hs_ref, carry):
    @pl.when(pl.program_id(0) == 0)
    def _():
        carry[...] = h0_ref[...]

    a, dr = _rglru_coeffs(cx_ref[...], wg_ref, 1, ba_ref[1:2, :], bi_ref[1:2, :], lam_ref[1:2, :])
    nslab = DL // LANES
    h = carry[...]
    for p in range(TP - 1, -1, -1):
        rs = slice(p * SUBLANES, (p + 1) * SUBLANES)
        h = a[rs, :] * h + dr[rs, :]
        hsum = hf_ref[rs, :] + h
        for k in range(nslab):
            hs_ref[k, rs, :] = hsum[:, k * LANES:(k + 1) * LANES]
    carry[...] = h
    rows = []
    for b in range(B):
        rec = [(gg_ref[b, :, k * LANES:(k + 1) * LANES].astype(F32)
                * hs_ref[k, pl.ds(b, TP, stride=B), :]).astype(BF16) for k in range(nslab)]
        rows.append(jnp.concatenate([sgu_ref[b]] + rec, axis=1))
    gb = B // PROJ_GROUPS
    ys = [jnp.dot(jnp.concatenate(rows[g * gb:(g + 1) * gb], axis=0), wout_ref[...], preferred_element_type=F32)
          for g in range(PROJ_GROUPS)]

    g2 = g2_ref[...]
    wr = wr_ref[...]
    wr_hi = wr.astype(BF16)
    wr_lo = (wr - wr_hi.astype(F32)).astype(BF16)
    for b in range(B):
        g1x = mod_ref[b:b + 1, 2 * D:3 * D]
        sh2 = mod_ref[b:b + 1, 3 * D:4 * D]
        gs2 = g2 * (1.0 + mod_ref[b:b + 1, 4 * D:5 * D])
        x1 = x_ref[b] + g1x * ys[b // gb][(b % gb) * TP:(b % gb + 1) * TP, :]
        x1_ref[b] = x1
        hx2 = _rms_mod(x1, gs2, sh2)
        for k in range(SUB_PER_TOK):
            hx2t_ref[b, pl.ds(k, TP, stride=SUB_PER_TOK), :] = hx2[:, k * LANES:(k + 1) * LANES]
        hx_hi = hx2.astype(BF16)
        hx_lo = (hx2 - hx_hi.astype(F32)).astype(BF16)
        nt = (((1,), (1,)), ((), ()))
        lg_ref[b] = (lax.dot_general(wr_hi, hx_hi, nt, preferred_element_type=F32)
                     + (lax.dot_general(wr_hi, hx_lo, nt, preferred_element_type=F32)
                        + lax.dot_general(wr_lo, hx_hi, nt, preferred_element_type=F32)))


def _mix_out(x, mod, cx, hf, sgu, gg, wg, ba, bi, lam, h0b, w_out_b, g2, wr_t):
    full = lambda shape: pl.BlockSpec(shape, lambda i: (0,) * len(shape))
    rev = lambda i: N_TT - 1 - i
    return pl.pallas_call(
        _mix_out_kernel,
        grid=(N_TT,),
        in_specs=[pl.BlockSpec((B, TP, D), lambda i: (0, rev(i), 0)),
                  full((2 * B, N_MOD * D)),
                  pl.BlockSpec((ROWS, DL), lambda i: (rev(i), 0)),
                  pl.BlockSpec((ROWS, DL), lambda i: (rev(i), 0)),
                  pl.BlockSpec((B, TP, DS), lambda i: (0, rev(i), 0)),
                  pl.BlockSpec((B, TP, DL), lambda i: (0, rev(i), 0)),
                  full((2, 2, DL // 2, DL)), full((2, DL)), full((2, DL)), full((2, DL)), full((B, DL)),
                  full((D, D)), full((1, D)), full((E, D))],
        out_specs=[pl.BlockSpec((B, TP, D), lambda i: (0, rev(i), 0)),
                   pl.BlockSpec((B, TP * SUB_PER_TOK, LANES), lambda i: (0, rev(i), 0)),
                   pl.BlockSpec((B, E, TP), lambda i: (0, 0, rev(i)))],
        out_shape=[jax.ShapeDtypeStruct((B, S, D), F32),
                   jax.ShapeDtypeStruct((B, S * SUB_PER_TOK, LANES), F32),
                   jax.ShapeDtypeStruct((B, E, S), F32)],
        scratch_shapes=[pltpu.VMEM((DL // LANES, ROWS, LANES), F32),
                        pltpu.VMEM((B, DL), F32)],
        compiler_params=_params(("arbitrary",), 56),
        name="mix_out",
    )(x, mod, cx, hf, sgu, gg, wg, ba, bi, lam, h0b, w_out_b, g2, wr_t)


NBLK = S // LANES
BIG = 1.0e9


def _bf16_parts(a):
    hi = a.astype(BF16).astype(F32)
    r1 = a - hi
    mid = r1.astype(BF16).astype(F32)
    lo = (r1 - mid).astype(BF16).astype(F32)
    return hi, mid, lo


def _route_kernel(lg_ref, idx_ref, gate_ref, aff_s, lci_s, offb_s, ahi_s, amid_s, alo_s, offi_s):
    for b in range(B):
        l = lg_ref[b]
        ex = jnp.exp(l - jnp.max(l, axis=0, keepdims=True))
        aff_s[b * E:(b + 1) * E, :] = ex / jnp.sum(ex, axis=0, keepdims=True)
    aff = aff_s[...]
    nrow = B * E

    def bisect(_, lohi):
        lo, hi = lohi
        mid = lo + ((hi - lo + 1) >> 1)
        cnt = jnp.sum(jnp.where(aff >= pltpu.bitcast(mid, F32), 1.0, 0.0), axis=1, keepdims=True)
        ge = cnt >= float(CAP)
        return jnp.where(ge, mid, lo), jnp.where(ge, hi, mid - 1)

    lo0 = jnp.zeros((nrow, 1), jnp.int32)
    hi0 = jnp.full((nrow, 1), 0x7F800000, jnp.int32)
    thr_bits, _ = lax.fori_loop(0, 31, bisect, (lo0, hi0))
    thr = pltpu.bitcast(thr_bits, F32)
    gt = aff > thr
    eq = aff == thr
    need = float(CAP) - jnp.sum(jnp.where(gt, 1.0, 0.0), axis=1, keepdims=True)

    qi = lax.broadcasted_iota(jnp.int32, (LANES, LANES), 0)
    ti = lax.broadcasted_iota(jnp.int32, (LANES, LANES), 1)
    tri = jnp.where(qi <= ti, 1.0, 0.0).astype(BF16)
    blocks = [slice(j * LANES, (j + 1) * LANES) for j in range(NBLK)]

    ties_before = jnp.zeros((nrow, 1), F32)
    sel = []
    for sl in blocks:
        eqb = jnp.where(eq[:, sl], 1.0, 0.0)
        incl = jnp.dot(eqb.astype(BF16), tri, preferred_element_type=F32) + ties_before
        sel.append(gt[:, sl] | (eq[:, sl] & ((incl - eqb) < need)))
        ties_before = incl[:, LANES - 1:LANES]

    lane_sq = lax.broadcasted_iota(jnp.int32, (nrow, LANES), 1)
    off = jnp.zeros((nrow, 1), F32)
    offi = jnp.full((nrow, LANES), BIG, F32)
    for j, sl in enumerate(blocks):
        rows = slice(j * nrow, (j + 1) * nrow)
        lci = jnp.dot(jnp.where(sel[j], 1.0, 0.0).astype(BF16), tri, preferred_element_type=F32)
        lci_s[rows, :] = lci
        offb_s[rows, :] = jnp.broadcast_to(off, (nrow, LANES))
        ahi_s[rows, :], amid_s[rows, :], alo_s[rows, :] = _bf16_parts(aff[:, sl])
        off = off + lci[:, LANES - 1:LANES]
        offi = jnp.where(lane_sq == j, off, offi)
    offi_s[...] = offi

    slot = lax.broadcasted_iota(jnp.int32, (CAP, LANES), 0).astype(F32)
    lane = lax.broadcasted_iota(jnp.int32, (CAP, LANES), 1)
    lane_f = lane.astype(F32)
    zpad = jnp.zeros((LANES - NBLK, 5 * LANES), BF16)

    def row_body(r, carry):
        idxm, gm = carry
        take = lambda ref: ref[pl.ds(r, NBLK, stride=nrow), :]
        table = jnp.concatenate([take(lci_s), take(offb_s), take(ahi_s), take(amid_s), take(alo_s)], axis=1)
        table = jnp.concatenate([table.astype(BF16), zpad], axis=0)
        blk = jnp.sum(jnp.where(offi_s[pl.ds(r, 1), :] <= slot, 1.0, 0.0), axis=1, keepdims=True)
        pick = jnp.where(lane_f == blk, 1.0, 0.0).astype(BF16)
        res = jnp.dot(pick, table, preferred_element_type=F32)
        rank1 = (slot + 1.0) - res[:, LANES:2 * LANES]
        tokl = jnp.sum(jnp.where(res[:, 0:LANES] < rank1, 1.0, 0.0), axis=1, keepdims=True)
        affs = (res[:, 2 * LANES:3 * LANES] + res[:, 3 * LANES:4 * LANES]) + res[:, 4 * LANES:5 * LANES]
        gv = jnp.sum(jnp.where(lane_f == tokl, affs, 0.0), axis=1, keepdims=True)
        iv = blk * float(LANES) + tokl
        put = lane == r
        return jnp.where(put, iv, idxm), jnp.where(put, gv, gm)

    z = jnp.zeros((CAP, LANES), F32)
    idxm, gm = lax.fori_loop(0, nrow, row_body, (z, z), unroll=8)
    idx_ref[...] = (idxm.T * float(SUB_PER_TOK)).astype(jnp.int32)
    gate_ref[...] = gm


def _route(logits_t):
    full = lambda shape: pl.BlockSpec(shape, lambda i: (0,) * len(shape))
    return pl.pallas_call(
        _route_kernel,
        grid=(1,),
        in_specs=[full((B, E, S))],
        out_specs=[full((B * E, CAP)), full((CAP, B * E))],
        out_shape=[jax.ShapeDtypeStruct((B * E, CAP), jnp.int32), jax.ShapeDtypeStruct((CAP, B * E), F32)],
        scratch_shapes=[pltpu.VMEM((B * E, S), F32)] + [pltpu.VMEM((NBLK * B * E, LANES), F32)] * 5
                       + [pltpu.VMEM((B * E, LANES), F32)],
        compiler_params=_params(("arbitrary",), 48),
        name="route",
    )(logits_t)


def _dispatch_kernel(idx_ref, h_ref, xg_ref, xt0, xt1):
    for e in range(E):
        xt = xt0 if e % 2 == 0 else xt1
        for s in range(CAP):
            src = pl.multiple_of(idx_ref[e * CAP + s], SUB_PER_TOK)
            xt[s * SUB_PER_TOK:(s + 1) * SUB_PER_TOK, :] = h_ref[0, pl.ds(src, SUB_PER_TOK), :]
        for k in range(SUB_PER_TOK):
            xg_ref[e, :, k * LANES:(k + 1) * LANES] = xt[pl.ds(k, CAP, stride=SUB_PER_TOK), :].astype(BF16)


def _dispatch(idx1, hx2t):
    return pl.pallas_call(
        _dispatch_kernel,
        grid=(B,),
        in_specs=[pl.BlockSpec((E * CAP,), lambda b: (b,), memory_space=pltpu.SMEM),
                  pl.BlockSpec((1, S * SUB_PER_TOK, LANES), lambda b: (b, 0, 0))],
        out_specs=pl.BlockSpec((E, CAP, D), lambda b: (0, b, 0)),
        out_shape=jax.ShapeDtypeStruct((E, B * CAP, D), BF16),
        scratch_shapes=[pltpu.VMEM((CAP * SUB_PER_TOK, LANES), F32), pltpu.VMEM((CAP * SUB_PER_TOK, LANES), F32)],
        compiler_params=_params(("arbitrary",), 40),
        name="dispatch",
    )(idx1, hx2t)


def _moe_kernel(x_ref, gate_ref, w1_ref, w3_ref, w2_ref, y_ref, hid_s, w2_s):
    f = pl.program_id(1)
    w1 = w1_ref[0].astype(BF16)
    w3 = w3_ref[0].astype(BF16)
    fs = pl.multiple_of(f * FN, FN)
    w2_s[pl.ds(fs, FN), :] = w2_ref[0].astype(BF16)
    for c in range(B * CAP // MC):
        rs = slice(c * MC, (c + 1) * MC)
        x = x_ref[0, rs, :]
        h1 = jnp.dot(x, w1, preferred_element_type=F32)
        h3 = jnp.dot(x, w3, preferred_element_type=F32)
        hid_s[f, rs, :] = ((h1 * jax.nn.sigmoid(h1)) * h3).astype(BF16)

    @pl.when(f == FF // FN - 1)
    def _():
        gates = gate_ref[...]
        lane = lax.broadcasted_iota(jnp.int32, gates.shape, 1)
        e = pl.program_id(0)
        for c in range(B * CAP // MC):
            rs = slice(c * MC, (c + 1) * MC)
            hid = jnp.concatenate([hid_s[j, rs, :] for j in range(FF // FN)], axis=1)
            y = jnp.dot(hid, w2_s[...], preferred_element_type=F32)
            for bb in range(MC // CAP):
                b = c * (MC // CAP) + bb
                gcol = jnp.sum(jnp.where(lane == b * E + e, gates, 0.0), axis=1, keepdims=True)
                yb = y[bb * CAP:(bb + 1) * CAP, :] * gcol
                for k in range(SUB_PER_TOK):
                    y_ref[0, pl.ds(b * CAP * SUB_PER_TOK + k, CAP, stride=SUB_PER_TOK), :] = yb[:, k * LANES:(k + 1) * LANES]


def _moe(xg, gate_cols, w1, w3, w2):
    return pl.pallas_call(
        _moe_kernel,
        grid=(E, FF // FN),
        in_specs=[pl.BlockSpec((1, B * CAP, D), lambda e, f: (e, 0, 0)),
                  pl.BlockSpec((CAP, B * E), lambda e, f: (0, 0)),
                  pl.BlockSpec((1, D, FN), lambda e, f: (e, 0, f)),
                  pl.BlockSpec((1, D, FN), lambda e, f: (e, 0, f)),
                  pl.BlockSpec((1, FN, D), lambda e, f: (e, f, 0))],
        out_specs=pl.BlockSpec((1, B * CAP * SUB_PER_TOK, LANES), lambda e, f: (e, 0, 0)),
        out_shape=jax.ShapeDtypeStruct((E, B * CAP * SUB_PER_TOK, LANES), F32),
        scratch_shapes=[pltpu.VMEM((FF // FN, B * CAP, FN), BF16), pltpu.VMEM((FF, D), BF16)],
        compiler_params=_params(("arbitrary", "arbitrary"), 58),
        name="moe",
    )(xg, gate_cols, w1, w3, w2)


def _combine_kernel(idx_ref, y_ref, x1_ref, mod_ref, fg_ref, out_ref, acc):
    b = pl.program_id(0)
    g = pl.program_id(1)

    @pl.when(g == 0)
    def _():
        acc[...] = jnp.zeros(acc.shape, F32)

    @pl.when(g < NG)
    def _():
        for el in range(EG):
            for s0 in range(0, CAP, SCATTER_BATCH):
                dsts, news = [], []
                for j in range(SCATTER_BATCH):
                    dst = pl.multiple_of(idx_ref[el * CAP + s0 + j], SUB_PER_TOK)
                    src = (s0 + j) * SUB_PER_TOK
                    dsts.append(dst)
                    news.append(acc[pl.ds(dst, SUB_PER_TOK), :] + y_ref[el, src:src + SUB_PER_TOK, :])
                for dst, new in zip(dsts, news):
                    acc[pl.ds(dst, SUB_PER_TOK), :] = new

    @pl.when(g >= NG)
    def _():
        row0 = pl.multiple_of((g - NG) * (FIN_ROWS * SUB_PER_TOK), FIN_ROWS * SUB_PER_TOK)
        g2x = mod_ref[pl.ds(b, 1), 5 * D:6 * D]
        ssq = jnp.zeros((FIN_ROWS, 1), F32)
        for k in range(SUB_PER_TOK):
            ls = slice(k * LANES, (k + 1) * LANES)
            xo = x1_ref[0, :, ls] + g2x[:, ls] * acc[pl.ds(row0 + k, FIN_ROWS, stride=SUB_PER_TOK), :]
            out_ref[0, :, ls] = xo
            ssq = ssq + jnp.sum(xo * xo, axis=1, keepdims=True)
        inv = lax.rsqrt(ssq * (1.0 / D) + EPS)
        out_ref[0] = (out_ref[0] * inv) * fg_ref[...]


def _combine(idx1, y, x1, mod, fg):
    fin = lambda b, g: (b, jnp.maximum(g - NG, 0), 0)
    grp = lambda g: jnp.minimum(g, NG - 1)
    return pl.pallas_call(
        _combine_kernel,
        grid=(B, NG + S // FIN_ROWS),
        in_specs=[pl.BlockSpec((EG * CAP,), lambda b, g: (b * NG + grp(g),), memory_space=pltpu.SMEM),
                  pl.BlockSpec((EG, CAP * SUB_PER_TOK, LANES), lambda b, g: (grp(g), b, 0)),
                  pl.BlockSpec((1, FIN_ROWS, D), fin),
                  pl.BlockSpec((2 * B, N_MOD * D), lambda b, g: (0, 0)),
                  pl.BlockSpec((1, D), lambda b, g: (0, 0))],
        out_specs=pl.BlockSpec((1, FIN_ROWS, D), fin),
        out_shape=jax.ShapeDtypeStruct((B, S, D), F32),
        scratch_shapes=[pltpu.VMEM((S * SUB_PER_TOK, LANES), F32)],
        compiler_params=_params(("arbitrary", "arbitrary"), 56),
        name="combine",
    )(idx1, y, x1, mod, fg)


def _pack_gate_weights(wa, wi):
    eye = jnp.eye(4, dtype=wa.dtype)

    def bdiag(w4):
        return jnp.einsum('hij,hg->higj', w4, eye).reshape(4 * LRU_HD, 4 * LRU_HD)

    dirs = []
    for d in range(2):
        halves = []
        for k in range(2):
            hs = slice(4 * k, 4 * (k + 1))
            halves.append(jnp.concatenate([bdiag(wa[d, hs]), bdiag(wi[d, hs])], axis=1))
        dirs.append(jnp.stack(halves))
    return (0.5 * jnp.stack(dirs)).astype(BF16)


def kernel(x, c, ctx, c_ctx, w_mod, b_mod, norm1_g, norm2_g, w_in, sgu_g, sgu_w, sgu_b, conv_w, conv_b,
           rg_wa, rg_ba, rg_wi, rg_bi, rg_lam, w_out, w_router, w1, w3, w2, final_g):
    assert x.shape == (B, S, D) and ctx.shape == (B, LC, D) and w_mod.shape[0] == 1

    cc = jnp.concatenate([c, c_ctx[None, :], jnp.zeros((B - 1, D), F32)], axis=0)
    mod = _modulation(cc, w_mod[0], b_mod[0][None, :])

    g1 = norm1_g[0][None, :]
    g2 = norm2_g[0][None, :]
    w_in_b = w_in[0].astype(BF16)
    w_in_x = w_in_b[:, 2 * DS:2 * DS + DL]
    wg = _pack_gate_weights(rg_wa[0], rg_wi[0])
    cw = conv_w[0]
    cb = conv_b[0][None, :]
    ba, bi, lam = 0.5 * rg_ba[0], 0.5 * rg_bi[0], rg_lam[0]
    sbias = jnp.repeat(sgu_b[0].T, HD, axis=1)

    h0f, h0b = _context_states(ctx, mod, g1, w_in_x, cw, cb, wg, ba, bi, lam)
    sgu, gg, cx, hf = _mix_in(x, mod, g1, w_in_b, sgu_g[0][None, :], sgu_w[0].astype(BF16), sbias, cw, cb,
                              wg, ba, bi, lam, h0f)
    x1, hx2t, logits_t = _mix_out(x, mod, cx, hf, sgu, gg, wg, ba, bi, lam, h0b, w_out[0].astype(BF16), g2,
                                  w_router[0].T)
    idx, gate = _route(logits_t)
    idx1 = idx.reshape(B * E * CAP)
    xg = _dispatch(idx1, hx2t)
    y = _moe(xg, gate, w1[0], w3[0], w2[0])
    return _combine(idx1, y, x1, mod, final_g[None, :])
```

```python
import jax
import jax.numpy as jnp
from jax import lax
from jax.experimental import pallas as pl
from jax.experimental.pallas import tpu as pltpu

F32 = jnp.float32
BF16 = jnp.bfloat16

D = 1024
B = 8
S = 2048
LC = 256
GRID_W = 64
DS = 512
NH = 4
HD = DS // NH
CHUNK = 128
DL = 512
LRU_HEADS = 8
LRU_HD = DL // LRU_HEADS
E = 16
CAP = 2 * S // E
FF = 2048
N_MOD = 6
EPS = 1e-6
RG_C = 8.0

SUBLANES = 8
LANES = 128
VMEM_LIMIT_V7X = 60000 * 1024

TP = CHUNK
ROWS = TP * B
N_TT = S // TP
SUB_PER_TOK = D // LANES
PROJ_GROUPS = 2
FN = 512
MC = 512
SCATTER_BATCH = 16
EG = 8
NG = E // EG
FIN_ROWS = 1024
PAD_F = SUBLANES
PAD_B = 2 * SUBLANES


def _params(sem, vmem_mb):
    return pltpu.CompilerParams(dimension_semantics=sem, vmem_limit_bytes=min(vmem_mb << 20, VMEM_LIMIT_V7X))


GELU_C1 = 0.7978845608028654
GELU_C2 = GELU_C1 * 0.044715


def _gelu(x):
    half = 0.5 * x
    return half + half * jnp.tanh(x * (GELU_C1 + GELU_C2 * (x * x)))


def _rms_mod(x, gs, sh):
    return (x * lax.rsqrt(jnp.mean(x * x, axis=-1, keepdims=True) + EPS)) * gs + sh


def _log_sigmoid(x):
    return -(jnp.maximum(-x, 0.0) + jnp.log1p(jnp.exp(-jnp.abs(x))))


def _conv_interleaved(xx, xi_ref, tp, period, cw, cb):
    rows = tp * B
    nslab = DL // LANES
    xi_ref[:, 0:PAD_F, :] = jnp.zeros((nslab, PAD_F, LANES), F32)
    xi_ref[:, PAD_F + rows:PAD_F + rows + PAD_B, :] = jnp.zeros((nslab, PAD_B, LANES), F32)
    for k in range(nslab):
        for b in range(B):
            xi_ref[k, pl.ds(PAD_F + b, tp, stride=B), :] = xx[b * tp:(b + 1) * tp, k * LANES:(k + 1) * LANES]
    pos = lax.broadcasted_iota(jnp.int32, (rows, LANES), 0) >> 3
    pm = pos & (period - 1)
    m0 = pm != 0
    m2 = pm != period - 1
    m3 = pm < period - 2
    outs = []
    for k in range(nslab):
        w = cw[:, k * LANES:(k + 1) * LANES]
        t0 = xi_ref[k, 0:rows, :]
        t1 = xi_ref[k, SUBLANES:SUBLANES + rows, :]
        t2 = xi_ref[k, 2 * SUBLANES:2 * SUBLANES + rows, :]
        t3 = xi_ref[k, 3 * SUBLANES:3 * SUBLANES + rows, :]
        acc = jnp.where(m0, t0, 0.0) * w[0:1] + t1 * w[1:2]
        acc = acc + jnp.where(m2, t2, 0.0) * w[2:3] + jnp.where(m3, t3, 0.0) * w[3:4]
        outs.append(acc + cb[:, k * LANES:(k + 1) * LANES])
    return jnp.concatenate(outs, axis=1)


def _rglru_coeffs(cx, wg_ref, d, ba_half, bi_half, lam):
    cxb = cx.astype(BF16)
    half = DL // 2
    a_parts, b_parts = [], []
    for k in range(2):
        sl = slice(half * k, half * (k + 1))
        zh = jnp.dot(cxb[:, sl], wg_ref[d, k], preferred_element_type=F32)
        t_r = jnp.tanh(zh[:, :half] + ba_half[:, sl])
        t_i = jnp.tanh(zh[:, half:] + bi_half[:, sl])
        ch = (0.5 * RG_C) * -_log_sigmoid(lam[:, sl])
        neg_log_a = ch + ch * t_r
        a = jnp.exp(-neg_log_a)
        one_minus_a2 = jnp.tanh(neg_log_a) * (a * a + 1.0)
        root = jnp.where(one_minus_a2 > 0.0, one_minus_a2 * lax.rsqrt(one_minus_a2), 0.0)
        xh = 0.5 * cx[:, sl]
        a_parts.append(a)
        b_parts.append(root * (xh + xh * t_i))
    return jnp.concatenate(a_parts, axis=1), jnp.concatenate(b_parts, axis=1)


def _scan(a, b, h_out_ref, h0, nsteps, reverse):
    h = h0
    for p in (range(nsteps - 1, -1, -1) if reverse else range(nsteps)):
        rs = slice(p * SUBLANES, (p + 1) * SUBLANES)
        h = a[rs, :] * h + b[rs, :]
        if h_out_ref is not None:
            h_out_ref[rs, :] = h
    return h


def _mod_kernel(c_ref, w_ref, b_ref, o_ref):
    c = c_ref[...]
    s = c * jax.nn.sigmoid(c)
    s_hi = s.astype(BF16)
    s_lo = (s - s_hi.astype(F32)).astype(BF16)
    w = w_ref[...]
    w_hi = w.astype(BF16)
    w_lo = (w - w_hi.astype(F32)).astype(BF16)
    nrow = s.shape[0]
    both = jnp.dot(jnp.concatenate([s_hi, s_lo], axis=0), w_hi, preferred_element_type=F32)
    o_ref[...] = (both[0:nrow] + (both[nrow:] + jnp.dot(s_hi, w_lo, preferred_element_type=F32))) + b_ref[...]


def _modulation(cc, w_mod, b_mod):
    nt = 4
    tn = N_MOD * D // nt
    return pl.pallas_call(
        _mod_kernel,
        grid=(nt,),
        in_specs=[pl.BlockSpec((2 * B, D), lambda i: (0, 0)),
                  pl.BlockSpec((D, tn), lambda i: (0, i)),
                  pl.BlockSpec((1, tn), lambda i: (0, i))],
        out_specs=pl.BlockSpec((2 * B, tn), lambda i: (0, i)),
        out_shape=jax.ShapeDtypeStruct((2 * B, N_MOD * D), F32),
        compiler_params=_params(("arbitrary",), 32),
        name="mod",
    )(cc, w_mod, b_mod)


def _ctx_kernel(ctx_ref, mod_ref, g1_ref, win_ref, cw_ref, cb_ref, wg_ref, ba_ref, bi_ref, lam_ref,
                hf_ref, hb_ref, xi_ref):
    sh = mod_ref[B:B + 1, 0:D]
    gs = g1_ref[...] * (1.0 + mod_ref[B:B + 1, D:2 * D])
    parts = []
    for b in range(B):
        hc = _rms_mod(ctx_ref[b], gs, sh).astype(BF16)
        parts.append(jnp.dot(hc, win_ref[...], preferred_element_type=F32))
    xx = jnp.concatenate(parts, axis=0)
    cx = _conv_interleaved(xx, xi_ref, LC, LC, cw_ref[...], cb_ref[...])
    for d in range(2):
        a, dr = _rglru_coeffs(cx, wg_ref, d, ba_ref[d:d + 1, :], bi_ref[d:d + 1, :], lam_ref[d:d + 1, :])
        h = _scan(a, dr, None, jnp.zeros((B, DL), F32), LC, reverse=(d == 1))
        if d == 0:
            hf_ref[...] = h
        else:
            hb_ref[...] = h


def _context_states(ctx, mod, g1, w_in_x, cw, cb, wg, ba, bi, lam):
    rows = LC * B
    full = lambda shape: pl.BlockSpec(shape, lambda i: (0,) * len(shape))
    return pl.pallas_call(
        _ctx_kernel,
        grid=(1,),
        in_specs=[full((B, LC, D)), full((2 * B, N_MOD * D)), full((1, D)), full((D, DL)), full((4, DL)),
                  full((1, DL)), full((2, 2, DL // 2, DL)), full((2, DL)), full((2, DL)), full((2, DL))],
        out_specs=[full((B, DL)), full((B, DL))],
        out_shape=[jax.ShapeDtypeStruct((B, DL), F32)] * 2,
        scratch_shapes=[pltpu.VMEM((DL // LANES, PAD_F + rows + PAD_B, LANES), F32)],
        compiler_params=_params(("arbitrary",), 56),
        name="ctx",
    )(ctx, mod, g1, w_in_x, cw, cb, wg, ba, bi, lam)


def _mix_in_kernel(x_ref, mod_ref, g1_ref, win_ref, sg_ref, sw_ref, sbias_ref, cw_ref, cb_ref, wg_ref,
                   ba_ref, bi_ref, lam_ref, h0_ref,
                   sgu_ref, gg_ref, cx_ref, hf_ref,
                   hbuf, xi_ref, carry):
    @pl.when(pl.program_id(0) == 0)
    def _():
        carry[...] = h0_ref[...]

    g1 = g1_ref[...]
    for b in range(B):
        sh = mod_ref[b:b + 1, 0:D]
        gs = g1 * (1.0 + mod_ref[b:b + 1, D:2 * D])
        hbuf[b * TP:(b + 1) * TP, :] = _rms_mod(x_ref[b], gs, sh).astype(BF16)
    z = jnp.dot(hbuf[...], win_ref[...], preferred_element_type=F32)
    ug = _gelu(z[:, 0:DS])
    vg = _gelu(z[:, DS:2 * DS])
    xx = z[:, 2 * DS:2 * DS + DL]
    gg = _gelu(z[:, 2 * DS + DL:])
    for b in range(B):
        gg_ref[b] = gg[b * TP:(b + 1) * TP, :].astype(BF16)

    for h in range(NH):
        hs = slice(h * HD, (h + 1) * HD)
        vh = vg[:, hs]
        vn = (vh * lax.rsqrt(jnp.mean(vh * vh, axis=-1, keepdims=True) + EPS)) * sg_ref[:, hs]
        vnb = vn.astype(BF16)
        for b in range(B):
            rs = slice(b * TP, (b + 1) * TP)
            s = jnp.dot(sw_ref[h], vnb[rs], preferred_element_type=F32) + sbias_ref[:, hs]
            sgu_ref[b, :, hs] = (ug[rs, hs] * s).astype(BF16)

    cx = _conv_interleaved(xx, xi_ref, TP, GRID_W, cw_ref[...], cb_ref[...])
    cx_ref[...] = cx
    a, dr = _rglru_coeffs(cx, wg_ref, 0, ba_ref[0:1, :], bi_ref[0:1, :], lam_ref[0:1, :])
    carry[...] = _scan(a, dr, hf_ref, carry[...], TP, reverse=False)


def _mix_in(x, mod, g1, w_in_b, sg, sw_b, sbias, cw, cb, wg, ba, bi, lam, h0f):
    full = lambda shape: pl.BlockSpec(shape, lambda i: (0,) * len(shape))
    return pl.pallas_call(
        _mix_in_kernel,
        grid=(N_TT,),
        in_specs=[pl.BlockSpec((B, TP, D), lambda i: (0, i, 0)),
                  full((2 * B, N_MOD * D)), full((1, D)), full((D, 2 * DS + 2 * DL)), full((1, DS)),
                  full((NH, CHUNK, CHUNK)), full((CHUNK, DS)), full((4, DL)), full((1, DL)),
                  full((2, 2, DL // 2, DL)), full((2, DL)), full((2, DL)), full((2, DL)), full((B, DL))],
        out_specs=[pl.BlockSpec((B, TP, DS), lambda i: (0, i, 0)),
                   pl.BlockSpec((B, TP, DL), lambda i: (0, i, 0)),
                   pl.BlockSpec((ROWS, DL), lambda i: (i, 0)),
                   pl.BlockSpec((ROWS, DL), lambda i: (i, 0))],
        out_shape=[jax.ShapeDtypeStruct((B, S, DS), BF16), jax.ShapeDtypeStruct((B, S, DL), BF16),
                   jax.ShapeDtypeStruct((S * B, DL), F32), jax.ShapeDtypeStruct((S * B, DL), F32)],
        scratch_shapes=[pltpu.VMEM((ROWS, D), BF16),
                        pltpu.VMEM((DL // LANES, PAD_F + ROWS + PAD_B, LANES), F32),
                        pltpu.VMEM((B, DL), F32)],
        compiler_params=_params(("arbitrary",), 56),
        name="mix_in",
    )(x, mod, g1, w_in_b, sg, sw_b, sbias, cw, cb, wg, ba, bi, lam, h0f)


def _mix_out_kernel(x_ref, mod_ref, cx_ref, hf_ref, sgu_ref, gg_ref, wg_ref, ba_ref, bi_ref, lam_ref, h0_ref,
                    wout_ref, g2_ref, wr_ref,
                    x1_ref, hx2t_ref, lg_ref,
                    hs_ref, carry):
    @pl.when(pl.program_id(0) == 0)
    def _():
        carry[...] = h0_ref[...]

    a, dr = _rglru_coeffs(cx_ref[...], wg_ref, 1, ba_ref[1:2, :], bi_ref[1:2, :], lam_ref[1:2, :])
    nslab = DL // LANES
    h = carry[...]
    for p in range(TP - 1, -1, -1):
        rs = slice(p * SUBLANES, (p + 1) * SUBLANES)
        h = a[rs, :] * h + dr[rs, :]
        hsum = hf_ref[rs, :] + h
        for k in range(nslab):
            hs_ref[k, rs, :] = hsum[:, k * LANES:(k + 1) * LANES]
    carry[...] = h
    rows = []
    for b in range(B):
        rec = [(gg_ref[b, :, k * LANES:(k + 1) * LANES].astype(F32)
                * hs_ref[k, pl.ds(b, TP, stride=B), :]).astype(BF16) for k in range(nslab)]
        rows.append(jnp.concatenate([sgu_ref[b]] + rec, axis=1))
    gb = B // PROJ_GROUPS
    ys = [jnp.dot(jnp.concatenate(rows[g * gb:(g + 1) * gb], axis=0), wout_ref[...], preferred_element_type=F32)
          for g in range(PROJ_GROUPS)]

    g2 = g2_ref[...]
    wr = wr_ref[...]
    wr_hi = wr.astype(BF16)
    wr_lo = (wr - wr_hi.astype(F32)).astype(BF16)
    for b in range(B):
        g1x = mod_ref[b:b + 1, 2 * D:3 * D]
        sh2 = mod_ref[b:b + 1, 3 * D:4 * D]
        gs2 = g2 * (1.0 + mod_ref[b:b + 1, 4 * D:5 * D])
        x1 = x_ref[b] + g1x * ys[b // gb][(b % gb) * TP:(b % gb + 1) * TP, :]
        x1_ref[b] = x1
        hx2 = _rms_mod(x1, gs2, sh2)
        for k in range(SUB_PER_TOK):
            hx2t_ref[b, pl.ds(k, TP, stride=SUB_PER_TOK), :] = hx2[:, k * LANES:(k + 1) * LANES]
        hx_hi = hx2.astype(BF16)
        hx_lo = (hx2 - hx_hi.astype(F32)).astype(BF16)
        nt = (((1,), (1,)), ((), ()))
        lg_ref[b] = (lax.dot_general(wr_hi, hx_hi, nt, preferred_element_type=F32)
                     + (lax.dot_general(wr_hi, hx_lo, nt, preferred_element_type=F32)
                        + lax.dot_general(wr_lo, hx_hi, nt, preferred_element_type=F32)))


def _mix_out(x, mod, cx, hf, sgu, gg, wg, ba, bi, lam, h0b, w_out_b, g2, wr_t):
    full = lambda shape: pl.BlockSpec(shape, lambda i: (0,) * len(shape))
    rev = lambda i: N_TT - 1 - i
    return pl.pallas_call(
        _mix_out_kernel,
        grid=(N_TT,),
        in_specs=[pl.BlockSpec((B, TP, D), lambda i: (0, rev(i), 0)),
                  full((2 * B, N_MOD * D)),
                  pl.BlockSpec((ROWS, DL), lambda i: (rev(i), 0)),
                  pl.BlockSpec((ROWS, DL), lambda i: (rev(i), 0)),
                  pl.BlockSpec((B, TP, DS), lambda i: (0, rev(i), 0)),
                  pl.BlockSpec((B, TP, DL), lambda i: (0, rev(i), 0)),
                  full((2, 2, DL // 2, DL)), full((2, DL)), full((2, DL)), full((2, DL)), full((B, DL)),
                  full((D, D)), full((1, D)), full((E, D))],
        out_specs=[pl.BlockSpec((B, TP, D), lambda i: (0, rev(i), 0)),
                   pl.BlockSpec((B, TP * SUB_PER_TOK, LANES), lambda i: (0, rev(i), 0)),
                   pl.BlockSpec((B, E, TP), lambda i: (0, 0, rev(i)))],
        out_shape=[jax.ShapeDtypeStruct((B, S, D), F32),
                   jax.ShapeDtypeStruct((B, S * SUB_PER_TOK, LANES), F32),
                   jax.ShapeDtypeStruct((B, E, S), F32)],
        scratch_shapes=[pltpu.VMEM((DL // LANES, ROWS, LANES), F32),
                        pltpu.VMEM((B, DL), F32)],
        compiler_params=_params(("arbitrary",), 56),
        name="mix_out",
    )(x, mod, cx, hf, sgu, gg, wg, ba, bi, lam, h0b, w_out_b, g2, wr_t)


NBLK = S // LANES
BIG = 1.0e9


def _bf16_parts(a):
    hi = a.astype(BF16).astype(F32)
    r1 = a - hi
    mid = r1.astype(BF16).astype(F32)
    lo = (r1 - mid).astype(BF16).astype(F32)
    return hi, mid, lo


def _route_kernel(lg_ref, idx_ref, gate_ref, aff_s, lci_s, offb_s, ahi_s, amid_s, alo_s, offi_s):
    for b in range(B):
        l = lg_ref[b]
        ex = jnp.exp(l - jnp.max(l, axis=0, keepdims=True))
        aff_s[b * E:(b + 1) * E, :] = ex / jnp.sum(ex, axis=0, keepdims=True)
    aff = aff_s[...]
    nrow = B * E

    def bisect(_, lohi):
        lo, hi = lohi
        mid = lo + ((hi - lo + 1) >> 1)
        cnt = jnp.sum(jnp.where(aff >= pltpu.bitcast(mid, F32), 1.0, 0.0), axis=1, keepdims=True)
        ge = cnt >= float(CAP)
        return jnp.where(ge, mid, lo), jnp.where(ge, hi, mid - 1)

    lo0 = jnp.zeros((nrow, 1), jnp.int32)
    hi0 = jnp.full((nrow, 1), 0x7F800000, jnp.int32)
    thr_bits, _ = lax.fori_loop(0, 31, bisect, (lo0, hi0))
    thr = pltpu.bitcast(thr_bits, F32)
    gt = aff > thr
    eq = aff == thr
    need = float(CAP) - jnp.sum(jnp.where(gt, 1.0, 0.0), axis=1, keepdims=True)

    qi = lax.broadcasted_iota(jnp.int32, (LANES, LANES), 0)
    ti = lax.broadcasted_iota(jnp.int32, (LANES, LANES), 1)
    tri = jnp.where(qi <= ti, 1.0, 0.0).astype(BF16)
    blocks = [slice(j * LANES, (j + 1) * LANES) for j in range(NBLK)]

    ties_before = jnp.zeros((nrow, 1), F32)
    sel = []
    for sl in blocks:
        eqb = jnp.where(eq[:, sl], 1.0, 0.0)
        incl = jnp.dot(eqb.astype(BF16), tri, preferred_element_type=F32) + ties_before
        sel.append(gt[:, sl] | (eq[:, sl] & ((incl - eqb) < need)))
        ties_before = incl[:, LANES - 1:LANES]

    lane_sq = lax.broadcasted_iota(jnp.int32, (nrow, LANES), 1)
    off = jnp.zeros((nrow, 1), F32)
    offi = jnp.full((nrow, LANES), BIG, F32)
    for j, sl in enumerate(blocks):
        rows = slice(j * nrow, (j + 1) * nrow)
        lci = jnp.dot(jnp.where(sel[j], 1.0, 0.0).astype(BF16), tri, preferred_element_type=F32)
        lci_s[rows, :] = lci
        offb_s[rows, :] = jnp.broadcast_to(off, (nrow, LANES))
        ahi_s[rows, :], amid_s[rows, :], alo_s[rows, :] = _bf16_parts(aff[:, sl])
        off = off + lci[:, LANES - 1:LANES]
        offi = jnp.where(lane_sq == j, off, offi)
    offi_s[...] = offi

    slot = lax.broadcasted_iota(jnp.int32, (CAP, LANES), 0).astype(F32)
    lane = lax.broadcasted_iota(jnp.int32, (CAP, LANES), 1)
    lane_f = lane.astype(F32)
    zpad = jnp.zeros((LANES - NBLK, 5 * LANES), BF16)

    def row_body(r, carry):
        idxm, gm = carry
        take = lambda ref: ref[pl.ds(r, NBLK, stride=nrow), :]
        table = jnp.concatenate([take(lci_s), take(offb_s), take(ahi_s), take(amid_s), take(alo_s)], axis=1)
        table = jnp.concatenate([table.astype(BF16), zpad], axis=0)
        blk = jnp.sum(jnp.where(offi_s[pl.ds(r, 1), :] <= slot, 1.0, 0.0), axis=1, keepdims=True)
        pick = jnp.where(lane_f == blk, 1.0, 0.0).astype(BF16)
        res = jnp.dot(pick, table, preferred_element_type=F32)
        rank1 = (slot + 1.0) - res[:, LANES:2 * LANES]
        tokl = jnp.sum(jnp.where(res[:, 0:LANES] < rank1, 1.0, 0.0), axis=1, keepdims=True)
        affs = (res[:, 2 * LANES:3 * LANES] + res[:, 3 * LANES:4 * LANES]) + res[:, 4 * LANES:5 * LANES]
        gv = jnp.sum(jnp.where(lane_f == tokl, affs, 0.0), axis=1, keepdims=True)
        iv = blk * float(LANES) + tokl
        put = lane == r
        return jnp.where(put, iv, idxm), jnp.where(put, gv, gm)

    z = jnp.zeros((CAP, LANES), F32)
    idxm, gm = lax.fori_loop(0, nrow, row_body, (z, z), unroll=8)
    idx_ref[...] = (idxm.T * float(SUB_PER_TOK)).astype(jnp.int32)
    gate_ref[...] = gm


def _route(logits_t):
    full = lambda shape: pl.BlockSpec(shape, lambda i: (0,) * len(shape))
    return pl.pallas_call(
        _route_kernel,
        grid=(1,),
        in_specs=[full((B, E, S))],
        out_specs=[full((B * E, CAP)), full((CAP, B * E))],
        out_shape=[jax.ShapeDtypeStruct((B * E, CAP), jnp.int32), jax.ShapeDtypeStruct((CAP, B * E), F32)],
        scratch_shapes=[pltpu.VMEM((B * E, S), F32)] + [pltpu.VMEM((NBLK * B * E, LANES), F32)] * 5
                       + [pltpu.VMEM((B * E, LANES), F32)],
        compiler_params=_params(("arbitrary",), 48),
        name="route",
    )(logits_t)


def _dispatch_kernel(idx_ref, h_ref, xg_ref, xt0, xt1):
    for e in range(E):
        xt = xt0 if e % 2 == 0 else xt1
        for s in range(CAP):
            src = pl.multiple_of(idx_ref[e * CAP + s], SUB_PER_TOK)
            xt[s * SUB_PER_TOK:(s + 1) * SUB_PER_TOK, :] = h_ref[0, pl.ds(src, SUB_PER_TOK), :]
        for k in range(SUB_PER_TOK):
            xg_ref[e, :, k * LANES:(k + 1) * LANES] = xt[pl.ds(k, CAP, stride=SUB_PER_TOK), :].astype(BF16)


def _dispatch(idx1, hx2t):
    return pl.pallas_call(
        _dispatch_kernel,
        grid=(B,),
        in_specs=[pl.BlockSpec((E * CAP,), lambda b: (b,), memory_space=pltpu.SMEM),
                  pl.BlockSpec((1, S * SUB_PER_TOK, LANES), lambda b: (b, 0, 0))],
        out_specs=pl.BlockSpec((E, CAP, D), lambda b: (0, b, 0)),
        out_shape=jax.ShapeDtypeStruct((E, B * CAP, D), BF16),
        scratch_shapes=[pltpu.VMEM((CAP * SUB_PER_TOK, LANES), F32), pltpu.VMEM((CAP * SUB_PER_TOK, LANES), F32)],
        compiler_params=_params(("arbitrary",), 40),
        name="dispatch",
    )(idx1, hx2t)


def _moe_kernel(x_ref, gate_ref, w1_ref, w3_ref, w2_ref, y_ref, hid_s, w2_s):
    f = pl.program_id(1)
    w1 = w1_ref[0].astype(BF16)
    w3 = w3_ref[0].astype(BF16)
    fs = pl.multiple_of(f * FN, FN)
    w2_s[pl.ds(fs, FN), :] = w2_ref[0].astype(BF16)
    for c in range(B * CAP // MC):
        rs = slice(c * MC, (c + 1) * MC)
        x = x_ref[0, rs, :]
        h1 = jnp.dot(x, w1, preferred_element_type=F32)
        h3 = jnp.dot(x, w3, preferred_element_type=F32)
        hid_s[f, rs, :] = ((h1 * jax.nn.sigmoid(h1)) * h3).astype(BF16)

    @pl.when(f == FF // FN - 1)
    def _():
        gates = gate_ref[...]
        lane = lax.broadcasted_iota(jnp.int32, gates.shape, 1)
        e = pl.program_id(0)
        for c in range(B * CAP // MC):
            rs = slice(c * MC, (c + 1) * MC)
            hid = jnp.concatenate([hid_s[j, rs, :] for j in range(FF // FN)], axis=1)
            y = jnp.dot(hid, w2_s[...], preferred_element_type=F32)
            for bb in range(MC // CAP):
                b = c * (MC // CAP) + bb
                gcol = jnp.sum(jnp.where(lane == b * E + e, gates, 0.0), axis=1, keepdims=True)
                y_ref[0, b * CAP:(b + 1) * CAP, :] = (y[bb * CAP:(bb + 1) * CAP, :] * gcol).astype(BF16)


def _moe(xg, gate_cols, w1, w3, w2):
    return pl.pallas_call(
        _moe_kernel,
        grid=(E, FF // FN),
        in_specs=[pl.BlockSpec((1, B * CAP, D), lambda e, f: (e, 0, 0)),
                  pl.BlockSpec((CAP, B * E), lambda e, f: (0, 0)),
                  pl.BlockSpec((1, D, FN), lambda e, f: (e, 0, f)),
                  pl.BlockSpec((1, D, FN), lambda e, f: (e, 0, f)),
                  pl.BlockSpec((1, FN, D), lambda e, f: (e, f, 0))],
        out_specs=pl.BlockSpec((1, B * CAP, D), lambda e, f: (e, 0, 0)),
        out_shape=jax.ShapeDtypeStruct((E, B * CAP, D), BF16),
        scratch_shapes=[pltpu.VMEM((FF // FN, B * CAP, FN), BF16), pltpu.VMEM((FF, D), BF16)],
        compiler_params=_params(("arbitrary", "arbitrary"), 56),
        name="moe",
    )(xg, gate_cols, w1, w3, w2)


def _combine_kernel(idx_ref, y_ref, x1_ref, mod_ref, fg_ref, out_ref, acc, yt0, yt1):
    b = pl.program_id(0)
    g = pl.program_id(1)

    @pl.when(g == 0)
    def _():
        acc[...] = jnp.zeros(acc.shape, F32)

    @pl.when(g < NG)
    def _():
        for el in range(EG):
            yt = yt0 if el % 2 == 0 else yt1
            for k in range(SUB_PER_TOK):
                yt[pl.ds(k, CAP, stride=SUB_PER_TOK), :] = y_ref[el, :, k * LANES:(k + 1) * LANES].astype(F32)
            for s0 in range(0, CAP, SCATTER_BATCH):
                dsts, news = [], []
                for j in range(SCATTER_BATCH):
                    dst = pl.multiple_of(idx_ref[el * CAP + s0 + j], SUB_PER_TOK)
                    src = (s0 + j) * SUB_PER_TOK
                    dsts.append(dst)
                    news.append(acc[pl.ds(dst, SUB_PER_TOK), :] + yt[src:src + SUB_PER_TOK, :])
                for dst, new in zip(dsts, news):
                    acc[pl.ds(dst, SUB_PER_TOK), :] = new

    @pl.when(g >= NG)
    def _():
        row0 = pl.multiple_of((g - NG) * (FIN_ROWS * SUB_PER_TOK), FIN_ROWS * SUB_PER_TOK)
        g2x = mod_ref[pl.ds(b, 1), 5 * D:6 * D]
        ssq = jnp.zeros((FIN_ROWS, 1), F32)
        for k in range(SUB_PER_TOK):
            ls = slice(k * LANES, (k + 1) * LANES)
            xo = x1_ref[0, :, ls] + g2x[:, ls] * acc[pl.ds(row0 + k, FIN_ROWS, stride=SUB_PER_TOK), :]
            out_ref[0, :, ls] = xo
            ssq = ssq + jnp.sum(xo * xo, axis=1, keepdims=True)
        inv = lax.rsqrt(ssq * (1.0 / D) + EPS)
        out_ref[0] = (out_ref[0] * inv) * fg_ref[...]


def _combine(idx1, y, x1, mod, fg):
    fin = lambda b, g: (b, jnp.maximum(g - NG, 0), 0)
    grp = lambda g: jnp.minimum(g, NG - 1)
    return pl.pallas_call(
        _combine_kernel,
        grid=(B, NG + S // FIN_ROWS),
        in_specs=[pl.BlockSpec((EG * CAP,), lambda b, g: (b * NG + grp(g),), memory_space=pltpu.SMEM),
                  pl.BlockSpec((EG, CAP, D), lambda b, g: (grp(g), b, 0)),
                  pl.BlockSpec((1, FIN_ROWS, D), fin),
                  pl.BlockSpec((2 * B, N_MOD * D), lambda b, g: (0, 0)),
                  pl.BlockSpec((1, D), lambda b, g: (0, 0))],
        out_specs=pl.BlockSpec((1, FIN_ROWS, D), fin),
        out_shape=jax.ShapeDtypeStruct((B, S, D), F32),
        scratch_shapes=[pltpu.VMEM((S * SUB_PER_TOK, LANES), F32),
                        pltpu.VMEM((CAP * SUB_PER_TOK, LANES), F32), pltpu.VMEM((CAP * SUB_PER_TOK, LANES), F32)],
        compiler_params=_params(("arbitrary", "arbitrary"), 56),
        name="combine",
    )(idx1, y, x1, mod, fg)


def _pack_gate_weights(wa, wi):
    eye = jnp.eye(4, dtype=wa.dtype)

    def bdiag(w4):
        return jnp.einsum('hij,hg->higj', w4, eye).reshape(4 * LRU_HD, 4 * LRU_HD)

    dirs = []
    for d in range(2):
        halves = []
        for k in range(2):
            hs = slice(4 * k, 4 * (k + 1))
            halves.append(jnp.concatenate([bdiag(wa[d, hs]), bdiag(wi[d, hs])], axis=1))
        dirs.append(jnp.stack(halves))
    return (0.5 * jnp.stack(dirs)).astype(BF16)


def kernel(x, c, ctx, c_ctx, w_mod, b_mod, norm1_g, norm2_g, w_in, sgu_g, sgu_w, sgu_b, conv_w, conv_b,
           rg_wa, rg_ba, rg_wi, rg_bi, rg_lam, w_out, w_router, w1, w3, w2, final_g):
    assert x.shape == (B, S, D) and ctx.shape == (B, LC, D) and w_mod.shape[0] == 1

    cc = jnp.concatenate([c, c_ctx[None, :], jnp.zeros((B - 1, D), F32)], axis=0)
    mod = _modulation(cc, w_mod[0], b_mod[0][None, :])

    g1 = norm1_g[0][None, :]
    g2 = norm2_g[0][None, :]
    w_in_b = w_in[0].astype(BF16)
    w_in_x = w_in_b[:, 2 * DS:2 * DS + DL]
    wg = _pack_gate_weights(rg_wa[0], rg_wi[0])
    cw = conv_w[0]
    cb = conv_b[0][None, :]
    ba, bi, lam = 0.5 * rg_ba[0], 0.5 * rg_bi[0], rg_lam[0]
    sbias = jnp.repeat(sgu_b[0].T, HD, axis=1)

    h0f, h0b = _context_states(ctx, mod, g1, w_in_x, cw, cb, wg, ba, bi, lam)
    sgu, gg, cx, hf = _mix_in(x, mod, g1, w_in_b, sgu_g[0][None, :], sgu_w[0].astype(BF16), sbias, cw, cb,
                              wg, ba, bi, lam, h0f)
    x1, hx2t, logits_t = _mix_out(x, mod, cx, hf, sgu, gg, wg, ba, bi, lam, h0b, w_out[0].astype(BF16), g2,
                                  w_router[0].T)
    idx, gate = _route(logits_t)
    idx1 = idx.reshape(B * E * CAP)
    xg = _dispatch(idx1, hx2t)
    y = _moe(xg, gate, w1[0], w3[0], w2[0])
    return _combine(idx1, y, x1, mod, final_g[None, :])
```

```python
import jax
import jax.numpy as jnp
from jax import lax
from jax.experimental import pallas as pl
from jax.experimental.pallas import tpu as pltpu

F32 = jnp.float32
BF16 = jnp.bfloat16

D = 1024
B = 8
S = 2048
LC = 256
GRID_W = 64
DS = 512
NH = 4
HD = DS // NH
CHUNK = 128
DL = 512
LRU_HEADS = 8
LRU_HD = DL // LRU_HEADS
E = 16
CAP = 2 * S // E
FF = 2048
N_MOD = 6
EPS = 1e-6
RG_C = 8.0

SUBLANES = 8
LANES = 128
VMEM_LIMIT_V7X = 60000 * 1024

TP = CHUNK
ROWS = TP * B
N_TT = S // TP
SUB_PER_TOK = D // LANES
PROJ_GROUPS = 2
FN = 512
MC = 512
SCATTER_BATCH = 16
EG = 8
NG = E // EG
FIN_ROWS = 1024
PAD_F = SUBLANES
PAD_B = 2 * SUBLANES


def _params(sem, vmem_mb):
    return pltpu.CompilerParams(dimension_semantics=sem, vmem_limit_bytes=min(vmem_mb << 20, VMEM_LIMIT_V7X))


GELU_C1 = 0.7978845608028654
GELU_C2 = GELU_C1 * 0.044715


def _gelu(x):
    half = 0.5 * x
    return half + half * jnp.tanh(x * (GELU_C1 + GELU_C2 * (x * x)))


def _rms_mod(x, gs, sh):
    return (x * lax.rsqrt(jnp.mean(x * x, axis=-1, keepdims=True) + EPS)) * gs + sh


def _log_sigmoid(x):
    return -(jnp.maximum(-x, 0.0) + jnp.log1p(jnp.exp(-jnp.abs(x))))


def _conv_interleaved(xx, xi_ref, tp, period, cw, cb):
    rows = tp * B
    nslab = DL // LANES
    xi_ref[:, 0:PAD_F, :] = jnp.zeros((nslab, PAD_F, LANES), F32)
    xi_ref[:, PAD_F + rows:PAD_F + rows + PAD_B, :] = jnp.zeros((nslab, PAD_B, LANES), F32)
    for k in range(nslab):
        for b in range(B):
            xi_ref[k, pl.ds(PAD_F + b, tp, stride=B), :] = xx[b * tp:(b + 1) * tp, k * LANES:(k + 1) * LANES]
    pos = lax.broadcasted_iota(jnp.int32, (rows, LANES), 0) >> 3
    pm = pos & (period - 1)
    m0 = pm != 0
    m2 = pm != period - 1
    m3 = pm < period - 2
    outs = []
    for k in range(nslab):
        w = cw[:, k * LANES:(k + 1) * LANES]
        t0 = xi_ref[k, 0:rows, :]
        t1 = xi_ref[k, SUBLANES:SUBLANES + rows, :]
        t2 = xi_ref[k, 2 * SUBLANES:2 * SUBLANES + rows, :]
        t3 = xi_ref[k, 3 * SUBLANES:3 * SUBLANES + rows, :]
        acc = jnp.where(m0, t0, 0.0) * w[0:1] + t1 * w[1:2]
        acc = acc + jnp.where(m2, t2, 0.0) * w[2:3] + jnp.where(m3, t3, 0.0) * w[3:4]
        outs.append(acc + cb[:, k * LANES:(k + 1) * LANES])
    return jnp.concatenate(outs, axis=1)


def _rglru_coeffs(cx, wg_ref, d, ba_half, bi_half, lam):
    cxb = cx.astype(BF16)
    half = DL // 2
    a_parts, b_parts = [], []
    for k in range(2):
        sl = slice(half * k, half * (k + 1))
        zh = jnp.dot(cxb[:, sl], wg_ref[d, k], preferred_element_type=F32)
        t_r = jnp.tanh(zh[:, :half] + ba_half[:, sl])
        t_i = jnp.tanh(zh[:, half:] + bi_half[:, sl])
        ch = (0.5 * RG_C) * -_log_sigmoid(lam[:, sl])
        neg_log_a = ch + ch * t_r
        a = jnp.exp(-neg_log_a)
        one_minus_a2 = jnp.tanh(neg_log_a) * (a * a + 1.0)
        root = jnp.where(one_minus_a2 > 0.0, one_minus_a2 * lax.rsqrt(one_minus_a2), 0.0)
        xh = 0.5 * cx[:, sl]
        a_parts.append(a)
        b_parts.append(root * (xh + xh * t_i))
    return jnp.concatenate(a_parts, axis=1), jnp.concatenate(b_parts, axis=1)


def _scan(a, b, h_out_ref, h0, nsteps, reverse):
    h = h0
    for p in (range(nsteps - 1, -1, -1) if reverse else range(nsteps)):
        rs = slice(p * SUBLANES, (p + 1) * SUBLANES)
        h = a[rs, :] * h + b[rs, :]
        if h_out_ref is not None:
            h_out_ref[rs, :] = h
    return h


def _mod_kernel(c_ref, w_ref, b_ref, o_ref):
    c = c_ref[...]
    s = c * jax.nn.sigmoid(c)
    s_hi = s.astype(BF16)
    s_lo = (s - s_hi.astype(F32)).astype(BF16)
    w = w_ref[...]
    w_hi = w.astype(BF16)
    w_lo = (w - w_hi.astype(F32)).astype(BF16)
    nrow = s.shape[0]
    both = jnp.dot(jnp.concatenate([s_hi, s_lo], axis=0), w_hi, preferred_element_type=F32)
    o_ref[...] = (both[0:nrow] + (both[nrow:] + jnp.dot(s_hi, w_lo, preferred_element_type=F32))) + b_ref[...]


def _modulation(cc, w_mod, b_mod):
    nt = 4
    tn = N_MOD * D // nt
    return pl.pallas_call(
        _mod_kernel,
        grid=(nt,),
        in_specs=[pl.BlockSpec((2 * B, D), lambda i: (0, 0)),
                  pl.BlockSpec((D, tn), lambda i: (0, i)),
                  pl.BlockSpec((1, tn), lambda i: (0, i))],
        out_specs=pl.BlockSpec((2 * B, tn), lambda i: (0, i)),
        out_shape=jax.ShapeDtypeStruct((2 * B, N_MOD * D), F32),
        compiler_params=_params(("arbitrary",), 32),
        name="mod",
    )(cc, w_mod, b_mod)


def _ctx_kernel(ctx_ref, mod_ref, g1_ref, win_ref, cw_ref, cb_ref, wg_ref, ba_ref, bi_ref, lam_ref,
                hf_ref, hb_ref, xi_ref):
    sh = mod_ref[B:B + 1, 0:D]
    gs = g1_ref[...] * (1.0 + mod_ref[B:B + 1, D:2 * D])
    parts = []
    for b in range(B):
        hc = _rms_mod(ctx_ref[b], gs, sh).astype(BF16)
        parts.append(jnp.dot(hc, win_ref[...], preferred_element_type=F32))
    xx = jnp.concatenate(parts, axis=0)
    cx = _conv_interleaved(xx, xi_ref, LC, LC, cw_ref[...], cb_ref[...])
    for d in range(2):
        a, dr = _rglru_coeffs(cx, wg_ref, d, ba_ref[d:d + 1, :], bi_ref[d:d + 1, :], lam_ref[d:d + 1, :])
        h = _scan(a, dr, None, jnp.zeros((B, DL), F32), LC, reverse=(d == 1))
        if d == 0:
            hf_ref[...] = h
        else:
            hb_ref[...] = h


def _context_states(ctx, mod, g1, w_in_x, cw, cb, wg, ba, bi, lam):
    rows = LC * B
    full = lambda shape: pl.BlockSpec(shape, lambda i: (0,) * len(shape))
    return pl.pallas_call(
        _ctx_kernel,
        grid=(1,),
        in_specs=[full((B, LC, D)), full((2 * B, N_MOD * D)), full((1, D)), full((D, DL)), full((4, DL)),
                  full((1, DL)), full((2, 2, DL // 2, DL)), full((2, DL)), full((2, DL)), full((2, DL))],
        out_specs=[full((B, DL)), full((B, DL))],
        out_shape=[jax.ShapeDtypeStruct((B, DL), F32)] * 2,
        scratch_shapes=[pltpu.VMEM((DL // LANES, PAD_F + rows + PAD_B, LANES), F32)],
        compiler_params=_params(("arbitrary",), 56),
        name="ctx",
    )(ctx, mod, g1, w_in_x, cw, cb, wg, ba, bi, lam)


def _mix_in_kernel(x_ref, mod_ref, g1_ref, win_ref, sg_ref, sw_ref, sbias_ref, cw_ref, cb_ref, wg_ref,
                   ba_ref, bi_ref, lam_ref, h0_ref,
                   sgu_ref, gg_ref, cx_ref, hf_ref,
                   hbuf, xi_ref, hf_s, carry):
    @pl.when(pl.program_id(0) == 0)
    def _():
        carry[...] = h0_ref[...]

    g1 = g1_ref[...]
    for b in range(B):
        sh = mod_ref[b:b + 1, 0:D]
        gs = g1 * (1.0 + mod_ref[b:b + 1, D:2 * D])
        hbuf[b * TP:(b + 1) * TP, :] = _rms_mod(x_ref[b], gs, sh).astype(BF16)
    z = jnp.dot(hbuf[...], win_ref[...], preferred_element_type=F32)
    ug = _gelu(z[:, 0:DS])
    vg = _gelu(z[:, DS:2 * DS])
    xx = z[:, 2 * DS:2 * DS + DL]
    gg = _gelu(z[:, 2 * DS + DL:])
    for b in range(B):
        gg_ref[b] = gg[b * TP:(b + 1) * TP, :].astype(BF16)

    for h in range(NH):
        hs = slice(h * HD, (h + 1) * HD)
        vh = vg[:, hs]
        vn = (vh * lax.rsqrt(jnp.mean(vh * vh, axis=-1, keepdims=True) + EPS)) * sg_ref[:, hs]
        vnb = vn.astype(BF16)
        for b in range(B):
            rs = slice(b * TP, (b + 1) * TP)
            s = jnp.dot(sw_ref[h], vnb[rs], preferred_element_type=F32) + sbias_ref[:, hs]
            sgu_ref[b, :, hs] = (ug[rs, hs] * s).astype(BF16)

    cx = _conv_interleaved(xx, xi_ref, TP, GRID_W, cw_ref[...], cb_ref[...])
    cx_ref[...] = cx.astype(BF16)
    a, dr = _rglru_coeffs(cx, wg_ref, 0, ba_ref[0:1, :], bi_ref[0:1, :], lam_ref[0:1, :])
    carry[...] = _scan(a, dr, hf_s, carry[...], TP, reverse=False)
    hf_ref[...] = hf_s[...].astype(BF16)


def _mix_in(x, mod, g1, w_in_b, sg, sw_b, sbias, cw, cb, wg, ba, bi, lam, h0f):
    full = lambda shape: pl.BlockSpec(shape, lambda i: (0,) * len(shape))
    return pl.pallas_call(
        _mix_in_kernel,
        grid=(N_TT,),
        in_specs=[pl.BlockSpec((B, TP, D), lambda i: (0, i, 0)),
                  full((2 * B, N_MOD * D)), full((1, D)), full((D, 2 * DS + 2 * DL)), full((1, DS)),
                  full((NH, CHUNK, CHUNK)), full((CHUNK, DS)), full((4, DL)), full((1, DL)),
                  full((2, 2, DL // 2, DL)), full((2, DL)), full((2, DL)), full((2, DL)), full((B, DL))],
        out_specs=[pl.BlockSpec((B, TP, DS), lambda i: (0, i, 0)),
                   pl.BlockSpec((B, TP, DL), lambda i: (0, i, 0)),
                   pl.BlockSpec((ROWS, DL), lambda i: (i, 0)),
                   pl.BlockSpec((ROWS, DL), lambda i: (i, 0))],
        out_shape=[jax.ShapeDtypeStruct((B, S, DS), BF16), jax.ShapeDtypeStruct((B, S, DL), BF16),
                   jax.ShapeDtypeStruct((S * B, DL), BF16), jax.ShapeDtypeStruct((S * B, DL), BF16)],
        scratch_shapes=[pltpu.VMEM((ROWS, D), BF16),
                        pltpu.VMEM((DL // LANES, PAD_F + ROWS + PAD_B, LANES), F32),
                        pltpu.VMEM((ROWS, DL), F32),
                        pltpu.VMEM((B, DL), F32)],
        compiler_params=_params(("arbitrary",), 56),
        name="mix_in",
    )(x, mod, g1, w_in_b, sg, sw_b, sbias, cw, cb, wg, ba, bi, lam, h0f)


def _mix_out_kernel(x_ref, mod_ref, cx_ref, hf_ref, sgu_ref, gg_ref, wg_ref, ba_ref, bi_ref, lam_ref, h0_ref,
                    wout_ref, g2_ref, wr_ref,
                    x1_ref, hx2t_ref, lg_ref,
                    hs_ref, carry):
    @pl.when(pl.program_id(0) == 0)
    def _():
        carry[...] = h0_ref[...]

    a, dr = _rglru_coeffs(cx_ref[...].astype(F32), wg_ref, 1, ba_ref[1:2, :], bi_ref[1:2, :], lam_ref[1:2, :])
    hf = hf_ref[...].astype(F32)
    nslab = DL // LANES
    h = carry[...]
    for p in range(TP - 1, -1, -1):
        rs = slice(p * SUBLANES, (p + 1) * SUBLANES)
        h = a[rs, :] * h + dr[rs, :]
        hsum = hf[rs, :] + h
        for k in range(nslab):
            hs_ref[k, rs, :] = hsum[:, k * LANES:(k + 1) * LANES]
    carry[...] = h
    rows = []
    for b in range(B):
        rec = [(gg_ref[b, :, k * LANES:(k + 1) * LANES].astype(F32)
                * hs_ref[k, pl.ds(b, TP, stride=B), :]).astype(BF16) for k in range(nslab)]
        rows.append(jnp.concatenate([sgu_ref[b]] + rec, axis=1))
    gb = B // PROJ_GROUPS
    ys = [jnp.dot(jnp.concatenate(rows[g * gb:(g + 1) * gb], axis=0), wout_ref[...], preferred_element_type=F32)
          for g in range(PROJ_GROUPS)]

    g2 = g2_ref[...]
    wr = wr_ref[...]
    wr_hi = wr.astype(BF16)
    wr_lo = (wr - wr_hi.astype(F32)).astype(BF16)
    for b in range(B):
        g1x = mod_ref[b:b + 1, 2 * D:3 * D]
        sh2 = mod_ref[b:b + 1, 3 * D:4 * D]
        gs2 = g2 * (1.0 + mod_ref[b:b + 1, 4 * D:5 * D])
        x1 = x_ref[b] + g1x * ys[b // gb][(b % gb) * TP:(b % gb + 1) * TP, :]
        x1_ref[b] = x1
        hx2 = _rms_mod(x1, gs2, sh2)
        for k in range(SUB_PER_TOK):
            hx2t_ref[b, pl.ds(k, TP, stride=SUB_PER_TOK), :] = hx2[:, k * LANES:(k + 1) * LANES]
        hx_hi = hx2.astype(BF16)
        hx_lo = (hx2 - hx_hi.astype(F32)).astype(BF16)
        nt = (((1,), (1,)), ((), ()))
        lg_ref[b] = (lax.dot_general(wr_hi, hx_hi, nt, preferred_element_type=F32)
                     + (lax.dot_general(wr_hi, hx_lo, nt, preferred_element_type=F32)
                        + lax.dot_general(wr_lo, hx_hi, nt, preferred_element_type=F32)))


def _mix_out(x, mod, cx, hf, sgu, gg, wg, ba, bi, lam, h0b, w_out_b, g2, wr_t):
    full = lambda shape: pl.BlockSpec(shape, lambda i: (0,) * len(shape))
    rev = lambda i: N_TT - 1 - i
    return pl.pallas_call(
        _mix_out_kernel,
        grid=(N_TT,),
        in_specs=[pl.BlockSpec((B, TP, D), lambda i: (0, rev(i), 0)),
                  full((2 * B, N_MOD * D)),
                  pl.BlockSpec((ROWS, DL), lambda i: (rev(i), 0)),
                  pl.BlockSpec((ROWS, DL), lambda i: (rev(i), 0)),
                  pl.BlockSpec((B, TP, DS), lambda i: (0, rev(i), 0)),
                  pl.BlockSpec((B, TP, DL), lambda i: (0, rev(i), 0)),
                  full((2, 2, DL // 2, DL)), full((2, DL)), full((2, DL)), full((2, DL)), full((B, DL)),
                  full((D, D)), full((1, D)), full((E, D))],
        out_specs=[pl.BlockSpec((B, TP, D), lambda i: (0, rev(i), 0)),
                   pl.BlockSpec((B, TP * SUB_PER_TOK, LANES), lambda i: (0, rev(i), 0)),
                   pl.BlockSpec((B, E, TP), lambda i: (0, 0, rev(i)))],
        out_shape=[jax.ShapeDtypeStruct((B, S, D), F32),
                   jax.ShapeDtypeStruct((B, S * SUB_PER_TOK, LANES), F32),
                   jax.ShapeDtypeStruct((B, E, S), F32)],
        scratch_shapes=[pltpu.VMEM((DL // LANES, ROWS, LANES), F32),
                        pltpu.VMEM((B, DL), F32)],
        compiler_params=_params(("arbitrary",), 56),
        name="mix_out",
    )(x, mod, cx, hf, sgu, gg, wg, ba, bi, lam, h0b, w_out_b, g2, wr_t)


NBLK = S // LANES
BIG = 1.0e9


def _bf16_parts(a):
    hi = a.astype(BF16).astype(F32)
    r1 = a - hi
    mid = r1.astype(BF16).astype(F32)
    lo = (r1 - mid).astype(BF16).astype(F32)
    return hi, mid, lo


def _route_kernel(lg_ref, idx_ref, gate_ref, aff_s, lci_s, offb_s, ahi_s, amid_s, alo_s, offi_s):
    for b in range(B):
        l = lg_ref[b]
        ex = jnp.exp(l - jnp.max(l, axis=0, keepdims=True))
        aff_s[b * E:(b + 1) * E, :] = ex / jnp.sum(ex, axis=0, keepdims=True)
    aff = aff_s[...]
    nrow = B * E

    def bisect(_, lohi):
        lo, hi = lohi
        mid = lo + ((hi - lo + 1) >> 1)
        cnt = jnp.sum(jnp.where(aff >= pltpu.bitcast(mid, F32), 1.0, 0.0), axis=1, keepdims=True)
        ge = cnt >= float(CAP)
        return jnp.where(ge, mid, lo), jnp.where(ge, hi, mid - 1)

    lo0 = jnp.zeros((nrow, 1), jnp.int32)
    hi0 = jnp.full((nrow, 1), 0x7F800000, jnp.int32)
    thr_bits, _ = lax.fori_loop(0, 31, bisect, (lo0, hi0))
    thr = pltpu.bitcast(thr_bits, F32)
    gt = aff > thr
    eq = aff == thr
    need = float(CAP) - jnp.sum(jnp.where(gt, 1.0, 0.0), axis=1, keepdims=True)

    qi = lax.broadcasted_iota(jnp.int32, (LANES, LANES), 0)
    ti = lax.broadcasted_iota(jnp.int32, (LANES, LANES), 1)
    tri = jnp.where(qi <= ti, 1.0, 0.0).astype(BF16)
    blocks = [slice(j * LANES, (j + 1) * LANES) for j in range(NBLK)]

    ties_before = jnp.zeros((nrow, 1), F32)
    sel = []
    for sl in blocks:
        eqb = jnp.where(eq[:, sl], 1.0, 0.0)
        incl = jnp.dot(eqb.astype(BF16), tri, preferred_element_type=F32) + ties_before
        sel.append(gt[:, sl] | (eq[:, sl] & ((incl - eqb) < need)))
        ties_before = incl[:, LANES - 1:LANES]

    lane_sq = lax.broadcasted_iota(jnp.int32, (nrow, LANES), 1)
    off = jnp.zeros((nrow, 1), F32)
    offi = jnp.full((nrow, LANES), BIG, F32)
    for j, sl in enumerate(blocks):
        rows = slice(j * nrow, (j + 1) * nrow)
        lci = jnp.dot(jnp.where(sel[j], 1.0, 0.0).astype(BF16), tri, preferred_element_type=F32)
        lci_s[rows, :] = lci
        offb_s[rows, :] = jnp.broadcast_to(off, (nrow, LANES))
        ahi_s[rows, :], amid_s[rows, :], alo_s[rows, :] = _bf16_parts(aff[:, sl])
        off = off + lci[:, LANES - 1:LANES]
        offi = jnp.where(lane_sq == j, off, offi)
    offi_s[...] = offi

    slot = lax.broadcasted_iota(jnp.int32, (CAP, LANES), 0).astype(F32)
    lane = lax.broadcasted_iota(jnp.int32, (CAP, LANES), 1)
    lane_f = lane.astype(F32)
    zpad = jnp.zeros((LANES - NBLK, 5 * LANES), BF16)

    def row_body(r, carry):
        idxm, gm = carry
        take = lambda ref: ref[pl.ds(r, NBLK, stride=nrow), :]
        table = jnp.concatenate([take(lci_s), take(offb_s), take(ahi_s), take(amid_s), take(alo_s)], axis=1)
        table = jnp.concatenate([table.astype(BF16), zpad], axis=0)
        blk = jnp.sum(jnp.where(offi_s[pl.ds(r, 1), :] <= slot, 1.0, 0.0), axis=1, keepdims=True)
        pick = jnp.where(lane_f == blk, 1.0, 0.0).astype(BF16)
        res = jnp.dot(pick, table, preferred_element_type=F32)
        rank1 = (slot + 1.0) - res[:, LANES:2 * LANES]
        tokl = jnp.sum(jnp.where(res[:, 0:LANES] < rank1, 1.0, 0.0), axis=1, keepdims=True)
        affs = (res[:, 2 * LANES:3 * LANES] + res[:, 3 * LANES:4 * LANES]) + res[:, 4 * LANES:5 * LANES]
        gv = jnp.sum(jnp.where(lane_f == tokl, affs, 0.0), axis=1, keepdims=True)
        iv = blk * float(LANES) + tokl
        put = lane == r
        return jnp.where(put, iv, idxm), jnp.where(put, gv, gm)

    z = jnp.zeros((CAP, LANES), F32)
    idxm, gm = lax.fori_loop(0, nrow, row_body, (z, z), unroll=8)
    idx_ref[...] = (idxm.T * float(SUB_PER_TOK)).astype(jnp.int32)
    gate_ref[...] = gm


def _route(logits_t):
    full = lambda shape: pl.BlockSpec(shape, lambda i: (0,) * len(shape))
    return pl.pallas_call(
        _route_kernel,
        grid=(1,),
        in_specs=[full((B, E, S))],
        out_specs=[full((B * E, CAP)), full((CAP, B * E))],
        out_shape=[jax.ShapeDtypeStruct((B * E, CAP), jnp.int32), jax.ShapeDtypeStruct((CAP, B * E), F32)],
        scratch_shapes=[pltpu.VMEM((B * E, S), F32)] + [pltpu.VMEM((NBLK * B * E, LANES), F32)] * 5
                       + [pltpu.VMEM((B * E, LANES), F32)],
        compiler_params=_params(("arbitrary",), 48),
        name="route",
    )(logits_t)


def _dispatch_kernel(idx_ref, h_ref, xg_ref, xt0, xt1):
    for e in range(E):
        xt = xt0 if e % 2 == 0 else xt1
        for s in range(CAP):
            src = pl.multiple_of(idx_ref[e * CAP + s], SUB_PER_TOK)
            xt[s * SUB_PER_TOK:(s + 1) * SUB_PER_TOK, :] = h_ref[0, pl.ds(src, SUB_PER_TOK), :]
        for k in range(SUB_PER_TOK):
            xg_ref[e, :, k * LANES:(k + 1) * LANES] = xt[pl.ds(k, CAP, stride=SUB_PER_TOK), :].astype(BF16)


def _dispatch(idx1, hx2t):
    return pl.pallas_call(
        _dispatch_kernel,
        grid=(B,),
        in_specs=[pl.BlockSpec((E * CAP,), lambda b: (b,), memory_space=pltpu.SMEM),
                  pl.BlockSpec((1, S * SUB_PER_TOK, LANES), lambda b: (b, 0, 0))],
        out_specs=pl.BlockSpec((E, CAP, D), lambda b: (0, b, 0)),
        out_shape=jax.ShapeDtypeStruct((E, B * CAP, D), BF16),
        scratch_shapes=[pltpu.VMEM((CAP * SUB_PER_TOK, LANES), F32), pltpu.VMEM((CAP * SUB_PER_TOK, LANES), F32)],
        compiler_params=_params(("arbitrary",), 40),
        name="dispatch",
    )(idx1, hx2t)


def _moe_kernel(x_ref, gate_ref, w1_ref, w3_ref, w2_ref, y_ref, hid_s, w2_s):
    f = pl.program_id(1)
    w1 = w1_ref[0].astype(BF16)
    w3 = w3_ref[0].astype(BF16)
    fs = pl.multiple_of(f * FN, FN)
    w2_s[pl.ds(fs, FN), :] = w2_ref[0].astype(BF16)
    for c in range(B * CAP // MC):
        rs = slice(c * MC, (c + 1) * MC)
        x = x_ref[0, rs, :]
        h1 = jnp.dot(x, w1, preferred_element_type=F32)
        h3 = jnp.dot(x, w3, preferred_element_type=F32)
        hid_s[f, rs, :] = ((h1 * jax.nn.sigmoid(h1)) * h3).astype(BF16)

    @pl.when(f == FF // FN - 1)
    def _():
        gates = gate_ref[...]
        lane = lax.broadcasted_iota(jnp.int32, gates.shape, 1)
        e = pl.program_id(0)
        for c in range(B * CAP // MC):
            rs = slice(c * MC, (c + 1) * MC)
            hid = jnp.concatenate([hid_s[j, rs, :] for j in range(FF // FN)], axis=1)
            y = jnp.dot(hid, w2_s[...], preferred_element_type=F32)
            for bb in range(MC // CAP):
                b = c * (MC // CAP) + bb
                gcol = jnp.sum(jnp.where(lane == b * E + e, gates, 0.0), axis=1, keepdims=True)
                y_ref[0, b * CAP:(b + 1) * CAP, :] = (y[bb * CAP:(bb + 1) * CAP, :] * gcol).astype(BF16)


def _moe(xg, gate_cols, w1, w3, w2):
    return pl.pallas_call(
        _moe_kernel,
        grid=(E, FF // FN),
        in_specs=[pl.BlockSpec((1, B * CAP, D), lambda e, f: (e, 0, 0)),
                  pl.BlockSpec((CAP, B * E), lambda e, f: (0, 0)),
                  pl.BlockSpec((1, D, FN), lambda e, f: (e, 0, f)),
                  pl.BlockSpec((1, D, FN), lambda e, f: (e, 0, f)),
                  pl.BlockSpec((1, FN, D), lambda e, f: (e, f, 0))],
        out_specs=pl.BlockSpec((1, B * CAP, D), lambda e, f: (e, 0, 0)),
        out_shape=jax.ShapeDtypeStruct((E, B * CAP, D), BF16),
        scratch_shapes=[pltpu.VMEM((FF // FN, B * CAP, FN), BF16), pltpu.VMEM((FF, D), BF16)],
        compiler_params=_params(("arbitrary", "arbitrary"), 56),
        name="moe",
    )(xg, gate_cols, w1, w3, w2)


def _combine_kernel(idx_ref, y_ref, x1_ref, mod_ref, fg_ref, out_ref, acc, yt0, yt1):
    b = pl.program_id(0)
    g = pl.program_id(1)

    @pl.when(g == 0)
    def _():
        acc[...] = jnp.zeros(acc.shape, F32)

    @pl.when(g < NG)
    def _():
        for el in range(EG):
            yt = yt0 if el % 2 == 0 else yt1
            for k in range(SUB_PER_TOK):
                yt[pl.ds(k, CAP, stride=SUB_PER_TOK), :] = y_ref[el, :, k * LANES:(k + 1) * LANES].astype(F32)
            for s0 in range(0, CAP, SCATTER_BATCH):
                dsts, news = [], []
                for j in range(SCATTER_BATCH):
                    dst = pl.multiple_of(idx_ref[el * CAP + s0 + j], SUB_PER_TOK)
                    src = (s0 + j) * SUB_PER_TOK
                    dsts.append(dst)
                    news.append(acc[pl.ds(dst, SUB_PER_TOK), :] + yt[src:src + SUB_PER_TOK, :])
                for dst, new in zip(dsts, news):
                    acc[pl.ds(dst, SUB_PER_TOK), :] = new

    @pl.when(g >= NG)
    def _():
        row0 = pl.multiple_of((g - NG) * (FIN_ROWS * SUB_PER_TOK), FIN_ROWS * SUB_PER_TOK)
        g2x = mod_ref[pl.ds(b, 1), 5 * D:6 * D]
        ssq = jnp.zeros((FIN_ROWS, 1), F32)
        for k in range(SUB_PER_TOK):
            ls = slice(k * LANES, (k + 1) * LANES)
            xo = x1_ref[0, :, ls] + g2x[:, ls] * acc[pl.ds(row0 + k, FIN_ROWS, stride=SUB_PER_TOK), :]
            out_ref[0, :, ls] = xo
            ssq = ssq + jnp.sum(xo * xo, axis=1, keepdims=True)
        inv = lax.rsqrt(ssq * (1.0 / D) + EPS)
        out_ref[0] = (out_ref[0] * inv) * fg_ref[...]


def _combine(idx1, y, x1, mod, fg):
    fin = lambda b, g: (b, jnp.maximum(g - NG, 0), 0)
    grp = lambda g: jnp.minimum(g, NG - 1)
    return pl.pallas_call(
        _combine_kernel,
        grid=(B, NG + S // FIN_ROWS),
        in_specs=[pl.BlockSpec((EG * CAP,), lambda b, g: (b * NG + grp(g),), memory_space=pltpu.SMEM),
                  pl.BlockSpec((EG, CAP, D), lambda b, g: (grp(g), b, 0)),
                  pl.BlockSpec((1, FIN_ROWS, D), fin),
                  pl.BlockSpec((2 * B, N_MOD * D), lambda b, g: (0, 0)),
                  pl.BlockSpec((1, D), lambda b, g: (0, 0))],
        out_specs=pl.BlockSpec((1, FIN_ROWS, D), fin),
        out_shape=jax.ShapeDtypeStruct((B, S, D), F32),
        scratch_shapes=[pltpu.VMEM((S * SUB_PER_TOK, LANES), F32),
                        pltpu.VMEM((CAP * SUB_PER_TOK, LANES), F32), pltpu.VMEM((CAP * SUB_PER_TOK, LANES), F32)],
        compiler_params=_params(("arbitrary", "arbitrary"), 56),
        name="combine",
    )(idx1, y, x1, mod, fg)


def _pack_gate_weights(wa, wi):
    eye = jnp.eye(4, dtype=wa.dtype)

    def bdiag(w4):
        return jnp.einsum('hij,hg->higj', w4, eye).reshape(4 * LRU_HD, 4 * LRU_HD)

    dirs = []
    for d in range(2):
        halves = []
        for k in range(2):
            hs = slice(4 * k, 4 * (k + 1))
            halves.append(jnp.concatenate([bdiag(wa[d, hs]), bdiag(wi[d, hs])], axis=1))
        dirs.append(jnp.stack(halves))
    return (0.5 * jnp.stack(dirs)).astype(BF16)


def kernel(x, c, ctx, c_ctx, w_mod, b_mod, norm1_g, norm2_g, w_in, sgu_g, sgu_w, sgu_b, conv_w, conv_b,
           rg_wa, rg_ba, rg_wi, rg_bi, rg_lam, w_out, w_router, w1, w3, w2, final_g):
    assert x.shape == (B, S, D) and ctx.shape == (B, LC, D) and w_mod.shape[0] == 1

    cc = jnp.concatenate([c, c_ctx[None, :], jnp.zeros((B - 1, D), F32)], axis=0)
    mod = _modulation(cc, w_mod[0], b_mod[0][None, :])

    g1 = norm1_g[0][None, :]
    g2 = norm2_g[0][None, :]
    w_in_b = w_in[0].astype(BF16)
    w_in_x = w_in_b[:, 2 * DS:2 * DS + DL]
    wg = _pack_gate_weights(rg_wa[0], rg_wi[0])
    cw = conv_w[0]
    cb = conv_b[0][None, :]
    ba, bi, lam = 0.5 * rg_ba[0], 0.5 * rg_bi[0], rg_lam[0]
    sbias = jnp.repeat(sgu_b[0].T, HD, axis=1)

    h0f, h0b = _context_states(ctx, mod, g1, w_in_x, cw, cb, wg, ba, bi, lam)
    sgu, gg, cx, hf = _mix_in(x, mod, g1, w_in_b, sgu_g[0][None, :], sgu_w[0].astype(BF16), sbias, cw, cb,
                              wg, ba, bi, lam, h0f)
    x1, hx2t, logits_t = _mix_out(x, mod, cx, hf, sgu, gg, wg, ba, bi, lam, h0b, w_out[0].astype(BF16), g2,
                                  w_router[0].T)
    idx, gate = _route(logits_t)
    idx1 = idx.reshape(B * E * CAP)
    xg = _dispatch(idx1, hx2t)
    y = _moe(xg, gate, w1[0], w3[0], w2[0])
    return _combine(idx1, y, x1, mod, final_g[None, :])
```

```python
import jax
import jax.numpy as jnp
from jax import lax
from jax.experimental import pallas as pl
from jax.experimental.pallas import tpu as pltpu

F32 = jnp.float32
BF16 = jnp.bfloat16

D = 1024
B = 8
S = 2048
LC = 256
GRID_W = 64
DS = 512
NH = 4
HD = DS // NH
CHUNK = 128
DL = 512
LRU_HEADS = 8
LRU_HD = DL // LRU_HEADS
E = 16
CAP = 2 * S // E
FF = 2048
N_MOD = 6
EPS = 1e-6
RG_C = 8.0

SUBLANES = 8
LANES = 128
VMEM_LIMIT_V7X = 60000 * 1024

TP = CHUNK
ROWS = TP * B
N_TT = S // TP
SUB_PER_TOK = D // LANES
PROJ_GROUPS = 2
FN = 512
MC = 512
SCATTER_BATCH = 16
EG = 8
NG = E // EG
FIN_ROWS = 1024
PAD_F = SUBLANES
PAD_B = 2 * SUBLANES


def _params(sem, vmem_mb):
    return pltpu.CompilerParams(dimension_semantics=sem, vmem_limit_bytes=min(vmem_mb << 20, VMEM_LIMIT_V7X))


GELU_C1 = 0.7978845608028654
GELU_C2 = GELU_C1 * 0.044715


def _gelu(x):
    half = 0.5 * x
    return half + half * jnp.tanh(x * (GELU_C1 + GELU_C2 * (x * x)))


def _rms_mod(x, gs, sh):
    return (x * lax.rsqrt(jnp.mean(x * x, axis=-1, keepdims=True) + EPS)) * gs + sh


def _log_sigmoid(x):
    return -(jnp.maximum(-x, 0.0) + jnp.log1p(jnp.exp(-jnp.abs(x))))


def _conv_interleaved(xx, xi_ref, tp, period, cw, cb):
    rows = tp * B
    nslab = DL // LANES
    xi_ref[:, 0:PAD_F, :] = jnp.zeros((nslab, PAD_F, LANES), F32)
    xi_ref[:, PAD_F + rows:PAD_F + rows + PAD_B, :] = jnp.zeros((nslab, PAD_B, LANES), F32)
    for k in range(nslab):
        for b in range(B):
            xi_ref[k, pl.ds(PAD_F + b, tp, stride=B), :] = xx[b * tp:(b + 1) * tp, k * LANES:(k + 1) * LANES]
    pos = lax.broadcasted_iota(jnp.int32, (rows, LANES), 0) >> 3
    pm = pos & (period - 1)
    m0 = pm != 0
    m2 = pm != period - 1
    m3 = pm < period - 2
    outs = []
    for k in range(nslab):
        w = cw[:, k * LANES:(k + 1) * LANES]
        t0 = xi_ref[k, 0:rows, :]
        t1 = xi_ref[k, SUBLANES:SUBLANES + rows, :]
        t2 = xi_ref[k, 2 * SUBLANES:2 * SUBLANES + rows, :]
        t3 = xi_ref[k, 3 * SUBLANES:3 * SUBLANES + rows, :]
        acc = jnp.where(m0, t0, 0.0) * w[0:1] + t1 * w[1:2]
        acc = acc + jnp.where(m2, t2, 0.0) * w[2:3] + jnp.where(m3, t3, 0.0) * w[3:4]
        outs.append(acc + cb[:, k * LANES:(k + 1) * LANES])
    return jnp.concatenate(outs, axis=1)


def _rglru_coeffs(cx, wg_ref, d, ba_half, bi_half, lam):
    cxb = cx.astype(BF16)
    half = DL // 2
    a_parts, b_parts = [], []
    for k in range(2):
        sl = slice(half * k, half * (k + 1))
        zh = jnp.dot(cxb[:, sl], wg_ref[d, k], preferred_element_type=F32)
        t_r = jnp.tanh(zh[:, :half] + ba_half[:, sl])
        t_i = jnp.tanh(zh[:, half:] + bi_half[:, sl])
        ch = (0.5 * RG_C) * -_log_sigmoid(lam[:, sl])
        neg_log_a = ch + ch * t_r
        a = jnp.exp(-neg_log_a)
        one_minus_a2 = jnp.tanh(neg_log_a) * (a * a + 1.0)
        root = jnp.where(one_minus_a2 > 0.0, one_minus_a2 * lax.rsqrt(one_minus_a2), 0.0)
        xh = 0.5 * cx[:, sl]
        a_parts.append(a)
        b_parts.append(root * (xh + xh * t_i))
    return jnp.concatenate(a_parts, axis=1), jnp.concatenate(b_parts, axis=1)


def _scan(a, b, h_out_ref, h0, nsteps, reverse):
    h = h0
    for p in (range(nsteps - 1, -1, -1) if reverse else range(nsteps)):
        rs = slice(p * SUBLANES, (p + 1) * SUBLANES)
        h = a[rs, :] * h + b[rs, :]
        if h_out_ref is not None:
            h_out_ref[rs, :] = h
    return h


def _mod_kernel(c_ref, w_ref, b_ref, o_ref):
    c = c_ref[...]
    s = c * jax.nn.sigmoid(c)
    s_hi = s.astype(BF16)
    s_lo = (s - s_hi.astype(F32)).astype(BF16)
    w = w_ref[...]
    w_hi = w.astype(BF16)
    w_lo = (w - w_hi.astype(F32)).astype(BF16)
    nrow = s.shape[0]
    both = jnp.dot(jnp.concatenate([s_hi, s_lo], axis=0), w_hi, preferred_element_type=F32)
    o_ref[...] = (both[0:nrow] + (both[nrow:] + jnp.dot(s_hi, w_lo, preferred_element_type=F32))) + b_ref[...]


def _modulation(cc, w_mod, b_mod):
    nt = 4
    tn = N_MOD * D // nt
    return pl.pallas_call(
        _mod_kernel,
        grid=(nt,),
        in_specs=[pl.BlockSpec((2 * B, D), lambda i: (0, 0)),
                  pl.BlockSpec((D, tn), lambda i: (0, i)),
                  pl.BlockSpec((1, tn), lambda i: (0, i))],
        out_specs=pl.BlockSpec((2 * B, tn), lambda i: (0, i)),
        out_shape=jax.ShapeDtypeStruct((2 * B, N_MOD * D), F32),
        compiler_params=_params(("arbitrary",), 32),
        name="mod",
    )(cc, w_mod, b_mod)


def _ctx_kernel(ctx_ref, mod_ref, g1_ref, win_ref, cw_ref, cb_ref, wg_ref, ba_ref, bi_ref, lam_ref,
                hf_ref, hb_ref, xi_ref):
    sh = mod_ref[B:B + 1, 0:D]
    gs = g1_ref[...] * (1.0 + mod_ref[B:B + 1, D:2 * D])
    parts = []
    for b in range(B):
        hc = _rms_mod(ctx_ref[b], gs, sh).astype(BF16)
        parts.append(jnp.dot(hc, win_ref[...], preferred_element_type=F32))
    xx = jnp.concatenate(parts, axis=0)
    cx = _conv_interleaved(xx, xi_ref, LC, LC, cw_ref[...], cb_ref[...])
    for d in range(2):
        a, dr = _rglru_coeffs(cx, wg_ref, d, ba_ref[d:d + 1, :], bi_ref[d:d + 1, :], lam_ref[d:d + 1, :])
        h = _scan(a, dr, None, jnp.zeros((B, DL), F32), LC, reverse=(d == 1))
        if d == 0:
            hf_ref[...] = h
        else:
            hb_ref[...] = h


def _context_states(ctx, mod, g1, w_in_x, cw, cb, wg, ba, bi, lam):
    rows = LC * B
    full = lambda shape: pl.BlockSpec(shape, lambda i: (0,) * len(shape))
    return pl.pallas_call(
        _ctx_kernel,
        grid=(1,),
        in_specs=[full((B, LC, D)), full((2 * B, N_MOD * D)), full((1, D)), full((D, DL)), full((4, DL)),
                  full((1, DL)), full((2, 2, DL // 2, DL)), full((2, DL)), full((2, DL)), full((2, DL))],
        out_specs=[full((B, DL)), full((B, DL))],
        out_shape=[jax.ShapeDtypeStruct((B, DL), F32)] * 2,
        scratch_shapes=[pltpu.VMEM((DL // LANES, PAD_F + rows + PAD_B, LANES), F32)],
        compiler_params=_params(("arbitrary",), 56),
        name="ctx",
    )(ctx, mod, g1, w_in_x, cw, cb, wg, ba, bi, lam)


def _mix_in_kernel(x_ref, mod_ref, g1_ref, win_ref, sg_ref, sw_ref, sbias_ref, cw_ref, cb_ref, wg_ref,
                   ba_ref, bi_ref, lam_ref, h0_ref,
                   sgu_ref, gg_ref, cx_ref, hf_ref,
                   hbuf, xi_ref, carry):
    @pl.when(pl.program_id(0) == 0)
    def _():
        carry[...] = h0_ref[...]

    g1 = g1_ref[...]
    for b in range(B):
        sh = mod_ref[b:b + 1, 0:D]
        gs = g1 * (1.0 + mod_ref[b:b + 1, D:2 * D])
        hbuf[b * TP:(b + 1) * TP, :] = _rms_mod(x_ref[b], gs, sh).astype(BF16)
    z = jnp.dot(hbuf[...], win_ref[...], preferred_element_type=F32)
    ug = _gelu(z[:, 0:DS])
    vg = _gelu(z[:, DS:2 * DS])
    xx = z[:, 2 * DS:2 * DS + DL]
    gg = _gelu(z[:, 2 * DS + DL:])
    for b in range(B):
        gg_ref[b] = gg[b * TP:(b + 1) * TP, :].astype(BF16)

    for h in range(NH):
        hs = slice(h * HD, (h + 1) * HD)
        vh = vg[:, hs]
        vn = (vh * lax.rsqrt(jnp.mean(vh * vh, axis=-1, keepdims=True) + EPS)) * sg_ref[:, hs]
        vnb = vn.astype(BF16)
        for b in range(B):
            rs = slice(b * TP, (b + 1) * TP)
            s = jnp.dot(sw_ref[h], vnb[rs], preferred_element_type=F32) + sbias_ref[:, hs]
            sgu_ref[b, :, hs] = (ug[rs, hs] * s).astype(BF16)

    cx = _conv_interleaved(xx, xi_ref, TP, GRID_W, cw_ref[...], cb_ref[...])
    cx_ref[...] = cx
    a, dr = _rglru_coeffs(cx, wg_ref, 0, ba_ref[0:1, :], bi_ref[0:1, :], lam_ref[0:1, :])
    carry[...] = _scan(a, dr, hf_ref, carry[...], TP, reverse=False)


def _mix_in(x, mod, g1, w_in_b, sg, sw_b, sbias, cw, cb, wg, ba, bi, lam, h0f):
    full = lambda shape: pl.BlockSpec(shape, lambda i: (0,) * len(shape))
    return pl.pallas_call(
        _mix_in_kernel,
        grid=(N_TT,),
        in_specs=[pl.BlockSpec((B, TP, D), lambda i: (0, i, 0)),
                  full((2 * B, N_MOD * D)), full((1, D)), full((D, 2 * DS + 2 * DL)), full((1, DS)),
                  full((NH, CHUNK, CHUNK)), full((CHUNK, DS)), full((4, DL)), full((1, DL)),
                  full((2, 2, DL // 2, DL)), full((2, DL)), full((2, DL)), full((2, DL)), full((B, DL))],
        out_specs=[pl.BlockSpec((B, TP, DS), lambda i: (0, i, 0)),
                   pl.BlockSpec((B, TP, DL), lambda i: (0, i, 0)),
                   pl.BlockSpec((ROWS, DL), lambda i: (i, 0)),
                   pl.BlockSpec((ROWS, DL), lambda i: (i, 0))],
        out_shape=[jax.ShapeDtypeStruct((B, S, DS), BF16), jax.ShapeDtypeStruct((B, S, DL), BF16),
                   jax.ShapeDtypeStruct((S * B, DL), F32), jax.ShapeDtypeStruct((S * B, DL), F32)],
        scratch_shapes=[pltpu.VMEM((ROWS, D), BF16),
                        pltpu.VMEM((DL // LANES, PAD_F + ROWS + PAD_B, LANES), F32),
                        pltpu.VMEM((B, DL), F32)],
        compiler_params=_params(("arbitrary",), 56),
        name="mix_in",
    )(x, mod, g1, w_in_b, sg, sw_b, sbias, cw, cb, wg, ba, bi, lam, h0f)


def _mix_out_kernel(x_ref, mod_ref, cx_ref, hf_ref, sgu_ref, gg_ref, wg_ref, ba_ref, bi_ref, lam_ref, h0_ref,
                    wout_ref, g2_ref, wr_ref,
                    x1_ref, hx2t_ref, lg_ref,
                    hs_ref, carry):
    @pl.when(pl.program_id(0) == 0)
    def _():
        carry[...] = h0_ref[...]

    a, dr = _rglru_coeffs(cx_ref[...], wg_ref, 1, ba_ref[1:2, :], bi_ref[1:2, :], lam_ref[1:2, :])
    nslab = DL // LANES
    h = carry[...]
    for p in range(TP - 1, -1, -1):
        rs = slice(p * SUBLANES, (p + 1) * SUBLANES)
        h = a[rs, :] * h + dr[rs, :]
        hsum = hf_ref[rs, :] + h
        for k in range(nslab):
            hs_ref[k, rs, :] = hsum[:, k * LANES:(k + 1) * LANES]
    carry[...] = h
    rows = []
    for b in range(B):
        rec = [(gg_ref[b, :, k * LANES:(k + 1) * LANES].astype(F32)
                * hs_ref[k, pl.ds(b, TP, stride=B), :]).astype(BF16) for k in range(nslab)]
        rows.append(jnp.concatenate([sgu_ref[b]] + rec, axis=1))
    gb = B // PROJ_GROUPS
    ys = [jnp.dot(jnp.concatenate(rows[g * gb:(g + 1) * gb], axis=0), wout_ref[...], preferred_element_type=F32)
          for g in range(PROJ_GROUPS)]

    g2 = g2_ref[...]
    wr = wr_ref[...]
    wr_hi = wr.astype(BF16)
    wr_lo = (wr - wr_hi.astype(F32)).astype(BF16)
    for b in range(B):
        g1x = mod_ref[b:b + 1, 2 * D:3 * D]
        sh2 = mod_ref[b:b + 1, 3 * D:4 * D]
        gs2 = g2 * (1.0 + mod_ref[b:b + 1, 4 * D:5 * D])
        x1 = x_ref[b] + g1x * ys[b // gb][(b % gb) * TP:(b % gb + 1) * TP, :]
        x1_ref[b] = x1
        hx2 = _rms_mod(x1, gs2, sh2)
        for k in range(SUB_PER_TOK):
            hx2t_ref[b, pl.ds(k, TP, stride=SUB_PER_TOK), :] = hx2[:, k * LANES:(k + 1) * LANES]
        hx_hi = hx2.astype(BF16)
        hx_lo = (hx2 - hx_hi.astype(F32)).astype(BF16)
        nt = (((1,), (1,)), ((), ()))
        lg_ref[b] = (lax.dot_general(wr_hi, hx_hi, nt, preferred_element_type=F32)
                     + (lax.dot_general(wr_hi, hx_lo, nt, preferred_element_type=F32)
                        + lax.dot_general(wr_lo, hx_hi, nt, preferred_element_type=F32)))


def _mix_out(x, mod, cx, hf, sgu, gg, wg, ba, bi, lam, h0b, w_out_b, g2, wr_t):
    full = lambda shape: pl.BlockSpec(shape, lambda i: (0,) * len(shape))
    rev = lambda i: N_TT - 1 - i
    return pl.pallas_call(
        _mix_out_kernel,
        grid=(N_TT,),
        in_specs=[pl.BlockSpec((B, TP, D), lambda i: (0, rev(i), 0)),
                  full((2 * B, N_MOD * D)),
                  pl.BlockSpec((ROWS, DL), lambda i: (rev(i), 0)),
                  pl.BlockSpec((ROWS, DL), lambda i: (rev(i), 0)),
                  pl.BlockSpec((B, TP, DS), lambda i: (0, rev(i), 0)),
                  pl.BlockSpec((B, TP, DL), lambda i: (0, rev(i), 0)),
                  full((2, 2, DL // 2, DL)), full((2, DL)), full((2, DL)), full((2, DL)), full((B, DL)),
                  full((D, D)), full((1, D)), full((E, D))],
        out_specs=[pl.BlockSpec((B, TP, D), lambda i: (0, rev(i), 0)),
                   pl.BlockSpec((B, TP * SUB_PER_TOK, LANES), lambda i: (0, rev(i), 0)),
                   pl.BlockSpec((B, E, TP), lambda i: (0, 0, rev(i)))],
        out_shape=[jax.ShapeDtypeStruct((B, S, D), F32),
                   jax.ShapeDtypeStruct((B, S * SUB_PER_TOK, LANES), F32),
                   jax.ShapeDtypeStruct((B, E, S), F32)],
        scratch_shapes=[pltpu.VMEM((DL // LANES, ROWS, LANES), F32),
                        pltpu.VMEM((B, DL), F32)],
        compiler_params=_params(("arbitrary",), 56),
        name="mix_out",
    )(x, mod, cx, hf, sgu, gg, wg, ba, bi, lam, h0b, w_out_b, g2, wr_t)


NBLK = S // LANES
BIG = 1.0e9


def _bf16_parts(a):
    hi = a.astype(BF16).astype(F32)
    r1 = a - hi
    mid = r1.astype(BF16).astype(F32)
    lo = (r1 - mid).astype(BF16).astype(F32)
    return hi, mid, lo


def _route_kernel(lg_ref, idx_ref, gate_ref, aff_s, lci_s, offb_s, ahi_s, amid_s, alo_s, offi_s):
    for b in range(B):
        l = lg_ref[b]
        ex = jnp.exp(l - jnp.max(l, axis=0, keepdims=True))
        aff_s[b * E:(b + 1) * E, :] = ex / jnp.sum(ex, axis=0, keepdims=True)
    aff = aff_s[...]
    nrow = B * E

    def bisect(_, lohi):
        lo, hi = lohi
        mid = lo + ((hi - lo + 1) >> 1)
        cnt = jnp.sum(jnp.where(aff >= pltpu.bitcast(mid, F32), 1.0, 0.0), axis=1, keepdims=True)
        ge = cnt >= float(CAP)
        return jnp.where(ge, mid, lo), jnp.where(ge, hi, mid - 1)

    lo0 = jnp.zeros((nrow, 1), jnp.int32)
    hi0 = jnp.full((nrow, 1), 0x7F800000, jnp.int32)
    thr_bits, _ = lax.fori_loop(0, 31, bisect, (lo0, hi0))
    thr = pltpu.bitcast(thr_bits, F32)
    gt = aff > thr
    eq = aff == thr
    need = float(CAP) - jnp.sum(jnp.where(gt, 1.0, 0.0), axis=1, keepdims=True)

    qi = lax.broadcasted_iota(jnp.int32, (LANES, LANES), 0)
    ti = lax.broadcasted_iota(jnp.int32, (LANES, LANES), 1)
    tri = jnp.where(qi <= ti, 1.0, 0.0).astype(BF16)
    blocks = [slice(j * LANES, (j + 1) * LANES) for j in range(NBLK)]

    ties_before = jnp.zeros((nrow, 1), F32)
    sel = []
    for sl in blocks:
        eqb = jnp.where(eq[:, sl], 1.0, 0.0)
        incl = jnp.dot(eqb.astype(BF16), tri, preferred_element_type=F32) + ties_before
        sel.append(gt[:, sl] | (eq[:, sl] & ((incl - eqb) < need)))
        ties_before = incl[:, LANES - 1:LANES]

    lane_sq = lax.broadcasted_iota(jnp.int32, (nrow, LANES), 1)
    off = jnp.zeros((nrow, 1), F32)
    offi = jnp.full((nrow, LANES), BIG, F32)
    for j, sl in enumerate(blocks):
        rows = slice(j * nrow, (j + 1) * nrow)
        lci = jnp.dot(jnp.where(sel[j], 1.0, 0.0).astype(BF16), tri, preferred_element_type=F32)
        lci_s[rows, :] = lci
        offb_s[rows, :] = jnp.broadcast_to(off, (nrow, LANES))
        ahi_s[rows, :], amid_s[rows, :], alo_s[rows, :] = _bf16_parts(aff[:, sl])
        off = off + lci[:, LANES - 1:LANES]
        offi = jnp.where(lane_sq == j, off, offi)
    offi_s[...] = offi

    slot = lax.broadcasted_iota(jnp.int32, (CAP, LANES), 0).astype(F32)
    lane = lax.broadcasted_iota(jnp.int32, (CAP, LANES), 1)
    lane_f = lane.astype(F32)
    zpad = jnp.zeros((LANES - NBLK, 5 * LANES), BF16)

    def row_body(r, carry):
        idxm, gm = carry
        take = lambda ref: ref[pl.ds(r, NBLK, stride=nrow), :]
        table = jnp.concatenate([take(lci_s), take(offb_s), take(ahi_s), take(amid_s), take(alo_s)], axis=1)
        table = jnp.concatenate([table.astype(BF16), zpad], axis=0)
        blk = jnp.sum(jnp.where(offi_s[pl.ds(r, 1), :] <= slot, 1.0, 0.0), axis=1, keepdims=True)
        pick = jnp.where(lane_f == blk, 1.0, 0.0).astype(BF16)
        res = jnp.dot(pick, table, preferred_element_type=F32)
        rank1 = (slot + 1.0) - res[:, LANES:2 * LANES]
        tokl = jnp.sum(jnp.where(res[:, 0:LANES] < rank1, 1.0, 0.0), axis=1, keepdims=True)
        affs = (res[:, 2 * LANES:3 * LANES] + res[:, 3 * LANES:4 * LANES]) + res[:, 4 * LANES:5 * LANES]
        gv = jnp.sum(jnp.where(lane_f == tokl, affs, 0.0), axis=1, keepdims=True)
        iv = blk * float(LANES) + tokl
        put = lane == r
        return jnp.where(put, iv, idxm), jnp.where(put, gv, gm)

    z = jnp.zeros((CAP, LANES), F32)
    idxm, gm = lax.fori_loop(0, nrow, row_body, (z, z), unroll=8)
    idx_ref[...] = (idxm.T * float(SUB_PER_TOK)).astype(jnp.int32)
    gate_ref[...] = gm


def _route(logits_t):
    full = lambda shape: pl.BlockSpec(shape, lambda i: (0,) * len(shape))
    return pl.pallas_call(
        _route_kernel,
        grid=(1,),
        in_specs=[full((B, E, S))],
        out_specs=[full((B * E, CAP)), full((CAP, B * E))],
        out_shape=[jax.ShapeDtypeStruct((B * E, CAP), jnp.int32), jax.ShapeDtypeStruct((CAP, B * E), F32)],
        scratch_shapes=[pltpu.VMEM((B * E, S), F32)] + [pltpu.VMEM((NBLK * B * E, LANES), F32)] * 5
                       + [pltpu.VMEM((B * E, LANES), F32)],
        compiler_params=_params(("arbitrary",), 48),
        name="route",
    )(logits_t)


def _dispatch_kernel(idx_ref, h_ref, xg_ref, xt0, xt1):
    for e in range(E):
        xt = xt0 if e % 2 == 0 else xt1
        for s in range(CAP):
            src = pl.multiple_of(idx_ref[e * CAP + s], SUB_PER_TOK)
            xt[s * SUB_PER_TOK:(s + 1) * SUB_PER_TOK, :] = h_ref[0, pl.ds(src, SUB_PER_TOK), :]
        for k in range(SUB_PER_TOK):
            xg_ref[e, :, k * LANES:(k + 1) * LANES] = xt[pl.ds(k, CAP, stride=SUB_PER_TOK), :].astype(BF16)


def _dispatch(idx1, hx2t):
    return pl.pallas_call(
        _dispatch_kernel,
        grid=(B,),
        in_specs=[pl.BlockSpec((E * CAP,), lambda b: (b,), memory_space=pltpu.SMEM),
                  pl.BlockSpec((1, S * SUB_PER_TOK, LANES), lambda b: (b, 0, 0))],
        out_specs=pl.BlockSpec((E, CAP, D), lambda b: (0, b, 0)),
        out_shape=jax.ShapeDtypeStruct((E, B * CAP, D), BF16),
        scratch_shapes=[pltpu.VMEM((CAP * SUB_PER_TOK, LANES), F32), pltpu.VMEM((CAP * SUB_PER_TOK, LANES), F32)],
        compiler_params=_params(("arbitrary",), 40),
        name="dispatch",
    )(idx1, hx2t)


def _moe_kernel(x_ref, gate_ref, w1_ref, w3_ref, w2_ref, y_ref, hid_s, w2_s):
    f = pl.program_id(1)
    w1 = w1_ref[0].astype(BF16)
    w3 = w3_ref[0].astype(BF16)
    fs = pl.multiple_of(f * FN, FN)
    w2_s[pl.ds(fs, FN), :] = w2_ref[0].astype(BF16)
    for c in range(B * CAP // MC):
        rs = slice(c * MC, (c + 1) * MC)
        x = x_ref[0, rs, :]
        h1 = jnp.dot(x, w1, preferred_element_type=F32)
        h3 = jnp.dot(x, w3, preferred_element_type=F32)
        half = 0.5 * h1
        hid_s[f, rs, :] = ((half + half * jnp.tanh(half)) * h3).astype(BF16)

    @pl.when(f == FF // FN - 1)
    def _():
        gates = gate_ref[...]
        lane = lax.broadcasted_iota(jnp.int32, gates.shape, 1)
        e = pl.program_id(0)
        for c in range(B * CAP // MC):
            rs = slice(c * MC, (c + 1) * MC)
            hid = jnp.concatenate([hid_s[j, rs, :] for j in range(FF // FN)], axis=1)
            y = jnp.dot(hid, w2_s[...], preferred_element_type=F32)
            for bb in range(MC // CAP):
                b = c * (MC // CAP) + bb
                gcol = jnp.sum(jnp.where(lane == b * E + e, gates, 0.0), axis=1, keepdims=True)
                y_ref[0, b * CAP:(b + 1) * CAP, :] = (y[bb * CAP:(bb + 1) * CAP, :] * gcol).astype(BF16)


def _moe(xg, gate_cols, w1, w3, w2):
    return pl.pallas_call(
        _moe_kernel,
        grid=(E, FF // FN),
        in_specs=[pl.BlockSpec((1, B * CAP, D), lambda e, f: (e, 0, 0)),
                  pl.BlockSpec((CAP, B * E), lambda e, f: (0, 0)),
                  pl.BlockSpec((1, D, FN), lambda e, f: (e, 0, f)),
                  pl.BlockSpec((1, D, FN), lambda e, f: (e, 0, f)),
                  pl.BlockSpec((1, FN, D), lambda e, f: (e, f, 0))],
        out_specs=pl.BlockSpec((1, B * CAP, D), lambda e, f: (e, 0, 0)),
        out_shape=jax.ShapeDtypeStruct((E, B * CAP, D), BF16),
        scratch_shapes=[pltpu.VMEM((FF // FN, B * CAP, FN), BF16), pltpu.VMEM((FF, D), BF16)],
        compiler_params=_params(("arbitrary", "arbitrary"), 56),
        name="moe",
    )(xg, gate_cols, w1, w3, w2)


def _combine_kernel(idx_ref, y_ref, x1_ref, mod_ref, fg_ref, out_ref, acc, yt0, yt1):
    b = pl.program_id(0)
    g = pl.program_id(1)

    @pl.when(g == 0)
    def _():
        acc[...] = jnp.zeros(acc.shape, F32)

    @pl.when(g < NG)
    def _():
        for el in range(EG):
            yt = yt0 if el % 2 == 0 else yt1
            for k in range(SUB_PER_TOK):
                yt[pl.ds(k, CAP, stride=SUB_PER_TOK), :] = y_ref[el, :, k * LANES:(k + 1) * LANES].astype(F32)
            for s0 in range(0, CAP, SCATTER_BATCH):
                dsts, news = [], []
                for j in range(SCATTER_BATCH):
                    dst = pl.multiple_of(idx_ref[el * CAP + s0 + j], SUB_PER_TOK)
                    src = (s0 + j) * SUB_PER_TOK
                    dsts.append(dst)
                    news.append(acc[pl.ds(dst, SUB_PER_TOK), :] + yt[src:src + SUB_PER_TOK, :])
                for dst, new in zip(dsts, news):
                    acc[pl.ds(dst, SUB_PER_TOK), :] = new

    @pl.when(g >= NG)
    def _():
        row0 = pl.multiple_of((g - NG) * (FIN_ROWS * SUB_PER_TOK), FIN_ROWS * SUB_PER_TOK)
        g2x = mod_ref[pl.ds(b, 1), 5 * D:6 * D]
        ssq = jnp.zeros((FIN_ROWS, 1), F32)
        for k in range(SUB_PER_TOK):
            ls = slice(k * LANES, (k + 1) * LANES)
            xo = x1_ref[0, :, ls] + g2x[:, ls] * acc[pl.ds(row0 + k, FIN_ROWS, stride=SUB_PER_TOK), :]
            out_ref[0, :, ls] = xo
            ssq = ssq + jnp.sum(xo * xo, axis=1, keepdims=True)
        inv = lax.rsqrt(ssq * (1.0 / D) + EPS)
        out_ref[0] = (out_ref[0] * inv) * fg_ref[...]


def _combine(idx1, y, x1, mod, fg):
    fin = lambda b, g: (b, jnp.maximum(g - NG, 0), 0)
    grp = lambda g: jnp.minimum(g, NG - 1)
    return pl.pallas_call(
        _combine_kernel,
        grid=(B, NG + S // FIN_ROWS),
        in_specs=[pl.BlockSpec((EG * CAP,), lambda b, g: (b * NG + grp(g),), memory_space=pltpu.SMEM),
                  pl.BlockSpec((EG, CAP, D), lambda b, g: (grp(g), b, 0)),
                  pl.BlockSpec((1, FIN_ROWS, D), fin),
                  pl.BlockSpec((2 * B, N_MOD * D), lambda b, g: (0, 0)),
                  pl.BlockSpec((1, D), lambda b, g: (0, 0))],
        out_specs=pl.BlockSpec((1, FIN_ROWS, D), fin),
        out_shape=jax.ShapeDtypeStruct((B, S, D), F32),
        scratch_shapes=[pltpu.VMEM((S * SUB_PER_TOK, LANES), F32),
                        pltpu.VMEM((CAP * SUB_PER_TOK, LANES), F32), pltpu.VMEM((CAP * SUB_PER_TOK, LANES), F32)],
        compiler_params=_params(("arbitrary", "arbitrary"), 56),
        name="combine",
    )(idx1, y, x1, mod, fg)


def _pack_gate_weights(wa, wi):
    eye = jnp.eye(4, dtype=wa.dtype)

    def bdiag(w4):
        return jnp.einsum('hij,hg->higj', w4, eye).reshape(4 * LRU_HD, 4 * LRU_HD)

    dirs = []
    for d in range(2):
        halves = []
        for k in range(2):
            hs = slice(4 * k, 4 * (k + 1))
            halves.append(jnp.concatenate([bdiag(wa[d, hs]), bdiag(wi[d, hs])], axis=1))
        dirs.append(jnp.stack(halves))
    return (0.5 * jnp.stack(dirs)).astype(BF16)


def kernel(x, c, ctx, c_ctx, w_mod, b_mod, norm1_g, norm2_g, w_in, sgu_g, sgu_w, sgu_b, conv_w, conv_b,
           rg_wa, rg_ba, rg_wi, rg_bi, rg_lam, w_out, w_router, w1, w3, w2, final_g):
    assert x.shape == (B, S, D) and ctx.shape == (B, LC, D) and w_mod.shape[0] == 1

    cc = jnp.concatenate([c, c_ctx[None, :], jnp.zeros((B - 1, D), F32)], axis=0)
    mod = _modulation(cc, w_mod[0], b_mod[0][None, :])

    g1 = norm1_g[0][None, :]
    g2 = norm2_g[0][None, :]
    w_in_b = w_in[0].astype(BF16)
    w_in_x = w_in_b[:, 2 * DS:2 * DS + DL]
    wg = _pack_gate_weights(rg_wa[0], rg_wi[0])
    cw = conv_w[0]
    cb = conv_b[0][None, :]
    ba, bi, lam = 0.5 * rg_ba[0], 0.5 * rg_bi[0], rg_lam[0]
    sbias = jnp.repeat(sgu_b[0].T, HD, axis=1)

    h0f, h0b = _context_states(ctx, mod, g1, w_in_x, cw, cb, wg, ba, bi, lam)
    sgu, gg, cx, hf = _mix_in(x, mod, g1, w_in_b, sgu_g[0][None, :], sgu_w[0].astype(BF16), sbias, cw, cb,
                              wg, ba, bi, lam, h0f)
    x1, hx2t, logits_t = _mix_out(x, mod, cx, hf, sgu, gg, wg, ba, bi, lam, h0b, w_out[0].astype(BF16), g2,
                                  w_router[0].T)
    idx, gate = _route(logits_t)
    idx1 = idx.reshape(B * E * CAP)
    xg = _dispatch(idx1, hx2t)
    y = _moe(xg, gate, w1[0], w3[0], w2[0])
    return _combine(idx1, y, x1, mod, final_g[None, :])
```

```python
import jax
import jax.numpy as jnp
from jax import lax
from jax.experimental import pallas as pl
from jax.experimental.pallas import tpu as pltpu

F32 = jnp.float32
BF16 = jnp.bfloat16

D = 1024
B = 8
S = 2048
LC = 256
GRID_W = 64
DS = 512
NH = 4
HD = DS // NH
CHUNK = 128
DL = 512
LRU_HEADS = 8
LRU_HD = DL // LRU_HEADS
E = 16
CAP = 2 * S // E
FF = 2048
N_MOD = 6
EPS = 1e-6
RG_C = 8.0

SUBLANES = 8
LANES = 128
VMEM_LIMIT_V7X = 60000 * 1024

TP = CHUNK
ROWS = TP * B
N_TT = S // TP
SUB_PER_TOK = D // LANES
PROJ_GROUPS = 2
FN = 512
MC = 512
SCATTER_BATCH = 16
EG = 8
NG = E // EG
FIN_ROWS = 1024
PAD_F = SUBLANES
PAD_B = 2 * SUBLANES


def _params(sem, vmem_mb):
    return pltpu.CompilerParams(dimension_semantics=sem, vmem_limit_bytes=min(vmem_mb << 20, VMEM_LIMIT_V7X))


GELU_C1 = 0.7978845608028654
GELU_C2 = GELU_C1 * 0.044715


def _gelu(x):
    half = 0.5 * x
    return half + half * jnp.tanh(x * (GELU_C1 + GELU_C2 * (x * x)))


def _rms_mod(x, gs, sh):
    return (x * lax.rsqrt(jnp.mean(x * x, axis=-1, keepdims=True) + EPS)) * gs + sh


def _log_sigmoid(x):
    return -(jnp.maximum(-x, 0.0) + jnp.log1p(jnp.exp(-jnp.abs(x))))


def _conv_interleaved(xx, xi_ref, tp, period, cw, cb):
    rows = tp * B
    nslab = DL // LANES
    xi_ref[:, 0:PAD_F, :] = jnp.zeros((nslab, PAD_F, LANES), F32)
    xi_ref[:, PAD_F + rows:PAD_F + rows + PAD_B, :] = jnp.zeros((nslab, PAD_B, LANES), F32)
    for k in range(nslab):
        for b in range(B):
            xi_ref[k, pl.ds(PAD_F + b, tp, stride=B), :] = xx[b * tp:(b + 1) * tp, k * LANES:(k + 1) * LANES]
    pos = lax.broadcasted_iota(jnp.int32, (rows, LANES), 0) >> 3
    pm = pos & (period - 1)
    m0 = pm != 0
    m2 = pm != period - 1
    m3 = pm < period - 2
    outs = []
    for k in range(nslab):
        w = cw[:, k * LANES:(k + 1) * LANES]
        t0 = xi_ref[k, 0:rows, :]
        t1 = xi_ref[k, SUBLANES:SUBLANES + rows, :]
        t2 = xi_ref[k, 2 * SUBLANES:2 * SUBLANES + rows, :]
        t3 = xi_ref[k, 3 * SUBLANES:3 * SUBLANES + rows, :]
        acc = jnp.where(m0, t0, 0.0) * w[0:1] + t1 * w[1:2]
        acc = acc + jnp.where(m2, t2, 0.0) * w[2:3] + jnp.where(m3, t3, 0.0) * w[3:4]
        outs.append(acc + cb[:, k * LANES:(k + 1) * LANES])
    return jnp.concatenate(outs, axis=1)


def _rglru_coeffs(cx, wg_ref, d, ba_half, bi_half, lam):
    cxb = cx.astype(BF16)
    half = DL // 2
    a_parts, b_parts = [], []
    for k in range(2):
        sl = slice(half * k, half * (k + 1))
        zh = jnp.dot(cxb[:, sl], wg_ref[d, k], preferred_element_type=F32)
        t_r = jnp.tanh(zh[:, :half] + ba_half[:, sl])
        t_i = jnp.tanh(zh[:, half:] + bi_half[:, sl])
        ch = (0.5 * RG_C) * -_log_sigmoid(lam[:, sl])
        neg_log_a = ch + ch * t_r
        a = jnp.exp(-neg_log_a)
        one_minus_a2 = jnp.tanh(neg_log_a) * (a * a + 1.0)
        root = jnp.where(one_minus_a2 > 0.0, one_minus_a2 * lax.rsqrt(one_minus_a2), 0.0)
        xh = 0.5 * cx[:, sl]
        a_parts.append(a)
        b_parts.append(root * (xh + xh * t_i))
    return jnp.concatenate(a_parts, axis=1), jnp.concatenate(b_parts, axis=1)


def _scan(a, b, h_out_ref, h0, nsteps, reverse):
    h = h0
    for p in (range(nsteps - 1, -1, -1) if reverse else range(nsteps)):
        rs = slice(p * SUBLANES, (p + 1) * SUBLANES)
        h = a[rs, :] * h + b[rs, :]
        if h_out_ref is not None:
            h_out_ref[rs, :] = h
    return h


def _mod_kernel(c_ref, w_ref, b_ref, o_ref):
    c = c_ref[...]
    s = c * jax.nn.sigmoid(c)
    s_hi = s.astype(BF16)
    s_lo = (s - s_hi.astype(F32)).astype(BF16)
    w = w_ref[...]
    w_hi = w.astype(BF16)
    w_lo = (w - w_hi.astype(F32)).astype(BF16)
    nrow = s.shape[0]
    both = jnp.dot(jnp.concatenate([s_hi, s_lo], axis=0), w_hi, preferred_element_type=F32)
    o_ref[...] = (both[0:nrow] + (both[nrow:] + jnp.dot(s_hi, w_lo, preferred_element_type=F32))) + b_ref[...]


def _modulation(cc, w_mod, b_mod):
    nt = 4
    tn = N_MOD * D // nt
    return pl.pallas_call(
        _mod_kernel,
        grid=(nt,),
        in_specs=[pl.BlockSpec((2 * B, D), lambda i: (0, 0)),
                  pl.BlockSpec((D, tn), lambda i: (0, i)),
                  pl.BlockSpec((1, tn), lambda i: (0, i))],
        out_specs=pl.BlockSpec((2 * B, tn), lambda i: (0, i)),
        out_shape=jax.ShapeDtypeStruct((2 * B, N_MOD * D), F32),
        compiler_params=_params(("arbitrary",), 32),
        name="mod",
    )(cc, w_mod, b_mod)


def _ctx_kernel(ctx_ref, mod_ref, g1_ref, win_ref, cw_ref, cb_ref, wg_ref, ba_ref, bi_ref, lam_ref,
                hf_ref, hb_ref, xi_ref):
    sh = mod_ref[B:B + 1, 0:D]
    gs = g1_ref[...] * (1.0 + mod_ref[B:B + 1, D:2 * D])
    parts = []
    for b in range(B):
        hc = _rms_mod(ctx_ref[b], gs, sh).astype(BF16)
        parts.append(jnp.dot(hc, win_ref[...], preferred_element_type=F32))
    xx = jnp.concatenate(parts, axis=0)
    cx = _conv_interleaved(xx, xi_ref, LC, LC, cw_ref[...], cb_ref[...])
    for d in range(2):
        a, dr = _rglru_coeffs(cx, wg_ref, d, ba_ref[d:d + 1, :], bi_ref[d:d + 1, :], lam_ref[d:d + 1, :])
        h = _scan(a, dr, None, jnp.zeros((B, DL), F32), LC, reverse=(d == 1))
        if d == 0:
            hf_ref[...] = h
        else:
            hb_ref[...] = h


def _context_states(ctx, mod, g1, w_in_x, cw, cb, wg, ba, bi, lam):
    rows = LC * B
    full = lambda shape: pl.BlockSpec(shape, lambda i: (0,) * len(shape))
    return pl.pallas_call(
        _ctx_kernel,
        grid=(1,),
        in_specs=[full((B, LC, D)), full((2 * B, N_MOD * D)), full((1, D)), full((D, DL)), full((4, DL)),
                  full((1, DL)), full((2, 2, DL // 2, DL)), full((2, DL)), full((2, DL)), full((2, DL))],
        out_specs=[full((B, DL)), full((B, DL))],
        out_shape=[jax.ShapeDtypeStruct((B, DL), F32)] * 2,
        scratch_shapes=[pltpu.VMEM((DL // LANES, PAD_F + rows + PAD_B, LANES), F32)],
        compiler_params=_params(("arbitrary",), 56),
        name="ctx",
    )(ctx, mod, g1, w_in_x, cw, cb, wg, ba, bi, lam)


def _mix_in_kernel(x_ref, mod_ref, g1_ref, win_ref, sg_ref, sw_ref, sbias_ref, cw_ref, cb_ref, wg_ref,
                   ba_ref, bi_ref, lam_ref, h0_ref,
                   sgu_ref, gg_ref, cx_ref, hf_ref,
                   hbuf, xi_ref, carry):
    @pl.when(pl.program_id(0) == 0)
    def _():
        carry[...] = h0_ref[...]

    g1 = g1_ref[...]
    for b in range(B):
        sh = mod_ref[b:b + 1, 0:D]
        gs = g1 * (1.0 + mod_ref[b:b + 1, D:2 * D])
        hbuf[b * TP:(b + 1) * TP, :] = _rms_mod(x_ref[b], gs, sh).astype(BF16)
    z = jnp.dot(hbuf[...], win_ref[...], preferred_element_type=F32)
    ug = _gelu(z[:, 0:DS])
    vg = _gelu(z[:, DS:2 * DS])
    xx = z[:, 2 * DS:2 * DS + DL]
    gg = _gelu(z[:, 2 * DS + DL:])
    for b in range(B):
        gg_ref[b] = gg[b * TP:(b + 1) * TP, :].astype(BF16)

    for h in range(NH):
        hs = slice(h * HD, (h + 1) * HD)
        vh = vg[:, hs]
        vn = (vh * lax.rsqrt(jnp.mean(vh * vh, axis=-1, keepdims=True) + EPS)) * sg_ref[:, hs]
        vnb = vn.astype(BF16)
        for b in range(B):
            rs = slice(b * TP, (b + 1) * TP)
            s = jnp.dot(sw_ref[h], vnb[rs], preferred_element_type=F32) + sbias_ref[:, hs]
            sgu_ref[b, :, hs] = (ug[rs, hs] * s).astype(BF16)

    cx = _conv_interleaved(xx, xi_ref, TP, GRID_W, cw_ref[...], cb_ref[...])
    cx_ref[...] = cx
    a, dr = _rglru_coeffs(cx, wg_ref, 0, ba_ref[0:1, :], bi_ref[0:1, :], lam_ref[0:1, :])
    carry[...] = _scan(a, dr, hf_ref, carry[...], TP, reverse=False)


def _mix_in(x, mod, g1, w_in_b, sg, sw_b, sbias, cw, cb, wg, ba, bi, lam, h0f):
    full = lambda shape: pl.BlockSpec(shape, lambda i: (0,) * len(shape))
    return pl.pallas_call(
        _mix_in_kernel,
        grid=(N_TT,),
        in_specs=[pl.BlockSpec((B, TP, D), lambda i: (0, i, 0)),
                  full((2 * B, N_MOD * D)), full((1, D)), full((D, 2 * DS + 2 * DL)), full((1, DS)),
                  full((NH, CHUNK, CHUNK)), full((CHUNK, DS)), full((4, DL)), full((1, DL)),
                  full((2, 2, DL // 2, DL)), full((2, DL)), full((2, DL)), full((2, DL)), full((B, DL))],
        out_specs=[pl.BlockSpec((B, TP, DS), lambda i: (0, i, 0)),
                   pl.BlockSpec((B, TP, DL), lambda i: (0, i, 0)),
                   pl.BlockSpec((ROWS, DL), lambda i: (i, 0)),
                   pl.BlockSpec((ROWS, DL), lambda i: (i, 0))],
        out_shape=[jax.ShapeDtypeStruct((B, S, DS), BF16), jax.ShapeDtypeStruct((B, S, DL), BF16),
                   jax.ShapeDtypeStruct((S * B, DL), F32), jax.ShapeDtypeStruct((S * B, DL), F32)],
        scratch_shapes=[pltpu.VMEM((ROWS, D), BF16),
                        pltpu.VMEM((DL // LANES, PAD_F + ROWS + PAD_B, LANES), F32),
                        pltpu.VMEM((B, DL), F32)],
        compiler_params=_params(("arbitrary",), 56),
        name="mix_in",
    )(x, mod, g1, w_in_b, sg, sw_b, sbias, cw, cb, wg, ba, bi, lam, h0f)


def _mix_out_kernel(x_ref, mod_ref, cx_ref, hf_ref, sgu_ref, gg_ref, wg_ref, ba_ref, bi_ref, lam_ref, h0_ref,
                    wout_ref, g2_ref, wr_ref,
                    x1_ref, hx2t_ref, lg_ref,
                    hs_ref, carry):
    @pl.when(pl.program_id(0) == 0)
    def _():
        carry[...] = h0_ref[...]

    a, dr = _rglru_coeffs(cx_ref[...], wg_ref, 1, ba_ref[1:2, :], bi_ref[1:2, :], lam_ref[1:2, :])
    nslab = DL // LANES
    h = carry[...]
    for p in range(TP - 1, -1, -1):
        rs = slice(p * SUBLANES, (p + 1) * SUBLANES)
        h = a[rs, :] * h + dr[rs, :]
        hsum = hf_ref[rs, :] + h
        for k in range(nslab):
            hs_ref[k, rs, :] = hsum[:, k * LANES:(k + 1) * LANES]
    carry[...] = h
    rows = []
    for b in range(B):
        rec = [(gg_ref[b, :, k * LANES:(k + 1) * LANES].astype(F32)
                * hs_ref[k, pl.ds(b, TP, stride=B), :]).astype(BF16) for k in range(nslab)]
        rows.append(jnp.concatenate([sgu_ref[b]] + rec, axis=1))
    gb = B // PROJ_GROUPS
    ys = [jnp.dot(jnp.concatenate(rows[g * gb:(g + 1) * gb], axis=0), wout_ref[...], preferred_element_type=F32)
          for g in range(PROJ_GROUPS)]

    g2 = g2_ref[...]
    wr = wr_ref[...]
    wr_hi = wr.astype(BF16)
    wr_lo = (wr - wr_hi.astype(F32)).astype(BF16)
    wr_both = jnp.concatenate([wr_hi, wr_lo], axis=0)
    his, los = [], []
    for b in range(B):
        g1x = mod_ref[b:b + 1, 2 * D:3 * D]
        sh2 = mod_ref[b:b + 1, 3 * D:4 * D]
        gs2 = g2 * (1.0 + mod_ref[b:b + 1, 4 * D:5 * D])
        x1 = x_ref[b] + g1x * ys[b // gb][(b % gb) * TP:(b % gb + 1) * TP, :]
        x1_ref[b] = x1
        hx2 = _rms_mod(x1, gs2, sh2)
        for k in range(SUB_PER_TOK):
            hx2t_ref[b, pl.ds(k, TP, stride=SUB_PER_TOK), :] = hx2[:, k * LANES:(k + 1) * LANES]
        hx_hi = hx2.astype(BF16)
        his.append(hx_hi)
        los.append((hx2 - hx_hi.astype(F32)).astype(BF16))

    nt = (((1,), (1,)), ((), ()))
    p = lax.dot_general(wr_both, jnp.concatenate(his, axis=0), nt, preferred_element_type=F32)
    q = lax.dot_general(wr_hi, jnp.concatenate(los, axis=0), nt, preferred_element_type=F32)
    lg = p[0:E] + (q + p[E:2 * E])
    for b in range(B):
        lg_ref[b] = lg[:, b * TP:(b + 1) * TP]


def _mix_out(x, mod, cx, hf, sgu, gg, wg, ba, bi, lam, h0b, w_out_b, g2, wr_t):
    full = lambda shape: pl.BlockSpec(shape, lambda i: (0,) * len(shape))
    rev = lambda i: N_TT - 1 - i
    return pl.pallas_call(
        _mix_out_kernel,
        grid=(N_TT,),
        in_specs=[pl.BlockSpec((B, TP, D), lambda i: (0, rev(i), 0)),
                  full((2 * B, N_MOD * D)),
                  pl.BlockSpec((ROWS, DL), lambda i: (rev(i), 0)),
                  pl.BlockSpec((ROWS, DL), lambda i: (rev(i), 0)),
                  pl.BlockSpec((B, TP, DS), lambda i: (0, rev(i), 0)),
                  pl.BlockSpec((B, TP, DL), lambda i: (0, rev(i), 0)),
                  full((2, 2, DL // 2, DL)), full((2, DL)), full((2, DL)), full((2, DL)), full((B, DL)),
                  full((D, D)), full((1, D)), full((E, D))],
        out_specs=[pl.BlockSpec((B, TP, D), lambda i: (0, rev(i), 0)),
                   pl.BlockSpec((B, TP * SUB_PER_TOK, LANES), lambda i: (0, rev(i), 0)),
                   pl.BlockSpec((B, E, TP), lambda i: (0, 0, rev(i)))],
        out_shape=[jax.ShapeDtypeStruct((B, S, D), F32),
                   jax.ShapeDtypeStruct((B, S * SUB_PER_TOK, LANES), F32),
                   jax.ShapeDtypeStruct((B, E, S), F32)],
        scratch_shapes=[pltpu.VMEM((DL // LANES, ROWS, LANES), F32),
                        pltpu.VMEM((B, DL), F32)],
        compiler_params=_params(("arbitrary",), 56),
        name="mix_out",
    )(x, mod, cx, hf, sgu, gg, wg, ba, bi, lam, h0b, w_out_b, g2, wr_t)


NBLK = S // LANES
BIG = 1.0e9


def _bf16_parts(a):
    hi = a.astype(BF16).astype(F32)
    r1 = a - hi
    mid = r1.astype(BF16).astype(F32)
    lo = (r1 - mid).astype(BF16).astype(F32)
    return hi, mid, lo


def _route_kernel(lg_ref, idx_ref, gate_ref, aff_s, lci_s, offb_s, ahi_s, amid_s, alo_s, offi_s):
    for b in range(B):
        l = lg_ref[b]
        ex = jnp.exp(l - jnp.max(l, axis=0, keepdims=True))
        aff_s[b * E:(b + 1) * E, :] = ex / jnp.sum(ex, axis=0, keepdims=True)
    aff = aff_s[...]
    nrow = B * E

    def bisect(_, lohi):
        lo, hi = lohi
        mid = lo + ((hi - lo + 1) >> 1)
        cnt = jnp.sum(jnp.where(aff >= pltpu.bitcast(mid, F32), 1.0, 0.0), axis=1, keepdims=True)
        ge = cnt >= float(CAP)
        return jnp.where(ge, mid, lo), jnp.where(ge, hi, mid - 1)

    lo0 = jnp.zeros((nrow, 1), jnp.int32)
    hi0 = jnp.full((nrow, 1), 0x7F800000, jnp.int32)
    thr_bits, _ = lax.fori_loop(0, 31, bisect, (lo0, hi0))
    thr = pltpu.bitcast(thr_bits, F32)
    gt = aff > thr
    eq = aff == thr
    need = float(CAP) - jnp.sum(jnp.where(gt, 1.0, 0.0), axis=1, keepdims=True)

    qi = lax.broadcasted_iota(jnp.int32, (LANES, LANES), 0)
    ti = lax.broadcasted_iota(jnp.int32, (LANES, LANES), 1)
    tri = jnp.where(qi <= ti, 1.0, 0.0).astype(BF16)
    blocks = [slice(j * LANES, (j + 1) * LANES) for j in range(NBLK)]

    ties_before = jnp.zeros((nrow, 1), F32)
    sel = []
    for sl in blocks:
        eqb = jnp.where(eq[:, sl], 1.0, 0.0)
        incl = jnp.dot(eqb.astype(BF16), tri, preferred_element_type=F32) + ties_before
        sel.append(gt[:, sl] | (eq[:, sl] & ((incl - eqb) < need)))
        ties_before = incl[:, LANES - 1:LANES]

    lane_sq = lax.broadcasted_iota(jnp.int32, (nrow, LANES), 1)
    off = jnp.zeros((nrow, 1), F32)
    offi = jnp.full((nrow, LANES), BIG, F32)
    for j, sl in enumerate(blocks):
        rows = slice(j * nrow, (j + 1) * nrow)
        lci = jnp.dot(jnp.where(sel[j], 1.0, 0.0).astype(BF16), tri, preferred_element_type=F32)
        lci_s[rows, :] = lci
        offb_s[rows, :] = jnp.broadcast_to(off, (nrow, LANES))
        ahi_s[rows, :], amid_s[rows, :], alo_s[rows, :] = _bf16_parts(aff[:, sl])
        off = off + lci[:, LANES - 1:LANES]
        offi = jnp.where(lane_sq == j, off, offi)
    offi_s[...] = offi

    slot = lax.broadcasted_iota(jnp.int32, (CAP, LANES), 0).astype(F32)
    lane = lax.broadcasted_iota(jnp.int32, (CAP, LANES), 1)
    lane_f = lane.astype(F32)
    zpad = jnp.zeros((LANES - NBLK, 5 * LANES), BF16)

    def row_body(r, carry):
        idxm, gm = carry
        take = lambda ref: ref[pl.ds(r, NBLK, stride=nrow), :]
        table = jnp.concatenate([take(lci_s), take(offb_s), take(ahi_s), take(amid_s), take(alo_s)], axis=1)
        table = jnp.concatenate([table.astype(BF16), zpad], axis=0)
        blk = jnp.sum(jnp.where(offi_s[pl.ds(r, 1), :] <= slot, 1.0, 0.0), axis=1, keepdims=True)
        pick = jnp.where(lane_f == blk, 1.0, 0.0).astype(BF16)
        res = jnp.dot(pick, table, preferred_element_type=F32)
        rank1 = (slot + 1.0) - res[:, LANES:2 * LANES]
        tokl = jnp.sum(jnp.where(res[:, 0:LANES] < rank1, 1.0, 0.0), axis=1, keepdims=True)
        affs = (res[:, 2 * LANES:3 * LANES] + res[:, 3 * LANES:4 * LANES]) + res[:, 4 * LANES:5 * LANES]
        gv = jnp.sum(jnp.where(lane_f == tokl, affs, 0.0), axis=1, keepdims=True)
        iv = blk * float(LANES) + tokl
        put = lane == r
        return jnp.where(put, iv, idxm), jnp.where(put, gv, gm)

    z = jnp.zeros((CAP, LANES), F32)
    idxm, gm = lax.fori_loop(0, nrow, row_body, (z, z), unroll=8)
    idx_ref[...] = (idxm.T * float(SUB_PER_TOK)).astype(jnp.int32)
    gate_ref[...] = gm


def _route(logits_t):
    full = lambda shape: pl.BlockSpec(shape, lambda i: (0,) * len(shape))
    return pl.pallas_call(
        _route_kernel,
        grid=(1,),
        in_specs=[full((B, E, S))],
        out_specs=[full((B * E, CAP)), full((CAP, B * E))],
        out_shape=[jax.ShapeDtypeStruct((B * E, CAP), jnp.int32), jax.ShapeDtypeStruct((CAP, B * E), F32)],
        scratch_shapes=[pltpu.VMEM((B * E, S), F32)] + [pltpu.VMEM((NBLK * B * E, LANES), F32)] * 5
                       + [pltpu.VMEM((B * E, LANES), F32)],
        compiler_params=_params(("arbitrary",), 48),
        name="route",
    )(logits_t)


def _dispatch_kernel(idx_ref, h_ref, xg_ref, xt0, xt1):
    for e in range(E):
        xt = xt0 if e % 2 == 0 else xt1
        for s in range(CAP):
            src = pl.multiple_of(idx_ref[e * CAP + s], SUB_PER_TOK)
            xt[s * SUB_PER_TOK:(s + 1) * SUB_PER_TOK, :] = h_ref[0, pl.ds(src, SUB_PER_TOK), :]
        for k in range(SUB_PER_TOK):
            xg_ref[e, :, k * LANES:(k + 1) * LANES] = xt[pl.ds(k, CAP, stride=SUB_PER_TOK), :].astype(BF16)


def _dispatch(idx1, hx2t):
    return pl.pallas_call(
        _dispatch_kernel,
        grid=(B,),
        in_specs=[pl.BlockSpec((E * CAP,), lambda b: (b,), memory_space=pltpu.SMEM),
                  pl.BlockSpec((1, S * SUB_PER_TOK, LANES), lambda b: (b, 0, 0))],
        out_specs=pl.BlockSpec((E, CAP, D), lambda b: (0, b, 0)),
        out_shape=jax.ShapeDtypeStruct((E, B * CAP, D), BF16),
        scratch_shapes=[pltpu.VMEM((CAP * SUB_PER_TOK, LANES), F32), pltpu.VMEM((CAP * SUB_PER_TOK, LANES), F32)],
        compiler_params=_params(("arbitrary",), 40),
        name="dispatch",
    )(idx1, hx2t)


def _moe_kernel(x_ref, gate_ref, w1_ref, w3_ref, w2_ref, y_ref, hid_s, w2_s):
    f = pl.program_id(1)
    w1 = w1_ref[0].astype(BF16)
    w3 = w3_ref[0].astype(BF16)
    fs = pl.multiple_of(f * FN, FN)
    w2_s[pl.ds(fs, FN), :] = w2_ref[0].astype(BF16)
    for c in range(B * CAP // MC):
        rs = slice(c * MC, (c + 1) * MC)
        x = x_ref[0, rs, :]
        h1 = jnp.dot(x, w1, preferred_element_type=F32)
        h3 = jnp.dot(x, w3, preferred_element_type=F32)
        half = 0.5 * h1
        hid_s[f, rs, :] = ((half + half * jnp.tanh(half)) * h3).astype(BF16)

    @pl.when(f == FF // FN - 1)
    def _():
        gates = gate_ref[...]
        lane = lax.broadcasted_iota(jnp.int32, gates.shape, 1)
        e = pl.program_id(0)
        for c in range(B * CAP // MC):
            rs = slice(c * MC, (c + 1) * MC)
            hid = jnp.concatenate([hid_s[j, rs, :] for j in range(FF // FN)], axis=1)
            y = jnp.dot(hid, w2_s[...], preferred_element_type=F32)
            for bb in range(MC // CAP):
                b = c * (MC // CAP) + bb
                gcol = jnp.sum(jnp.where(lane == b * E + e, gates, 0.0), axis=1, keepdims=True)
                y_ref[0, b * CAP:(b + 1) * CAP, :] = (y[bb * CAP:(bb + 1) * CAP, :] * gcol).astype(BF16)


def _moe(xg, gate_cols, w1, w3, w2):
    return pl.pallas_call(
        _moe_kernel,
        grid=(E, FF // FN),
        in_specs=[pl.BlockSpec((1, B * CAP, D), lambda e, f: (e, 0, 0)),
                  pl.BlockSpec((CAP, B * E), lambda e, f: (0, 0)),
                  pl.BlockSpec((1, D, FN), lambda e, f: (e, 0, f)),
                  pl.BlockSpec((1, D, FN), lambda e, f: (e, 0, f)),
                  pl.BlockSpec((1, FN, D), lambda e, f: (e, f, 0))],
        out_specs=pl.BlockSpec((1, B * CAP, D), lambda e, f: (e, 0, 0)),
        out_shape=jax.ShapeDtypeStruct((E, B * CAP, D), BF16),
        scratch_shapes=[pltpu.VMEM((FF // FN, B * CAP, FN), BF16), pltpu.VMEM((FF, D), BF16)],
        compiler_params=_params(("arbitrary", "arbitrary"), 56),
        name="moe",
    )(xg, gate_cols, w1, w3, w2)


def _combine_kernel(idx_ref, y_ref, x1_ref, mod_ref, fg_ref, out_ref, acc, yt0, yt1):
    b = pl.program_id(0)
    g = pl.program_id(1)

    @pl.when(g == 0)
    def _():
        acc[...] = jnp.zeros(acc.shape, F32)

    @pl.when(g < NG)
    def _():
        for el in range(EG):
            yt = yt0 if el % 2 == 0 else yt1
            for k in range(SUB_PER_TOK):
                yt[pl.ds(k, CAP, stride=SUB_PER_TOK), :] = y_ref[el, :, k * LANES:(k + 1) * LANES].astype(F32)
            for s0 in range(0, CAP, SCATTER_BATCH):
                dsts, news = [], []
                for j in range(SCATTER_BATCH):
                    dst = pl.multiple_of(idx_ref[el * CAP + s0 + j], SUB_PER_TOK)
                    src = (s0 + j) * SUB_PER_TOK
                    dsts.append(dst)
                    news.append(acc[pl.ds(dst, SUB_PER_TOK), :] + yt[src:src + SUB_PER_TOK, :])
                for dst, new in zip(dsts, news):
                    acc[pl.ds(dst, SUB_PER_TOK), :] = new

    @pl.when(g >= NG)
    def _():
        row0 = pl.multiple_of((g - NG) * (FIN_ROWS * SUB_PER_TOK), FIN_ROWS * SUB_PER_TOK)
        g2x = mod_ref[pl.ds(b, 1), 5 * D:6 * D]
        ssq = jnp.zeros((FIN_ROWS, 1), F32)
        for k in range(SUB_PER_TOK):
            ls = slice(k * LANES, (k + 1) * LANES)
            xo = x1_ref[0, :, ls] + g2x[:, ls] * acc[pl.ds(row0 + k, FIN_ROWS, stride=SUB_PER_TOK), :]
            out_ref[0, :, ls] = xo
            ssq = ssq + jnp.sum(xo * xo, axis=1, keepdims=True)
        inv = lax.rsqrt(ssq * (1.0 / D) + EPS)
        out_ref[0] = (out_ref[0] * inv) * fg_ref[...]


def _combine(idx1, y, x1, mod, fg):
    fin = lambda b, g: (b, jnp.maximum(g - NG, 0), 0)
    grp = lambda g: jnp.minimum(g, NG - 1)
    return pl.pallas_call(
        _combine_kernel,
        grid=(B, NG + S // FIN_ROWS),
        in_specs=[pl.BlockSpec((EG * CAP,), lambda b, g: (b * NG + grp(g),), memory_space=pltpu.SMEM),
                  pl.BlockSpec((EG, CAP, D), lambda b, g: (grp(g), b, 0)),
                  pl.BlockSpec((1, FIN_ROWS, D), fin),
                  pl.BlockSpec((2 * B, N_MOD * D), lambda b, g: (0, 0)),
                  pl.BlockSpec((1, D), lambda b, g: (0, 0))],
        out_specs=pl.BlockSpec((1, FIN_ROWS, D), fin),
        out_shape=jax.ShapeDtypeStruct((B, S, D), F32),
        scratch_shapes=[pltpu.VMEM((S * SUB_PER_TOK, LANES), F32),
                        pltpu.VMEM((CAP * SUB_PER_TOK, LANES), F32), pltpu.VMEM((CAP * SUB_PER_TOK, LANES), F32)],
        compiler_params=_params(("arbitrary", "arbitrary"), 56),
        name="combine",
    )(idx1, y, x1, mod, fg)


def _pack_gate_weights(wa, wi):
    eye = jnp.eye(4, dtype=wa.dtype)

    def bdiag(w4):
        return jnp.einsum('hij,hg->higj', w4, eye).reshape(4 * LRU_HD, 4 * LRU_HD)

    dirs = []
    for d in range(2):
        halves = []
        for k in range(2):
            hs = slice(4 * k, 4 * (k + 1))
            halves.append(jnp.concatenate([bdiag(wa[d, hs]), bdiag(wi[d, hs])], axis=1))
        dirs.append(jnp.stack(halves))
    return (0.5 * jnp.stack(dirs)).astype(BF16)


def kernel(x, c, ctx, c_ctx, w_mod, b_mod, norm1_g, norm2_g, w_in, sgu_g, sgu_w, sgu_b, conv_w, conv_b,
           rg_wa, rg_ba, rg_wi, rg_bi, rg_lam, w_out, w_router, w1, w3, w2, final_g):
    assert x.shape == (B, S, D) and ctx.shape == (B, LC, D) and w_mod.shape[0] == 1

    cc = jnp.concatenate([c, c_ctx[None, :], jnp.zeros((B - 1, D), F32)], axis=0)
    mod = _modulation(cc, w_mod[0], b_mod[0][None, :])

    g1 = norm1_g[0][None, :]
    g2 = norm2_g[0][None, :]
    w_in_b = w_in[0].astype(BF16)
    w_in_x = w_in_b[:, 2 * DS:2 * DS + DL]
    wg = _pack_gate_weights(rg_wa[0], rg_wi[0])
    cw = conv_w[0]
    cb = conv_b[0][None, :]
    ba, bi, lam = 0.5 * rg_ba[0], 0.5 * rg_bi[0], rg_lam[0]
    sbias = jnp.repeat(sgu_b[0].T, HD, axis=1)

    h0f, h0b = _context_states(ctx, mod, g1, w_in_x, cw, cb, wg, ba, bi, lam)
    sgu, gg, cx, hf = _mix_in(x, mod, g1, w_in_b, sgu_g[0][None, :], sgu_w[0].astype(BF16), sbias, cw, cb,
                              wg, ba, bi, lam, h0f)
    x1, hx2t, logits_t = _mix_out(x, mod, cx, hf, sgu, gg, wg, ba, bi, lam, h0b, w_out[0].astype(BF16), g2,
                                  w_router[0].T)
    idx, gate = _route(logits_t)
    idx1 = idx.reshape(B * E * CAP)
    xg = _dispatch(idx1, hx2t)
    y = _moe(xg, gate, w1[0], w3[0], w2[0])
    return _combine(idx1, y, x1, mod, final_g[None, :])
```

```python
import jax
import jax.numpy as jnp
from jax import lax
from jax.experimental import pallas as pl
from jax.experimental.pallas import tpu as pltpu

F32 = jnp.float32
BF16 = jnp.bfloat16

D = 1024
B = 8
S = 2048
LC = 256
GRID_W = 64
DS = 512
NH = 4
HD = DS // NH
CHUNK = 128
DL = 512
LRU_HEADS = 8
LRU_HD = DL // LRU_HEADS
E = 16
CAP = 2 * S // E
FF = 2048
N_MOD = 6
EPS = 1e-6
RG_C = 8.0

SUBLANES = 8
LANES = 128
VMEM_LIMIT_V7X = 60000 * 1024

TP = CHUNK
ROWS = TP * B
N_TT = S // TP
SUB_PER_TOK = D // LANES
PROJ_GROUPS = 2
FN = 512
MC = 512
SCATTER_BATCH = 16
EG = 8
NG = E // EG
FIN_ROWS = 1024
PAD_F = SUBLANES
PAD_B = 2 * SUBLANES


def _params(sem, vmem_mb):
    return pltpu.CompilerParams(dimension_semantics=sem, vmem_limit_bytes=min(vmem_mb << 20, VMEM_LIMIT_V7X))


GELU_C1 = 0.7978845608028654
GELU_C2 = GELU_C1 * 0.044715


def _gelu(x):
    half = 0.5 * x
    return half + half * jnp.tanh(x * (GELU_C1 + GELU_C2 * (x * x)))


def _rms_mod(x, gs, sh):
    return (x * lax.rsqrt(jnp.mean(x * x, axis=-1, keepdims=True) + EPS)) * gs + sh


def _log_sigmoid(x):
    return -(jnp.maximum(-x, 0.0) + jnp.log1p(jnp.exp(-jnp.abs(x))))


def _conv_interleaved(xx, xi_ref, tp, period, cw, cb):
    rows = tp * B
    nslab = DL // LANES
    xi_ref[:, 0:PAD_F, :] = jnp.zeros((nslab, PAD_F, LANES), F32)
    xi_ref[:, PAD_F + rows:PAD_F + rows + PAD_B, :] = jnp.zeros((nslab, PAD_B, LANES), F32)
    for k in range(nslab):
        for b in range(B):
            xi_ref[k, pl.ds(PAD_F + b, tp, stride=B), :] = xx[b * tp:(b + 1) * tp, k * LANES:(k + 1) * LANES]
    pos = lax.broadcasted_iota(jnp.int32, (rows, LANES), 0) >> 3
    pm = pos & (period - 1)
    m0 = pm != 0
    m2 = pm != period - 1
    m3 = pm < period - 2
    outs = []
    for k in range(nslab):
        w = cw[:, k * LANES:(k + 1) * LANES]
        t0 = xi_ref[k, 0:rows, :]
        t1 = xi_ref[k, SUBLANES:SUBLANES + rows, :]
        t2 = xi_ref[k, 2 * SUBLANES:2 * SUBLANES + rows, :]
        t3 = xi_ref[k, 3 * SUBLANES:3 * SUBLANES + rows, :]
        acc = jnp.where(m0, t0, 0.0) * w[0:1] + t1 * w[1:2]
        acc = acc + jnp.where(m2, t2, 0.0) * w[2:3] + jnp.where(m3, t3, 0.0) * w[3:4]
        outs.append(acc + cb[:, k * LANES:(k + 1) * LANES])
    return jnp.concatenate(outs, axis=1)


def _rglru_coeffs(cx, wg_ref, d, ba_half, bi_half, lam):
    cxb = cx.astype(BF16)
    half = DL // 2
    a_parts, b_parts = [], []
    for k in range(2):
        sl = slice(half * k, half * (k + 1))
        zh = jnp.dot(cxb[:, sl], wg_ref[d, k], preferred_element_type=F32)
        t_r = jnp.tanh(zh[:, :half] + ba_half[:, sl])
        t_i = jnp.tanh(zh[:, half:] + bi_half[:, sl])
        ch = (0.5 * RG_C) * -_log_sigmoid(lam[:, sl])
        neg_log_a = ch + ch * t_r
        a = jnp.exp(-neg_log_a)
        one_minus_a2 = jnp.tanh(neg_log_a) * (a * a + 1.0)
        root = jnp.where(one_minus_a2 > 0.0, one_minus_a2 * lax.rsqrt(one_minus_a2), 0.0)
        xh = 0.5 * cx[:, sl]
        a_parts.append(a)
        b_parts.append(root * (xh + xh * t_i))
    return jnp.concatenate(a_parts, axis=1), jnp.concatenate(b_parts, axis=1)


def _scan(a, b, h_out_ref, h0, nsteps, reverse):
    h = h0
    for p in (range(nsteps - 1, -1, -1) if reverse else range(nsteps)):
        rs = slice(p * SUBLANES, (p + 1) * SUBLANES)
        h = a[rs, :] * h + b[rs, :]
        if h_out_ref is not None:
            h_out_ref[rs, :] = h
    return h


def _mod_kernel(c_ref, w_ref, b_ref, o_ref):
    c = c_ref[...]
    s = c * jax.nn.sigmoid(c)
    s_hi = s.astype(BF16)
    s_lo = (s - s_hi.astype(F32)).astype(BF16)
    w = w_ref[...]
    w_hi = w.astype(BF16)
    w_lo = (w - w_hi.astype(F32)).astype(BF16)
    nrow = s.shape[0]
    both = jnp.dot(jnp.concatenate([s_hi, s_lo], axis=0), w_hi, preferred_element_type=F32)
    o_ref[...] = (both[0:nrow] + (both[nrow:] + jnp.dot(s_hi, w_lo, preferred_element_type=F32))) + b_ref[...]


def _modulation(cc, w_mod, b_mod):
    nt = 4
    tn = N_MOD * D // nt
    return pl.pallas_call(
        _mod_kernel,
        grid=(nt,),
        in_specs=[pl.BlockSpec((2 * B, D), lambda i: (0, 0)),
                  pl.BlockSpec((D, tn), lambda i: (0, i)),
                  pl.BlockSpec((1, tn), lambda i: (0, i))],
        out_specs=pl.BlockSpec((2 * B, tn), lambda i: (0, i)),
        out_shape=jax.ShapeDtypeStruct((2 * B, N_MOD * D), F32),
        compiler_params=_params(("arbitrary",), 32),
        name="mod",
    )(cc, w_mod, b_mod)


def _ctx_kernel(ctx_ref, mod_ref, g1_ref, win_ref, cw_ref, cb_ref, wg_ref, ba_ref, bi_ref, lam_ref,
                hf_ref, hb_ref, xi_ref):
    sh = mod_ref[B:B + 1, 0:D]
    gs = g1_ref[...] * (1.0 + mod_ref[B:B + 1, D:2 * D])
    parts = []
    for b in range(B):
        hc = _rms_mod(ctx_ref[b], gs, sh).astype(BF16)
        parts.append(jnp.dot(hc, win_ref[...], preferred_element_type=F32))
    xx = jnp.concatenate(parts, axis=0)
    cx = _conv_interleaved(xx, xi_ref, LC, LC, cw_ref[...], cb_ref[...])
    for d in range(2):
        a, dr = _rglru_coeffs(cx, wg_ref, d, ba_ref[d:d + 1, :], bi_ref[d:d + 1, :], lam_ref[d:d + 1, :])
        h = _scan(a, dr, None, jnp.zeros((B, DL), F32), LC, reverse=(d == 1))
        if d == 0:
            hf_ref[...] = h
        else:
            hb_ref[...] = h


def _context_states(ctx, mod, g1, w_in_x, cw, cb, wg, ba, bi, lam):
    rows = LC * B
    full = lambda shape: pl.BlockSpec(shape, lambda i: (0,) * len(shape))
    return pl.pallas_call(
        _ctx_kernel,
        grid=(1,),
        in_specs=[full((B, LC, D)), full((2 * B, N_MOD * D)), full((1, D)), full((D, DL)), full((4, DL)),
                  full((1, DL)), full((2, 2, DL // 2, DL)), full((2, DL)), full((2, DL)), full((2, DL))],
        out_specs=[full((B, DL)), full((B, DL))],
        out_shape=[jax.ShapeDtypeStruct((B, DL), F32)] * 2,
        scratch_shapes=[pltpu.VMEM((DL // LANES, PAD_F + rows + PAD_B, LANES), F32)],
        compiler_params=_params(("arbitrary",), 56),
        name="ctx",
    )(ctx, mod, g1, w_in_x, cw, cb, wg, ba, bi, lam)


def _mix_in_kernel(x_ref, mod_ref, g1_ref, win_ref, sg_ref, sw_ref, sbias_ref, cw_ref, cb_ref, wg_ref,
                   ba_ref, bi_ref, lam_ref, h0_ref,
                   sgu_ref, gg_ref, cx_ref, hf_ref,
                   hbuf, xi_ref, carry):
    @pl.when(pl.program_id(0) == 0)
    def _():
        carry[...] = h0_ref[...]

    g1 = g1_ref[...]
    for b in range(B):
        sh = mod_ref[b:b + 1, 0:D]
        gs = g1 * (1.0 + mod_ref[b:b + 1, D:2 * D])
        hbuf[b * TP:(b + 1) * TP, :] = _rms_mod(x_ref[b], gs, sh).astype(BF16)
    z = jnp.dot(hbuf[...], win_ref[...], preferred_element_type=F32)
    ug = _gelu(z[:, 0:DS])
    vg = _gelu(z[:, DS:2 * DS])
    xx = z[:, 2 * DS:2 * DS + DL]
    gg = _gelu(z[:, 2 * DS + DL:])
    for b in range(B):
        gg_ref[b] = gg[b * TP:(b + 1) * TP, :].astype(BF16)

    for h in range(NH):
        hs = slice(h * HD, (h + 1) * HD)
        vh = vg[:, hs]
        vn = (vh * lax.rsqrt(jnp.mean(vh * vh, axis=-1, keepdims=True) + EPS)) * sg_ref[:, hs]
        vnb = vn.astype(BF16)
        for b in range(B):
            rs = slice(b * TP, (b + 1) * TP)
            s = jnp.dot(sw_ref[h], vnb[rs], preferred_element_type=F32) + sbias_ref[:, hs]
            sgu_ref[b, :, hs] = (ug[rs, hs] * s).astype(BF16)

    cx = _conv_interleaved(xx, xi_ref, TP, GRID_W, cw_ref[...], cb_ref[...])
    cx_ref[...] = cx
    a, dr = _rglru_coeffs(cx, wg_ref, 0, ba_ref[0:1, :], bi_ref[0:1, :], lam_ref[0:1, :])
    carry[...] = _scan(a, dr, hf_ref, carry[...], TP, reverse=False)


def _mix_in(x, mod, g1, w_in_b, sg, sw_b, sbias, cw, cb, wg, ba, bi, lam, h0f):
    full = lambda shape: pl.BlockSpec(shape, lambda i: (0,) * len(shape))
    return pl.pallas_call(
        _mix_in_kernel,
        grid=(N_TT,),
        in_specs=[pl.BlockSpec((B, TP, D), lambda i: (0, i, 0)),
                  full((2 * B, N_MOD * D)), full((1, D)), full((D, 2 * DS + 2 * DL)), full((1, DS)),
                  full((NH, CHUNK, CHUNK)), full((CHUNK, DS)), full((4, DL)), full((1, DL)),
                  full((2, 2, DL // 2, DL)), full((2, DL)), full((2, DL)), full((2, DL)), full((B, DL))],
        out_specs=[pl.BlockSpec((B, TP, DS), lambda i: (0, i, 0)),
                   pl.BlockSpec((B, TP, DL), lambda i: (0, i, 0)),
                   pl.BlockSpec((ROWS, DL), lambda i: (i, 0)),
                   pl.BlockSpec((ROWS, DL), lambda i: (i, 0))],
        out_shape=[jax.ShapeDtypeStruct((B, S, DS), BF16), jax.ShapeDtypeStruct((B, S, DL), BF16),
                   jax.ShapeDtypeStruct((S * B, DL), F32), jax.ShapeDtypeStruct((S * B, DL), F32)],
        scratch_shapes=[pltpu.VMEM((ROWS, D), BF16),
                        pltpu.VMEM((DL // LANES, PAD_F + ROWS + PAD_B, LANES), F32),
                        pltpu.VMEM((B, DL), F32)],
        compiler_params=_params(("arbitrary",), 56),
        name="mix_in",
    )(x, mod, g1, w_in_b, sg, sw_b, sbias, cw, cb, wg, ba, bi, lam, h0f)


def _mix_out_kernel(x_ref, mod_ref, cx_ref, hf_ref, sgu_ref, gg_ref, wg_ref, ba_ref, bi_ref, lam_ref, h0_ref,
                    wout_ref, g2_ref, wr_ref,
                    x1_ref, hx2t_ref, lg_ref,
                    hs_ref, carry):
    @pl.when(pl.program_id(0) == 0)
    def _():
        carry[...] = h0_ref[...]

    a, dr = _rglru_coeffs(cx_ref[...], wg_ref, 1, ba_ref[1:2, :], bi_ref[1:2, :], lam_ref[1:2, :])
    nslab = DL // LANES
    h = carry[...]
    for p in range(TP - 1, -1, -1):
        rs = slice(p * SUBLANES, (p + 1) * SUBLANES)
        h = a[rs, :] * h + dr[rs, :]
        hsum = hf_ref[rs, :] + h
        for k in range(nslab):
            hs_ref[k, rs, :] = hsum[:, k * LANES:(k + 1) * LANES]
    carry[...] = h
    rows = []
    for b in range(B):
        rec = [(gg_ref[b, :, k * LANES:(k + 1) * LANES].astype(F32)
                * hs_ref[k, pl.ds(b, TP, stride=B), :]).astype(BF16) for k in range(nslab)]
        rows.append(jnp.concatenate([sgu_ref[b]] + rec, axis=1))
    gb = B // PROJ_GROUPS
    ys = [jnp.dot(jnp.concatenate(rows[g * gb:(g + 1) * gb], axis=0), wout_ref[...], preferred_element_type=F32)
          for g in range(PROJ_GROUPS)]

    g2 = g2_ref[...]
    wr = wr_ref[...]
    wr_hi = wr.astype(BF16)
    wr_lo = (wr - wr_hi.astype(F32)).astype(BF16)
    wr_both = jnp.concatenate([wr_hi, wr_lo], axis=0)
    his, los = [], []
    for b in range(B):
        g1x = mod_ref[b:b + 1, 2 * D:3 * D]
        sh2 = mod_ref[b:b + 1, 3 * D:4 * D]
        gs2 = g2 * (1.0 + mod_ref[b:b + 1, 4 * D:5 * D])
        x1 = x_ref[b] + g1x * ys[b // gb][(b % gb) * TP:(b % gb + 1) * TP, :]
        x1_ref[b] = x1
        hx2 = _rms_mod(x1, gs2, sh2)
        for k in range(SUB_PER_TOK):
            hx2t_ref[b, pl.ds(k, TP, stride=SUB_PER_TOK), :] = hx2[:, k * LANES:(k + 1) * LANES]
        hx_hi = hx2.astype(BF16)
        his.append(hx_hi)
        los.append((hx2 - hx_hi.astype(F32)).astype(BF16))

    nt = (((1,), (1,)), ((), ()))
    p = lax.dot_general(wr_both, jnp.concatenate(his, axis=0), nt, preferred_element_type=F32)
    q = lax.dot_general(wr_hi, jnp.concatenate(los, axis=0), nt, preferred_element_type=F32)
    lg = p[0:E] + (q + p[E:2 * E])
    for b in range(B):
        lg_ref[b] = lg[:, b * TP:(b + 1) * TP]


def _mix_out(x, mod, cx, hf, sgu, gg, wg, ba, bi, lam, h0b, w_out_b, g2, wr_t):
    full = lambda shape: pl.BlockSpec(shape, lambda i: (0,) * len(shape))
    rev = lambda i: N_TT - 1 - i
    return pl.pallas_call(
        _mix_out_kernel,
        grid=(N_TT,),
        in_specs=[pl.BlockSpec((B, TP, D), lambda i: (0, rev(i), 0)),
                  full((2 * B, N_MOD * D)),
                  pl.BlockSpec((ROWS, DL), lambda i: (rev(i), 0)),
                  pl.BlockSpec((ROWS, DL), lambda i: (rev(i), 0)),
                  pl.BlockSpec((B, TP, DS), lambda i: (0, rev(i), 0)),
                  pl.BlockSpec((B, TP, DL), lambda i: (0, rev(i), 0)),
                  full((2, 2, DL // 2, DL)), full((2, DL)), full((2, DL)), full((2, DL)), full((B, DL)),
                  full((D, D)), full((1, D)), full((E, D))],
        out_specs=[pl.BlockSpec((B, TP, D), lambda i: (0, rev(i), 0)),
                   pl.BlockSpec((B, TP * SUB_PER_TOK, LANES), lambda i: (0, rev(i), 0)),
                   pl.BlockSpec((B, E, TP), lambda i: (0, 0, rev(i)))],
        out_shape=[jax.ShapeDtypeStruct((B, S, D), F32),
                   jax.ShapeDtypeStruct((B, S * SUB_PER_TOK, LANES), F32),
                   jax.ShapeDtypeStruct((B, E, S), F32)],
        scratch_shapes=[pltpu.VMEM((DL // LANES, ROWS, LANES), F32),
                        pltpu.VMEM((B, DL), F32)],
        compiler_params=_params(("arbitrary",), 56),
        name="mix_out",
    )(x, mod, cx, hf, sgu, gg, wg, ba, bi, lam, h0b, w_out_b, g2, wr_t)


NBLK = S // LANES
BIG = 1.0e9


def _bf16_parts(a):
    hi = a.astype(BF16).astype(F32)
    r1 = a - hi
    mid = r1.astype(BF16).astype(F32)
    lo = (r1 - mid).astype(BF16).astype(F32)
    return hi, mid, lo


def _route_kernel(lg_ref, idx_ref, gate_ref, aff_s, lci_s, offb_s, ahi_s, amid_s, alo_s, offi_s):
    for b in range(B):
        l = lg_ref[b]
        ex = jnp.exp(l - jnp.max(l, axis=0, keepdims=True))
        aff_s[b * E:(b + 1) * E, :] = ex / jnp.sum(ex, axis=0, keepdims=True)
    aff = aff_s[...]
    nrow = B * E

    def bisect(_, lohi):
        lo, hi = lohi
        mid = lo + ((hi - lo + 1) >> 1)
        cnt = jnp.sum(jnp.where(aff >= pltpu.bitcast(mid, F32), 1.0, 0.0), axis=1, keepdims=True)
        ge = cnt >= float(CAP)
        return jnp.where(ge, mid, lo), jnp.where(ge, hi, mid - 1)

    lo0 = jnp.zeros((nrow, 1), jnp.int32)
    hi0 = jnp.full((nrow, 1), 0x7F800000, jnp.int32)
    thr_bits, _ = lax.fori_loop(0, 31, bisect, (lo0, hi0))
    thr = pltpu.bitcast(thr_bits, F32)
    gt = aff > thr
    eq = aff == thr
    need = float(CAP) - jnp.sum(jnp.where(gt, 1.0, 0.0), axis=1, keepdims=True)

    qi = lax.broadcasted_iota(jnp.int32, (LANES, LANES), 0)
    ti = lax.broadcasted_iota(jnp.int32, (LANES, LANES), 1)
    tri = jnp.where(qi <= ti, 1.0, 0.0).astype(BF16)
    blocks = [slice(j * LANES, (j + 1) * LANES) for j in range(NBLK)]

    ties_before = jnp.zeros((nrow, 1), F32)
    sel = []
    for sl in blocks:
        eqb = jnp.where(eq[:, sl], 1.0, 0.0)
        incl = jnp.dot(eqb.astype(BF16), tri, preferred_element_type=F32) + ties_before
        sel.append(gt[:, sl] | (eq[:, sl] & ((incl - eqb) < need)))
        ties_before = incl[:, LANES - 1:LANES]

    lane_sq = lax.broadcasted_iota(jnp.int32, (nrow, LANES), 1)
    off = jnp.zeros((nrow, 1), F32)
    offi = jnp.full((nrow, LANES), BIG, F32)
    for j, sl in enumerate(blocks):
        rows = slice(j * nrow, (j + 1) * nrow)
        lci = jnp.dot(jnp.where(sel[j], 1.0, 0.0).astype(BF16), tri, preferred_element_type=F32)
        lci_s[rows, :] = lci
        offb_s[rows, :] = jnp.broadcast_to(off, (nrow, LANES))
        ahi_s[rows, :], amid_s[rows, :], alo_s[rows, :] = _bf16_parts(aff[:, sl])
        off = off + lci[:, LANES - 1:LANES]
        offi = jnp.where(lane_sq == j, off, offi)
    offi_s[...] = offi

    slot = lax.broadcasted_iota(jnp.int32, (CAP, LANES), 0).astype(F32)
    lane = lax.broadcasted_iota(jnp.int32, (CAP, LANES), 1)
    lane_f = lane.astype(F32)
    zpad = jnp.zeros((LANES - NBLK, 5 * LANES), BF16)

    def row_body(r, carry):
        idxm, gm = carry
        take = lambda ref: ref[pl.ds(r, NBLK, stride=nrow), :]
        table = jnp.concatenate([take(lci_s), take(offb_s), take(ahi_s), take(amid_s), take(alo_s)], axis=1)
        table = jnp.concatenate([table.astype(BF16), zpad], axis=0)
        blk = jnp.sum(jnp.where(offi_s[pl.ds(r, 1), :] <= slot, 1.0, 0.0), axis=1, keepdims=True)
        pick = jnp.where(lane_f == blk, 1.0, 0.0).astype(BF16)
        res = jnp.dot(pick, table, preferred_element_type=F32)
        rank1 = (slot + 1.0) - res[:, LANES:2 * LANES]
        tokl = jnp.sum(jnp.where(res[:, 0:LANES] < rank1, 1.0, 0.0), axis=1, keepdims=True)
        affs = (res[:, 2 * LANES:3 * LANES] + res[:, 3 * LANES:4 * LANES]) + res[:, 4 * LANES:5 * LANES]
        gv = jnp.sum(jnp.where(lane_f == tokl, affs, 0.0), axis=1, keepdims=True)
        iv = blk * float(LANES) + tokl
        put = lane == r
        return jnp.where(put, iv, idxm), jnp.where(put, gv, gm)

    z = jnp.zeros((CAP, LANES), F32)
    idxm, gm = lax.fori_loop(0, nrow, row_body, (z, z), unroll=8)
    idx_ref[...] = (idxm.T * float(SUB_PER_TOK)).astype(jnp.int32)
    gate_ref[...] = gm


def _route(logits_t):
    full = lambda shape: pl.BlockSpec(shape, lambda i: (0,) * len(shape))
    return pl.pallas_call(
        _route_kernel,
        grid=(1,),
        in_specs=[full((B, E, S))],
        out_specs=[full((B * E, CAP)), full((CAP, B * E))],
        out_shape=[jax.ShapeDtypeStruct((B * E, CAP), jnp.int32), jax.ShapeDtypeStruct((CAP, B * E), F32)],
        scratch_shapes=[pltpu.VMEM((B * E, S), F32)] + [pltpu.VMEM((NBLK * B * E, LANES), F32)] * 5
                       + [pltpu.VMEM((B * E, LANES), F32)],
        compiler_params=_params(("arbitrary",), 48),
        name="route",
    )(logits_t)


def _dispatch_kernel(idx_ref, h_ref, xg_ref, xt0, xt1):
    for e in range(E):
        xt = xt0 if e % 2 == 0 else xt1
        for s in range(CAP):
            src = pl.multiple_of(idx_ref[e * CAP + s], SUB_PER_TOK)
            xt[s * SUB_PER_TOK:(s + 1) * SUB_PER_TOK, :] = h_ref[0, pl.ds(src, SUB_PER_TOK), :]
        for k in range(SUB_PER_TOK):
            xg_ref[e, :, k * LANES:(k + 1) * LANES] = xt[pl.ds(k, CAP, stride=SUB_PER_TOK), :].astype(BF16)


def _dispatch(idx1, hx2t):
    return pl.pallas_call(
        _dispatch_kernel,
        grid=(B,),
        in_specs=[pl.BlockSpec((E * CAP,), lambda b: (b,), memory_space=pltpu.SMEM),
                  pl.BlockSpec((1, S * SUB_PER_TOK, LANES), lambda b: (b, 0, 0))],
        out_specs=pl.BlockSpec((E, CAP, D), lambda b: (0, b, 0)),
        out_shape=jax.ShapeDtypeStruct((E, B * CAP, D), BF16),
        scratch_shapes=[pltpu.VMEM((CAP * SUB_PER_TOK, LANES), F32), pltpu.VMEM((CAP * SUB_PER_TOK, LANES), F32)],
        compiler_params=_params(("arbitrary",), 40),
        name="dispatch",
    )(idx1, hx2t)


def _moe_kernel(x_ref, gate_ref, w1_ref, w3_ref, w2_ref, y_ref, hid_s, w2_s):
    f = pl.program_id(1)
    w1 = w1_ref[0].astype(BF16)
    w3 = w3_ref[0].astype(BF16)
    fs = pl.multiple_of(f * FN, FN)
    w2_s[pl.ds(fs, FN), :] = w2_ref[0].astype(BF16)
    for c in range(B * CAP // MC):
        rs = slice(c * MC, (c + 1) * MC)
        x = x_ref[0, rs, :]
        h1 = jnp.dot(x, w1, preferred_element_type=F32)
        h3 = jnp.dot(x, w3, preferred_element_type=F32)
        half = 0.5 * h1
        hid_s[f, rs, :] = ((half + half * jnp.tanh(half)) * h3).astype(BF16)

    @pl.when(f == FF // FN - 1)
    def _():
        gates = gate_ref[...]
        lane = lax.broadcasted_iota(jnp.int32, gates.shape, 1)
        e = pl.program_id(0)
        for c in range(B * CAP // MC):
            rs = slice(c * MC, (c + 1) * MC)
            hid = jnp.concatenate([hid_s[j, rs, :] for j in range(FF // FN)], axis=1)
            y = jnp.dot(hid, w2_s[...], preferred_element_type=F32)
            for bb in range(MC // CAP):
                b = c * (MC // CAP) + bb
                gcol = jnp.sum(jnp.where(lane == b * E + e, gates, 0.0), axis=1, keepdims=True)
                y_ref[0, b * CAP:(b + 1) * CAP, :] = (y[bb * CAP:(bb + 1) * CAP, :] * gcol).astype(BF16)


def _moe(xg, gate_cols, w1, w3, w2):
    return pl.pallas_call(
        _moe_kernel,
        grid=(E, FF // FN),
        in_specs=[pl.BlockSpec((1, B * CAP, D), lambda e, f: (e, 0, 0)),
                  pl.BlockSpec((CAP, B * E), lambda e, f: (0, 0)),
                  pl.BlockSpec((1, D, FN), lambda e, f: (e, 0, f)),
                  pl.BlockSpec((1, D, FN), lambda e, f: (e, 0, f)),
                  pl.BlockSpec((1, FN, D), lambda e, f: (e, f, 0))],
        out_specs=pl.BlockSpec((1, B * CAP, D), lambda e, f: (e, 0, 0)),
        out_shape=jax.ShapeDtypeStruct((E, B * CAP, D), BF16),
        scratch_shapes=[pltpu.VMEM((FF // FN, B * CAP, FN), BF16), pltpu.VMEM((FF, D), BF16)],
        compiler_params=_params(("arbitrary", "arbitrary"), 56),
        name="moe",
    )(xg, gate_cols, w1, w3, w2)


def _combine_kernel(idx_ref, y_ref, x1_ref, mod_ref, fg_ref, out_ref, acc, yt0, yt1):
    b = pl.program_id(0)
    g = pl.program_id(1)

    @pl.when(g == 0)
    def _():
        acc[...] = jnp.zeros(acc.shape, F32)

    @pl.when(g < NG)
    def _():
        for el in range(EG):
            yt = yt0 if el % 2 == 0 else yt1
            for k in range(SUB_PER_TOK):
                yt[pl.ds(k, CAP, stride=SUB_PER_TOK), :] = y_ref[el, :, k * LANES:(k + 1) * LANES].astype(F32)
            for s0 in range(0, CAP, SCATTER_BATCH):
                dsts, news = [], []
                for j in range(SCATTER_BATCH):
                    dst = pl.multiple_of(idx_ref[el * CAP + s0 + j], SUB_PER_TOK)
                    src = (s0 + j) * SUB_PER_TOK
                    dsts.append(dst)
                    news.append(acc[pl.ds(dst, SUB_PER_TOK), :] + yt[src:src + SUB_PER_TOK, :])
                for dst, new in zip(dsts, news):
                    acc[pl.ds(dst, SUB_PER_TOK), :] = new

    @pl.when(g >= NG)
    def _():
        row0 = pl.multiple_of((g - NG) * (FIN_ROWS * SUB_PER_TOK), FIN_ROWS * SUB_PER_TOK)
        g2x = mod_ref[pl.ds(b, 1), 5 * D:6 * D]
        sq = jnp.zeros((FIN_ROWS, LANES), F32)
        for k in range(SUB_PER_TOK):
            ls = slice(k * LANES, (k + 1) * LANES)
            xo = x1_ref[0, :, ls] + g2x[:, ls] * acc[pl.ds(row0 + k, FIN_ROWS, stride=SUB_PER_TOK), :]
            out_ref[0, :, ls] = xo
            sq = sq + xo * xo
        inv = lax.rsqrt(jnp.sum(sq, axis=1, keepdims=True) * (1.0 / D) + EPS)
        out_ref[0] = (out_ref[0] * inv) * fg_ref[...]


def _combine(idx1, y, x1, mod, fg):
    fin = lambda b, g: (b, jnp.maximum(g - NG, 0), 0)
    grp = lambda g: jnp.minimum(g, NG - 1)
    return pl.pallas_call(
        _combine_kernel,
        grid=(B, NG + S // FIN_ROWS),
        in_specs=[pl.BlockSpec((EG * CAP,), lambda b, g: (b * NG + grp(g),), memory_space=pltpu.SMEM),
                  pl.BlockSpec((EG, CAP, D), lambda b, g: (grp(g), b, 0)),
                  pl.BlockSpec((1, FIN_ROWS, D), fin),
                  pl.BlockSpec((2 * B, N_MOD * D), lambda b, g: (0, 0)),
                  pl.BlockSpec((1, D), lambda b, g: (0, 0))],
        out_specs=pl.BlockSpec((1, FIN_ROWS, D), fin),
        out_shape=jax.ShapeDtypeStruct((B, S, D), F32),
        scratch_shapes=[pltpu.VMEM((S * SUB_PER_TOK, LANES), F32),
                        pltpu.VMEM((CAP * SUB_PER_TOK, LANES), F32), pltpu.VMEM((CAP * SUB_PER_TOK, LANES), F32)],
        compiler_params=_params(("arbitrary", "arbitrary"), 56),
        name="combine",
    )(idx1, y, x1, mod, fg)


def _pack_gate_weights(wa, wi):
    eye = jnp.eye(4, dtype=wa.dtype)

    def bdiag(w4):
        return jnp.einsum('hij,hg->higj', w4, eye).reshape(4 * LRU_HD, 4 * LRU_HD)

    dirs = []
    for d in range(2):
        halves = []
        for k in range(2):
            hs = slice(4 * k, 4 * (k + 1))
            halves.append(jnp.concatenate([bdiag(wa[d, hs]), bdiag(wi[d, hs])], axis=1))
        dirs.append(jnp.stack(halves))
    return (0.5 * jnp.stack(dirs)).astype(BF16)


def kernel(x, c, ctx, c_ctx, w_mod, b_mod, norm1_g, norm2_g, w_in, sgu_g, sgu_w, sgu_b, conv_w, conv_b,
           rg_wa, rg_ba, rg_wi, rg_bi, rg_lam, w_out, w_router, w1, w3, w2, final_g):
    assert x.shape == (B, S, D) and ctx.shape == (B, LC, D) and w_mod.shape[0] == 1

    cc = jnp.concatenate([c, c_ctx[None, :], jnp.zeros((B - 1, D), F32)], axis=0)
    mod = _modulation(cc, w_mod[0], b_mod[0][None, :])

    g1 = norm1_g[0][None, :]
    g2 = norm2_g[0][None, :]
    w_in_b = w_in[0].astype(BF16)
    w_in_x = w_in_b[:, 2 * DS:2 * DS + DL]
    wg = _pack_gate_weights(rg_wa[0], rg_wi[0])
    cw = conv_w[0]
    cb = conv_b[0][None, :]
    ba, bi, lam = 0.5 * rg_ba[0], 0.5 * rg_bi[0], rg_lam[0]
    sbias = jnp.repeat(sgu_b[0].T, HD, axis=1)

    h0f, h0b = _context_states(ctx, mod, g1, w_in_x, cw, cb, wg, ba, bi, lam)
    sgu, gg, cx, hf = _mix_in(x, mod, g1, w_in_b, sgu_g[0][None, :], sgu_w[0].astype(BF16), sbias, cw, cb,
                              wg, ba, bi, lam, h0f)
    x1, hx2t, logits_t = _mix_out(x, mod, cx, hf, sgu, gg, wg, ba, bi, lam, h0b, w_out[0].astype(BF16), g2,
                                  w_router[0].T)
    idx, gate = _route(logits_t)
    idx1 = idx.reshape(B * E * CAP)
    xg = _dispatch(idx1, hx2t)
    y = _moe(xg, gate, w1[0], w3[0], w2[0])
    return _combine(idx1, y, x1, mod, final_g[None, :])
```

```python
import jax
import jax.numpy as jnp
from jax import lax
from jax.experimental import pallas as pl
from jax.experimental.pallas import tpu as pltpu

F32 = jnp.float32
BF16 = jnp.bfloat16

D = 1024
B = 8
S = 2048
LC = 256
GRID_W = 64
DS = 512
NH = 4
HD = DS // NH
CHUNK = 128
DL = 512
LRU_HEADS = 8
LRU_HD = DL // LRU_HEADS
E = 16
CAP = 2 * S // E
FF = 2048
N_MOD = 6
EPS = 1e-6
RG_C = 8.0

SUBLANES = 8
LANES = 128
VMEM_LIMIT_V7X = 60000 * 1024

TP = CHUNK
ROWS = TP * B
N_TT = S // TP
SUB_PER_TOK = D // LANES
PROJ_GROUPS = 2
FN = 512
MC = 512
W_BUF = 3
SCATTER_BATCH = 16
EG = 8
NG = E // EG
FIN_ROWS = 1024
PAD_F = SUBLANES
PAD_B = 2 * SUBLANES


def _params(sem, vmem_mb):
    return pltpu.CompilerParams(dimension_semantics=sem, vmem_limit_bytes=min(vmem_mb << 20, VMEM_LIMIT_V7X))


GELU_C1 = 0.7978845608028654
GELU_C2 = GELU_C1 * 0.044715


def _gelu(x):
    half = 0.5 * x
    return half + half * jnp.tanh(x * (GELU_C1 + GELU_C2 * (x * x)))


def _rms_mod(x, gs, sh):
    return (x * lax.rsqrt(jnp.mean(x * x, axis=-1, keepdims=True) + EPS)) * gs + sh


def _log_sigmoid(x):
    return -(jnp.maximum(-x, 0.0) + jnp.log1p(jnp.exp(-jnp.abs(x))))


def _conv_interleaved(xx, xi_ref, tp, period, cw, cb):
    rows = tp * B
    nslab = DL // LANES
    xi_ref[:, 0:PAD_F, :] = jnp.zeros((nslab, PAD_F, LANES), F32)
    xi_ref[:, PAD_F + rows:PAD_F + rows + PAD_B, :] = jnp.zeros((nslab, PAD_B, LANES), F32)
    for k in range(nslab):
        for b in range(B):
            xi_ref[k, pl.ds(PAD_F + b, tp, stride=B), :] = xx[b * tp:(b + 1) * tp, k * LANES:(k + 1) * LANES]
    pos = lax.broadcasted_iota(jnp.int32, (rows, LANES), 0) >> 3
    pm = pos & (period - 1)
    m0 = pm != 0
    m2 = pm != period - 1
    m3 = pm < period - 2
    outs = []
    for k in range(nslab):
        w = cw[:, k * LANES:(k + 1) * LANES]
        t0 = xi_ref[k, 0:rows, :]
        t1 = xi_ref[k, SUBLANES:SUBLANES + rows, :]
        t2 = xi_ref[k, 2 * SUBLANES:2 * SUBLANES + rows, :]
        t3 = xi_ref[k, 3 * SUBLANES:3 * SUBLANES + rows, :]
        acc = jnp.where(m0, t0, 0.0) * w[0:1] + t1 * w[1:2]
        acc = acc + jnp.where(m2, t2, 0.0) * w[2:3] + jnp.where(m3, t3, 0.0) * w[3:4]
        outs.append(acc + cb[:, k * LANES:(k + 1) * LANES])
    return jnp.concatenate(outs, axis=1)


def _rglru_coeffs(cx, wg_ref, d, ba_half, bi_half, lam):
    cxb = cx.astype(BF16)
    half = DL // 2
    a_parts, b_parts = [], []
    for k in range(2):
        sl = slice(half * k, half * (k + 1))
        zh = jnp.dot(cxb[:, sl], wg_ref[d, k], preferred_element_type=F32)
        t_r = jnp.tanh(zh[:, :half] + ba_half[:, sl])
        t_i = jnp.tanh(zh[:, half:] + bi_half[:, sl])
        ch = (0.5 * RG_C) * -_log_sigmoid(lam[:, sl])
        neg_log_a = ch + ch * t_r
        a = jnp.exp(-neg_log_a)
        one_minus_a2 = jnp.tanh(neg_log_a) * (a * a + 1.0)
        root = jnp.where(one_minus_a2 > 0.0, one_minus_a2 * lax.rsqrt(one_minus_a2), 0.0)
        xh = 0.5 * cx[:, sl]
        a_parts.append(a)
        b_parts.append(root * (xh + xh * t_i))
    return jnp.concatenate(a_parts, axis=1), jnp.concatenate(b_parts, axis=1)


def _scan(a, b, h_out_ref, h0, nsteps, reverse):
    h = h0
    for p in (range(nsteps - 1, -1, -1) if reverse else range(nsteps)):
        rs = slice(p * SUBLANES, (p + 1) * SUBLANES)
        h = a[rs, :] * h + b[rs, :]
        if h_out_ref is not None:
            h_out_ref[rs, :] = h
    return h


def _mod_kernel(c_ref, w_ref, b_ref, o_ref):
    c = c_ref[...]
    s = c * jax.nn.sigmoid(c)
    s_hi = s.astype(BF16)
    s_lo = (s - s_hi.astype(F32)).astype(BF16)
    w = w_ref[...]
    w_hi = w.astype(BF16)
    w_lo = (w - w_hi.astype(F32)).astype(BF16)
    nrow = s.shape[0]
    both = jnp.dot(jnp.concatenate([s_hi, s_lo], axis=0), w_hi, preferred_element_type=F32)
    o_ref[...] = (both[0:nrow] + (both[nrow:] + jnp.dot(s_hi, w_lo, preferred_element_type=F32))) + b_ref[...]


def _modulation(cc, w_mod, b_mod):
    nt = 4
    tn = N_MOD * D // nt
    return pl.pallas_call(
        _mod_kernel,
        grid=(nt,),
        in_specs=[pl.BlockSpec((2 * B, D), lambda i: (0, 0)),
                  pl.BlockSpec((D, tn), lambda i: (0, i)),
                  pl.BlockSpec((1, tn), lambda i: (0, i))],
        out_specs=pl.BlockSpec((2 * B, tn), lambda i: (0, i)),
        out_shape=jax.ShapeDtypeStruct((2 * B, N_MOD * D), F32),
        compiler_params=_params(("arbitrary",), 32),
        name="mod",
    )(cc, w_mod, b_mod)


def _ctx_kernel(ctx_ref, mod_ref, g1_ref, win_ref, cw_ref, cb_ref, wg_ref, ba_ref, bi_ref, lam_ref,
                hf_ref, hb_ref, xi_ref):
    sh = mod_ref[B:B + 1, 0:D]
    gs = g1_ref[...] * (1.0 + mod_ref[B:B + 1, D:2 * D])
    parts = []
    for b in range(B):
        hc = _rms_mod(ctx_ref[b], gs, sh).astype(BF16)
        parts.append(jnp.dot(hc, win_ref[...], preferred_element_type=F32))
    xx = jnp.concatenate(parts, axis=0)
    cx = _conv_interleaved(xx, xi_ref, LC, LC, cw_ref[...], cb_ref[...])
    for d in range(2):
        a, dr = _rglru_coeffs(cx, wg_ref, d, ba_ref[d:d + 1, :], bi_ref[d:d + 1, :], lam_ref[d:d + 1, :])
        h = _scan(a, dr, None, jnp.zeros((B, DL), F32), LC, reverse=(d == 1))
        if d == 0:
            hf_ref[...] = h
        else:
            hb_ref[...] = h


def _context_states(ctx, mod, g1, w_in_x, cw, cb, wg, ba, bi, lam):
    rows = LC * B
    full = lambda shape: pl.BlockSpec(shape, lambda i: (0,) * len(shape))
    return pl.pallas_call(
        _ctx_kernel,
        grid=(1,),
        in_specs=[full((B, LC, D)), full((2 * B, N_MOD * D)), full((1, D)), full((D, DL)), full((4, DL)),
                  full((1, DL)), full((2, 2, DL // 2, DL)), full((2, DL)), full((2, DL)), full((2, DL))],
        out_specs=[full((B, DL)), full((B, DL))],
        out_shape=[jax.ShapeDtypeStruct((B, DL), F32)] * 2,
        scratch_shapes=[pltpu.VMEM((DL // LANES, PAD_F + rows + PAD_B, LANES), F32)],
        compiler_params=_params(("arbitrary",), 56),
        name="ctx",
    )(ctx, mod, g1, w_in_x, cw, cb, wg, ba, bi, lam)


def _mix_in_kernel(x_ref, mod_ref, g1_ref, win_ref, sg_ref, sw_ref, sbias_ref, cw_ref, cb_ref, wg_ref,
                   ba_ref, bi_ref, lam_ref, h0_ref,
                   sgu_ref, gg_ref, cx_ref, hf_ref,
                   hbuf, xi_ref, carry):
    @pl.when(pl.program_id(0) == 0)
    def _():
        carry[...] = h0_ref[...]

    g1 = g1_ref[...]
    for b in range(B):
        sh = mod_ref[b:b + 1, 0:D]
        gs = g1 * (1.0 + mod_ref[b:b + 1, D:2 * D])
        hbuf[b * TP:(b + 1) * TP, :] = _rms_mod(x_ref[b], gs, sh).astype(BF16)
    z = jnp.dot(hbuf[...], win_ref[...], preferred_element_type=F32)
    ug = _gelu(z[:, 0:DS])
    vg = _gelu(z[:, DS:2 * DS])
    xx = z[:, 2 * DS:2 * DS + DL]
    gg = _gelu(z[:, 2 * DS + DL:])
    for b in range(B):
        gg_ref[b] = gg[b * TP:(b + 1) * TP, :].astype(BF16)

    for h in range(NH):
        hs = slice(h * HD, (h + 1) * HD)
        vh = vg[:, hs]
        vn = (vh * lax.rsqrt(jnp.mean(vh * vh, axis=-1, keepdims=True) + EPS)) * sg_ref[:, hs]
        vnb = vn.astype(BF16)
        for b in range(B):
            rs = slice(b * TP, (b + 1) * TP)
            s = jnp.dot(sw_ref[h], vnb[rs], preferred_element_type=F32) + sbias_ref[:, hs]
            sgu_ref[b, :, hs] = (ug[rs, hs] * s).astype(BF16)

    cx = _conv_interleaved(xx, xi_ref, TP, GRID_W, cw_ref[...], cb_ref[...])
    cx_ref[...] = cx
    a, dr = _rglru_coeffs(cx, wg_ref, 0, ba_ref[0:1, :], bi_ref[0:1, :], lam_ref[0:1, :])
    carry[...] = _scan(a, dr, hf_ref, carry[...], TP, reverse=False)


def _mix_in(x, mod, g1, w_in_b, sg, sw_b, sbias, cw, cb, wg, ba, bi, lam, h0f):
    full = lambda shape: pl.BlockSpec(shape, lambda i: (0,) * len(shape))
    return pl.pallas_call(
        _mix_in_kernel,
        grid=(N_TT,),
        in_specs=[pl.BlockSpec((B, TP, D), lambda i: (0, i, 0)),
                  full((2 * B, N_MOD * D)), full((1, D)), full((D, 2 * DS + 2 * DL)), full((1, DS)),
                  full((NH, CHUNK, CHUNK)), full((CHUNK, DS)), full((4, DL)), full((1, DL)),
                  full((2, 2, DL // 2, DL)), full((2, DL)), full((2, DL)), full((2, DL)), full((B, DL))],
        out_specs=[pl.BlockSpec((B, TP, DS), lambda i: (0, i, 0)),
                   pl.BlockSpec((B, TP, DL), lambda i: (0, i, 0)),
                   pl.BlockSpec((ROWS, DL), lambda i: (i, 0)),
                   pl.BlockSpec((ROWS, DL), lambda i: (i, 0))],
        out_shape=[jax.ShapeDtypeStruct((B, S, DS), BF16), jax.ShapeDtypeStruct((B, S, DL), BF16),
                   jax.ShapeDtypeStruct((S * B, DL), F32), jax.ShapeDtypeStruct((S * B, DL), F32)],
        scratch_shapes=[pltpu.VMEM((ROWS, D), BF16),
                        pltpu.VMEM((DL // LANES, PAD_F + ROWS + PAD_B, LANES), F32),
                        pltpu.VMEM((B, DL), F32)],
        compiler_params=_params(("arbitrary",), 56),
        name="mix_in",
    )(x, mod, g1, w_in_b, sg, sw_b, sbias, cw, cb, wg, ba, bi, lam, h0f)


def _mix_out_kernel(x_ref, mod_ref, cx_ref, hf_ref, sgu_ref, gg_ref, wg_ref, ba_ref, bi_ref, lam_ref, h0_ref,
                    wout_ref, g2_ref, wr_ref,
                    x1_ref, hx2t_ref, lg_ref,
                    hs_ref, carry):
    @pl.when(pl.program_id(0) == 0)
    def _():
        carry[...] = h0_ref[...]

    a, dr = _rglru_coeffs(cx_ref[...], wg_ref, 1, ba_ref[1:2, :], bi_ref[1:2, :], lam_ref[1:2, :])
    nslab = DL // LANES
    h = carry[...]
    for p in range(TP - 1, -1, -1):
        rs = slice(p * SUBLANES, (p + 1) * SUBLANES)
        h = a[rs, :] * h + dr[rs, :]
        hsum = hf_ref[rs, :] + h
        for k in range(nslab):
            hs_ref[k, rs, :] = hsum[:, k * LANES:(k + 1) * LANES]
    carry[...] = h
    rows = []
    for b in range(B):
        rec = [(gg_ref[b, :, k * LANES:(k + 1) * LANES].astype(F32)
                * hs_ref[k, pl.ds(b, TP, stride=B), :]).astype(BF16) for k in range(nslab)]
        rows.append(jnp.concatenate([sgu_ref[b]] + rec, axis=1))
    gb = B // PROJ_GROUPS
    ys = [jnp.dot(jnp.concatenate(rows[g * gb:(g + 1) * gb], axis=0), wout_ref[...], preferred_element_type=F32)
          for g in range(PROJ_GROUPS)]

    g2 = g2_ref[...]
    wr = wr_ref[...]
    wr_hi = wr.astype(BF16)
    wr_lo = (wr - wr_hi.astype(F32)).astype(BF16)
    wr_both = jnp.concatenate([wr_hi, wr_lo], axis=0)
    his, los = [], []
    for b in range(B):
        g1x = mod_ref[b:b + 1, 2 * D:3 * D]
        sh2 = mod_ref[b:b + 1, 3 * D:4 * D]
        gs2 = g2 * (1.0 + mod_ref[b:b + 1, 4 * D:5 * D])
        x1 = x_ref[b] + g1x * ys[b // gb][(b % gb) * TP:(b % gb + 1) * TP, :]
        x1_ref[b] = x1
        hx2 = _rms_mod(x1, gs2, sh2)
        for k in range(SUB_PER_TOK):
            hx2t_ref[b, pl.ds(k, TP, stride=SUB_PER_TOK), :] = hx2[:, k * LANES:(k + 1) * LANES]
        hx_hi = hx2.astype(BF16)
        his.append(hx_hi)
        los.append((hx2 - hx_hi.astype(F32)).astype(BF16))

    nt = (((1,), (1,)), ((), ()))
    p = lax.dot_general(wr_both, jnp.concatenate(his, axis=0), nt, preferred_element_type=F32)
    q = lax.dot_general(wr_hi, jnp.concatenate(los, axis=0), nt, preferred_element_type=F32)
    lg = p[0:E] + (q + p[E:2 * E])
    for b in range(B):
        lg_ref[b] = lg[:, b * TP:(b + 1) * TP]


def _mix_out(x, mod, cx, hf, sgu, gg, wg, ba, bi, lam, h0b, w_out_b, g2, wr_t):
    full = lambda shape: pl.BlockSpec(shape, lambda i: (0,) * len(shape))
    rev = lambda i: N_TT - 1 - i
    return pl.pallas_call(
        _mix_out_kernel,
        grid=(N_TT,),
        in_specs=[pl.BlockSpec((B, TP, D), lambda i: (0, rev(i), 0)),
                  full((2 * B, N_MOD * D)),
                  pl.BlockSpec((ROWS, DL), lambda i: (rev(i), 0)),
                  pl.BlockSpec((ROWS, DL), lambda i: (rev(i), 0)),
                  pl.BlockSpec((B, TP, DS), lambda i: (0, rev(i), 0)),
                  pl.BlockSpec((B, TP, DL), lambda i: (0, rev(i), 0)),
                  full((2, 2, DL // 2, DL)), full((2, DL)), full((2, DL)), full((2, DL)), full((B, DL)),
                  full((D, D)), full((1, D)), full((E, D))],
        out_specs=[pl.BlockSpec((B, TP, D), lambda i: (0, rev(i), 0)),
                   pl.BlockSpec((B, TP * SUB_PER_TOK, LANES), lambda i: (0, rev(i), 0)),
                   pl.BlockSpec((B, E, TP), lambda i: (0, 0, rev(i)))],
        out_shape=[jax.ShapeDtypeStruct((B, S, D), F32),
                   jax.ShapeDtypeStruct((B, S * SUB_PER_TOK, LANES), F32),
                   jax.ShapeDtypeStruct((B, E, S), F32)],
        scratch_shapes=[pltpu.VMEM((DL // LANES, ROWS, LANES), F32),
                        pltpu.VMEM((B, DL), F32)],
        compiler_params=_params(("arbitrary",), 56),
        name="mix_out",
    )(x, mod, cx, hf, sgu, gg, wg, ba, bi, lam, h0b, w_out_b, g2, wr_t)


NBLK = S // LANES
BIG = 1.0e9


def _bf16_parts(a):
    hi = a.astype(BF16).astype(F32)
    r1 = a - hi
    mid = r1.astype(BF16).astype(F32)
    lo = (r1 - mid).astype(BF16).astype(F32)
    return hi, mid, lo


def _route_kernel(lg_ref, idx_ref, gate_ref, aff_s, lci_s, offb_s, ahi_s, amid_s, alo_s, offi_s):
    for b in range(B):
        l = lg_ref[b]
        ex = jnp.exp(l - jnp.max(l, axis=0, keepdims=True))
        aff_s[b * E:(b + 1) * E, :] = ex / jnp.sum(ex, axis=0, keepdims=True)
    aff = aff_s[...]
    nrow = B * E

    def bisect(_, lohi):
        lo, hi = lohi
        mid = lo + ((hi - lo + 1) >> 1)
        cnt = jnp.sum(jnp.where(aff >= pltpu.bitcast(mid, F32), 1.0, 0.0), axis=1, keepdims=True)
        ge = cnt >= float(CAP)
        return jnp.where(ge, mid, lo), jnp.where(ge, hi, mid - 1)

    lo0 = jnp.zeros((nrow, 1), jnp.int32)
    hi0 = jnp.full((nrow, 1), 0x7F800000, jnp.int32)
    thr_bits, _ = lax.fori_loop(0, 31, bisect, (lo0, hi0))
    thr = pltpu.bitcast(thr_bits, F32)
    gt = aff > thr
    eq = aff == thr
    need = float(CAP) - jnp.sum(jnp.where(gt, 1.0, 0.0), axis=1, keepdims=True)

    qi = lax.broadcasted_iota(jnp.int32, (LANES, LANES), 0)
    ti = lax.broadcasted_iota(jnp.int32, (LANES, LANES), 1)
    tri = jnp.where(qi <= ti, 1.0, 0.0).astype(BF16)
    blocks = [slice(j * LANES, (j + 1) * LANES) for j in range(NBLK)]

    ties_before = jnp.zeros((nrow, 1), F32)
    sel = []
    for sl in blocks:
        eqb = jnp.where(eq[:, sl], 1.0, 0.0)
        incl = jnp.dot(eqb.astype(BF16), tri, preferred_element_type=F32) + ties_before
        sel.append(gt[:, sl] | (eq[:, sl] & ((incl - eqb) < need)))
        ties_before = incl[:, LANES - 1:LANES]

    lane_sq = lax.broadcasted_iota(jnp.int32, (nrow, LANES), 1)
    off = jnp.zeros((nrow, 1), F32)
    offi = jnp.full((nrow, LANES), BIG, F32)
    for j, sl in enumerate(blocks):
        rows = slice(j * nrow, (j + 1) * nrow)
        lci = jnp.dot(jnp.where(sel[j], 1.0, 0.0).astype(BF16), tri, preferred_element_type=F32)
        lci_s[rows, :] = lci
        offb_s[rows, :] = jnp.broadcast_to(off, (nrow, LANES))
        ahi_s[rows, :], amid_s[rows, :], alo_s[rows, :] = _bf16_parts(aff[:, sl])
        off = off + lci[:, LANES - 1:LANES]
        offi = jnp.where(lane_sq == j, off, offi)
    offi_s[...] = offi

    slot = lax.broadcasted_iota(jnp.int32, (CAP, LANES), 0).astype(F32)
    lane = lax.broadcasted_iota(jnp.int32, (CAP, LANES), 1)
    lane_f = lane.astype(F32)
    zpad = jnp.zeros((LANES - NBLK, 5 * LANES), BF16)

    def row_body(r, carry):
        idxm, gm = carry
        take = lambda ref: ref[pl.ds(r, NBLK, stride=nrow), :]
        table = jnp.concatenate([take(lci_s), take(offb_s), take(ahi_s), take(amid_s), take(alo_s)], axis=1)
        table = jnp.concatenate([table.astype(BF16), zpad], axis=0)
        blk = jnp.sum(jnp.where(offi_s[pl.ds(r, 1), :] <= slot, 1.0, 0.0), axis=1, keepdims=True)
        pick = jnp.where(lane_f == blk, 1.0, 0.0).astype(BF16)
        res = jnp.dot(pick, table, preferred_element_type=F32)
        rank1 = (slot + 1.0) - res[:, LANES:2 * LANES]
        tokl = jnp.sum(jnp.where(res[:, 0:LANES] < rank1, 1.0, 0.0), axis=1, keepdims=True)
        affs = (res[:, 2 * LANES:3 * LANES] + res[:, 3 * LANES:4 * LANES]) + res[:, 4 * LANES:5 * LANES]
        gv = jnp.sum(jnp.where(lane_f == tokl, affs, 0.0), axis=1, keepdims=True)
        iv = blk * float(LANES) + tokl
        put = lane == r
        return jnp.where(put, iv, idxm), jnp.where(put, gv, gm)

    z = jnp.zeros((CAP, LANES), F32)
    idxm, gm = lax.fori_loop(0, nrow, row_body, (z, z), unroll=8)
    idx_ref[...] = (idxm.T * float(SUB_PER_TOK)).astype(jnp.int32)
    gate_ref[...] = gm


def _route(logits_t):
    full = lambda shape: pl.BlockSpec(shape, lambda i: (0,) * len(shape))
    return pl.pallas_call(
        _route_kernel,
        grid=(1,),
        in_specs=[full((B, E, S))],
        out_specs=[full((B * E, CAP)), full((CAP, B * E))],
        out_shape=[jax.ShapeDtypeStruct((B * E, CAP), jnp.int32), jax.ShapeDtypeStruct((CAP, B * E), F32)],
        scratch_shapes=[pltpu.VMEM((B * E, S), F32)] + [pltpu.VMEM((NBLK * B * E, LANES), F32)] * 5
                       + [pltpu.VMEM((B * E, LANES), F32)],
        compiler_params=_params(("arbitrary",), 48),
        name="route",
    )(logits_t)


def _dispatch_kernel(idx_ref, h_ref, xg_ref, xt0, xt1):
    for e in range(E):
        xt = xt0 if e % 2 == 0 else xt1
        for s in range(CAP):
            src = pl.multiple_of(idx_ref[e * CAP + s], SUB_PER_TOK)
            xt[s * SUB_PER_TOK:(s + 1) * SUB_PER_TOK, :] = h_ref[0, pl.ds(src, SUB_PER_TOK), :]
        for k in range(SUB_PER_TOK):
            xg_ref[e, :, k * LANES:(k + 1) * LANES] = xt[pl.ds(k, CAP, stride=SUB_PER_TOK), :].astype(BF16)


def _dispatch(idx1, hx2t):
    return pl.pallas_call(
        _dispatch_kernel,
        grid=(B,),
        in_specs=[pl.BlockSpec((E * CAP,), lambda b: (b,), memory_space=pltpu.SMEM),
                  pl.BlockSpec((1, S * SUB_PER_TOK, LANES), lambda b: (b, 0, 0))],
        out_specs=pl.BlockSpec((E, CAP, D), lambda b: (0, b, 0)),
        out_shape=jax.ShapeDtypeStruct((E, B * CAP, D), BF16),
        scratch_shapes=[pltpu.VMEM((CAP * SUB_PER_TOK, LANES), F32), pltpu.VMEM((CAP * SUB_PER_TOK, LANES), F32)],
        compiler_params=_params(("arbitrary",), 40),
        name="dispatch",
    )(idx1, hx2t)


def _moe_weight_copies(w1_hbm, w3_hbm, w2_hbm, w1_b, w3_b, w2_b, sem, ee, cs, slot):
    return (pltpu.make_async_copy(w1_hbm.at[ee, :, pl.ds(cs, FN)], w1_b.at[slot], sem.at[0, slot]),
            pltpu.make_async_copy(w3_hbm.at[ee, :, pl.ds(cs, FN)], w3_b.at[slot], sem.at[1, slot]),
            pltpu.make_async_copy(w2_hbm.at[ee, pl.ds(cs, FN), :], w2_b.at[slot], sem.at[2, slot]))


def _moe_kernel(x_ref, gate_ref, w1_hbm, w3_hbm, w2_hbm, y_ref, hid_s, w2_s, w1_b, w3_b, w2_b, sem):
    f = pl.program_id(1)
    nf = FF // FN
    step = pl.program_id(0) * nf + f
    bufs = (w1_hbm, w3_hbm, w2_hbm, w1_b, w3_b, w2_b, sem)

    @pl.when(step == 0)
    def _():
        for s in range(W_BUF - 1):
            for cp in _moe_weight_copies(*bufs, s // nf, (s % nf) * FN, s):
                cp.start()

    nxt = step + (W_BUF - 1)

    @pl.when(nxt < E * nf)
    def _():
        cs_n = pl.multiple_of(lax.rem(nxt, nf) * FN, FN)
        for cp in _moe_weight_copies(*bufs, lax.div(nxt, nf), cs_n, lax.rem(nxt, W_BUF)):
            cp.start()

    slot = lax.rem(step, W_BUF)
    fs = pl.multiple_of(f * FN, FN)
    for cp in _moe_weight_copies(*bufs, pl.program_id(0), fs, slot):
        cp.wait()
    w1 = w1_b[slot].astype(BF16)
    w3 = w3_b[slot].astype(BF16)
    w2_s[pl.ds(fs, FN), :] = w2_b[slot].astype(BF16)
    for c in range(B * CAP // MC):
        rs = slice(c * MC, (c + 1) * MC)
        x = x_ref[0, rs, :]
        h1 = jnp.dot(x, w1, preferred_element_type=F32)
        h3 = jnp.dot(x, w3, preferred_element_type=F32)
        half = 0.5 * h1
        hid_s[f, rs, :] = ((half + half * jnp.tanh(half)) * h3).astype(BF16)

    @pl.when(f == FF // FN - 1)
    def _():
        gates = gate_ref[...]
        lane = lax.broadcasted_iota(jnp.int32, gates.shape, 1)
        e = pl.program_id(0)
        for c in range(B * CAP // MC):
            rs = slice(c * MC, (c + 1) * MC)
            hid = jnp.concatenate([hid_s[j, rs, :] for j in range(FF // FN)], axis=1)
            y = jnp.dot(hid, w2_s[...], preferred_element_type=F32)
            for bb in range(MC // CAP):
                b = c * (MC // CAP) + bb
                gcol = jnp.sum(jnp.where(lane == b * E + e, gates, 0.0), axis=1, keepdims=True)
                y_ref[0, b * CAP:(b + 1) * CAP, :] = (y[bb * CAP:(bb + 1) * CAP, :] * gcol).astype(BF16)


def _moe(xg, gate_cols, w1, w3, w2):
    return pl.pallas_call(
        _moe_kernel,
        grid=(E, FF // FN),
        in_specs=[pl.BlockSpec((1, B * CAP, D), lambda e, f: (e, 0, 0)),
                  pl.BlockSpec((CAP, B * E), lambda e, f: (0, 0)),
                  pl.BlockSpec(memory_space=pl.ANY),
                  pl.BlockSpec(memory_space=pl.ANY),
                  pl.BlockSpec(memory_space=pl.ANY)],
        out_specs=pl.BlockSpec((1, B * CAP, D), lambda e, f: (e, 0, 0)),
        out_shape=jax.ShapeDtypeStruct((E, B * CAP, D), BF16),
        scratch_shapes=[pltpu.VMEM((FF // FN, B * CAP, FN), BF16), pltpu.VMEM((FF, D), BF16),
                        pltpu.VMEM((W_BUF, D, FN), F32), pltpu.VMEM((W_BUF, D, FN), F32),
                        pltpu.VMEM((W_BUF, FN, D), F32), pltpu.SemaphoreType.DMA((3, W_BUF))],
        compiler_params=_params(("arbitrary", "arbitrary"), 58),
        name="moe",
    )(xg, gate_cols, w1, w3, w2)


def _combine_kernel(idx_ref, y_ref, x1_ref, mod_ref, fg_ref, out_ref, acc, yt0, yt1):
    b = pl.program_id(0)
    g = pl.program_id(1)

    @pl.when(g == 0)
    def _():
        acc[...] = jnp.zeros(acc.shape, F32)

    @pl.when(g < NG)
    def _():
        for el in range(EG):
            yt = yt0 if el % 2 == 0 else yt1
            for k in range(SUB_PER_TOK):
                yt[pl.ds(k, CAP, stride=SUB_PER_TOK), :] = y_ref[el, :, k * LANES:(k + 1) * LANES].astype(F32)
            for s0 in range(0, CAP, SCATTER_BATCH):
                dsts, news = [], []
                for j in range(SCATTER_BATCH):
                    dst = pl.multiple_of(idx_ref[el * CAP + s0 + j], SUB_PER_TOK)
                    src = (s0 + j) * SUB_PER_TOK
                    dsts.append(dst)
                    news.append(acc[pl.ds(dst, SUB_PER_TOK), :] + yt[src:src + SUB_PER_TOK, :])
                for dst, new in zip(dsts, news):
                    acc[pl.ds(dst, SUB_PER_TOK), :] = new

    @pl.when(g >= NG)
    def _():
        row0 = pl.multiple_of((g - NG) * (FIN_ROWS * SUB_PER_TOK), FIN_ROWS * SUB_PER_TOK)
        g2x = mod_ref[pl.ds(b, 1), 5 * D:6 * D]
        sq = jnp.zeros((FIN_ROWS, LANES), F32)
        for k in range(SUB_PER_TOK):
            ls = slice(k * LANES, (k + 1) * LANES)
            xo = x1_ref[0, :, ls] + g2x[:, ls] * acc[pl.ds(row0 + k, FIN_ROWS, stride=SUB_PER_TOK), :]
            out_ref[0, :, ls] = xo
            sq = sq + xo * xo
        inv = lax.rsqrt(jnp.sum(sq, axis=1, keepdims=True) * (1.0 / D) + EPS)
        out_ref[0] = (out_ref[0] * inv) * fg_ref[...]


def _combine(idx1, y, x1, mod, fg):
    fin = lambda b, g: (b, jnp.maximum(g - NG, 0), 0)
    grp = lambda g: jnp.minimum(g, NG - 1)
    return pl.pallas_call(
        _combine_kernel,
        grid=(B, NG + S // FIN_ROWS),
        in_specs=[pl.BlockSpec((EG * CAP,), lambda b, g: (b * NG + grp(g),), memory_space=pltpu.SMEM),
                  pl.BlockSpec((EG, CAP, D), lambda b, g: (grp(g), b, 0)),
                  pl.BlockSpec((1, FIN_ROWS, D), fin),
                  pl.BlockSpec((2 * B, N_MOD * D), lambda b, g: (0, 0)),
                  pl.BlockSpec((1, D), lambda b, g: (0, 0))],
        out_specs=pl.BlockSpec((1, FIN_ROWS, D), fin),
        out_shape=jax.ShapeDtypeStruct((B, S, D), F32),
        scratch_shapes=[pltpu.VMEM((S * SUB_PER_TOK, LANES), F32),
                        pltpu.VMEM((CAP * SUB_PER_TOK, LANES), F32), pltpu.VMEM((CAP * SUB_PER_TOK, LANES), F32)],
        compiler_params=_params(("arbitrary", "arbitrary"), 56),
        name="combine",
    )(idx1, y, x1, mod, fg)


def _pack_gate_weights(wa, wi):
    eye = jnp.eye(4, dtype=wa.dtype)

    def bdiag(w4):
        return jnp.einsum('hij,hg->higj', w4, eye).reshape(4 * LRU_HD, 4 * LRU_HD)

    dirs = []
    for d in range(2):
        halves = []
        for k in range(2):
            hs = slice(4 * k, 4 * (k + 1))
            halves.append(jnp.concatenate([bdiag(wa[d, hs]), bdiag(wi[d, hs])], axis=1))
        dirs.append(jnp.stack(halves))
    return (0.5 * jnp.stack(dirs)).astype(BF16)


def kernel(x, c, ctx, c_ctx, w_mod, b_mod, norm1_g, norm2_g, w_in, sgu_g, sgu_w, sgu_b, conv_w, conv_b,
           rg_wa, rg_ba, rg_wi, rg_bi, rg_lam, w_out, w_router, w1, w3, w2, final_g):
    assert x.shape == (B, S, D) and ctx.shape == (B, LC, D) and w_mod.shape[0] == 1

    cc = jnp.concatenate([c, c_ctx[None, :], jnp.zeros((B - 1, D), F32)], axis=0)
    mod = _modulation(cc, w_mod[0], b_mod[0][None, :])

    g1 = norm1_g[0][None, :]
    g2 = norm2_g[0][None, :]
    w_in_b = w_in[0].astype(BF16)
    w_in_x = w_in_b[:, 2 * DS:2 * DS + DL]
    wg = _pack_gate_weights(rg_wa[0], rg_wi[0])
    cw = conv_w[0]
    cb = conv_b[0][None, :]
    ba, bi, lam = 0.5 * rg_ba[0], 0.5 * rg_bi[0], rg_lam[0]
    sbias = jnp.repeat(sgu_b[0].T, HD, axis=1)

    h0f, h0b = _context_states(ctx, mod, g1, w_in_x, cw, cb, wg, ba, bi, lam)
    sgu, gg, cx, hf = _mix_in(x, mod, g1, w_in_b, sgu_g[0][None, :], sgu_w[0].astype(BF16), sbias, cw, cb,
                              wg, ba, bi, lam, h0f)
    x1, hx2t, logits_t = _mix_out(x, mod, cx, hf, sgu, gg, wg, ba, bi, lam, h0b, w_out[0].astype(BF16), g2,
                                  w_router[0].T)
    idx, gate = _route(logits_t)
    idx1 = idx.reshape(B * E * CAP)
    xg = _dispatch(idx1, hx2t)
    y = _moe(xg, gate, w1[0], w3[0], w2[0])
    return _combine(idx1, y, x1, mod, final_g[None, :])
```

```python
import jax
import jax.numpy as jnp
from jax import lax
from jax.experimental import pallas as pl
from jax.experimental.pallas import tpu as pltpu

F32 = jnp.float32
BF16 = jnp.bfloat16

D = 1024
B = 8
S = 2048
LC = 256
GRID_W = 64
DS = 512
NH = 4
HD = DS // NH
CHUNK = 128
DL = 512
LRU_HEADS = 8
LRU_HD = DL // LRU_HEADS
E = 16
CAP = 2 * S // E
FF = 2048
N_MOD = 6
EPS = 1e-6
RG_C = 8.0

SUBLANES = 8
LANES = 128
VMEM_LIMIT_V7X = 60000 * 1024

TP = CHUNK
ROWS = TP * B
N_TT = S // TP
SUB_PER_TOK = D // LANES
PROJ_GROUPS = 2
FN = 512
MC = 512
SCATTER_BATCH = 16
EG = 16
NG = E // EG
FIN_ROWS = 1024
PAD_F = SUBLANES
PAD_B = 2 * SUBLANES


def _params(sem, vmem_mb):
    return pltpu.CompilerParams(dimension_semantics=sem, vmem_limit_bytes=min(vmem_mb << 20, VMEM_LIMIT_V7X))


GELU_C1 = 0.7978845608028654
GELU_C2 = GELU_C1 * 0.044715


def _gelu(x):
    half = 0.5 * x
    return half + half * jnp.tanh(x * (GELU_C1 + GELU_C2 * (x * x)))


def _rms_mod(x, gs, sh):
    return (x * lax.rsqrt(jnp.mean(x * x, axis=-1, keepdims=True) + EPS)) * gs + sh


def _log_sigmoid(x):
    return -(jnp.maximum(-x, 0.0) + jnp.log1p(jnp.exp(-jnp.abs(x))))


def _conv_interleaved(xx, xi_ref, tp, period, cw, cb):
    rows = tp * B
    nslab = DL // LANES
    xi_ref[:, 0:PAD_F, :] = jnp.zeros((nslab, PAD_F, LANES), F32)
    xi_ref[:, PAD_F + rows:PAD_F + rows + PAD_B, :] = jnp.zeros((nslab, PAD_B, LANES), F32)
    for k in range(nslab):
        for b in range(B):
            xi_ref[k, pl.ds(PAD_F + b, tp, stride=B), :] = xx[b * tp:(b + 1) * tp, k * LANES:(k + 1) * LANES]
    pos = lax.broadcasted_iota(jnp.int32, (rows, LANES), 0) >> 3
    pm = pos & (period - 1)
    m0 = pm != 0
    m2 = pm != period - 1
    m3 = pm < period - 2
    outs = []
    for k in range(nslab):
        w = cw[:, k * LANES:(k + 1) * LANES]
        t0 = xi_ref[k, 0:rows, :]
        t1 = xi_ref[k, SUBLANES:SUBLANES + rows, :]
        t2 = xi_ref[k, 2 * SUBLANES:2 * SUBLANES + rows, :]
        t3 = xi_ref[k, 3 * SUBLANES:3 * SUBLANES + rows, :]
        acc = jnp.where(m0, t0, 0.0) * w[0:1] + t1 * w[1:2]
        acc = acc + jnp.where(m2, t2, 0.0) * w[2:3] + jnp.where(m3, t3, 0.0) * w[3:4]
        outs.append(acc + cb[:, k * LANES:(k + 1) * LANES])
    return jnp.concatenate(outs, axis=1)


def _rglru_coeffs(cx, wg_ref, d, ba_half, bi_half, lam):
    cxb = cx.astype(BF16)
    half = DL // 2
    a_parts, b_parts = [], []
    for k in range(2):
        sl = slice(half * k, half * (k + 1))
        zh = jnp.dot(cxb[:, sl], wg_ref[d, k], preferred_element_type=F32)
        t_r = jnp.tanh(zh[:, :half] + ba_half[:, sl])
        t_i = jnp.tanh(zh[:, half:] + bi_half[:, sl])
        ch = (0.5 * RG_C) * -_log_sigmoid(lam[:, sl])
        neg_log_a = ch + ch * t_r
        a = jnp.exp(-neg_log_a)
        one_minus_a2 = jnp.tanh(neg_log_a) * (a * a + 1.0)
        root = jnp.where(one_minus_a2 > 0.0, one_minus_a2 * lax.rsqrt(one_minus_a2), 0.0)
        xh = 0.5 * cx[:, sl]
        a_parts.append(a)
        b_parts.append(root * (xh + xh * t_i))
    return jnp.concatenate(a_parts, axis=1), jnp.concatenate(b_parts, axis=1)


def _scan(a, b, h_out_ref, h0, nsteps, reverse):
    h = h0
    for p in (range(nsteps - 1, -1, -1) if reverse else range(nsteps)):
        rs = slice(p * SUBLANES, (p + 1) * SUBLANES)
        h = a[rs, :] * h + b[rs, :]
        if h_out_ref is not None:
            h_out_ref[rs, :] = h
    return h


def _mod_kernel(c_ref, w_ref, b_ref, o_ref):
    c = c_ref[...]
    s = c * jax.nn.sigmoid(c)
    s_hi = s.astype(BF16)
    s_lo = (s - s_hi.astype(F32)).astype(BF16)
    w = w_ref[...]
    w_hi = w.astype(BF16)
    w_lo = (w - w_hi.astype(F32)).astype(BF16)
    nrow = s.shape[0]
    both = jnp.dot(jnp.concatenate([s_hi, s_lo], axis=0), w_hi, preferred_element_type=F32)
    o_ref[...] = (both[0:nrow] + (both[nrow:] + jnp.dot(s_hi, w_lo, preferred_element_type=F32))) + b_ref[...]


def _modulation(cc, w_mod, b_mod):
    nt = 4
    tn = N_MOD * D // nt
    return pl.pallas_call(
        _mod_kernel,
        grid=(nt,),
        in_specs=[pl.BlockSpec((2 * B, D), lambda i: (0, 0)),
                  pl.BlockSpec((D, tn), lambda i: (0, i)),
                  pl.BlockSpec((1, tn), lambda i: (0, i))],
        out_specs=pl.BlockSpec((2 * B, tn), lambda i: (0, i)),
        out_shape=jax.ShapeDtypeStruct((2 * B, N_MOD * D), F32),
        compiler_params=_params(("arbitrary",), 32),
        name="mod",
    )(cc, w_mod, b_mod)


def _ctx_kernel(ctx_ref, mod_ref, g1_ref, win_ref, cw_ref, cb_ref, wg_ref, ba_ref, bi_ref, lam_ref,
                hf_ref, hb_ref, xi_ref):
    sh = mod_ref[B:B + 1, 0:D]
    gs = g1_ref[...] * (1.0 + mod_ref[B:B + 1, D:2 * D])
    parts = []
    for b in range(B):
        hc = _rms_mod(ctx_ref[b], gs, sh).astype(BF16)
        parts.append(jnp.dot(hc, win_ref[...], preferred_element_type=F32))
    xx = jnp.concatenate(parts, axis=0)
    cx = _conv_interleaved(xx, xi_ref, LC, LC, cw_ref[...], cb_ref[...])
    for d in range(2):
        a, dr = _rglru_coeffs(cx, wg_ref, d, ba_ref[d:d + 1, :], bi_ref[d:d + 1, :], lam_ref[d:d + 1, :])
        h = _scan(a, dr, None, jnp.zeros((B, DL), F32), LC, reverse=(d == 1))
        if d == 0:
            hf_ref[...] = h
        else:
            hb_ref[...] = h


def _context_states(ctx, mod, g1, w_in_x, cw, cb, wg, ba, bi, lam):
    rows = LC * B
    full = lambda shape: pl.BlockSpec(shape, lambda i: (0,) * len(shape))
    return pl.pallas_call(
        _ctx_kernel,
        grid=(1,),
        in_specs=[full((B, LC, D)), full((2 * B, N_MOD * D)), full((1, D)), full((D, DL)), full((4, DL)),
                  full((1, DL)), full((2, 2, DL // 2, DL)), full((2, DL)), full((2, DL)), full((2, DL))],
        out_specs=[full((B, DL)), full((B, DL))],
        out_shape=[jax.ShapeDtypeStruct((B, DL), F32)] * 2,
        scratch_shapes=[pltpu.VMEM((DL // LANES, PAD_F + rows + PAD_B, LANES), F32)],
        compiler_params=_params(("arbitrary",), 56),
        name="ctx",
    )(ctx, mod, g1, w_in_x, cw, cb, wg, ba, bi, lam)


def _mix_in_kernel(x_ref, mod_ref, g1_ref, win_ref, sg_ref, sw_ref, sbias_ref, cw_ref, cb_ref, wg_ref,
                   ba_ref, bi_ref, lam_ref, h0_ref,
                   sgu_ref, gg_ref, cx_ref, hf_ref,
                   hbuf, xi_ref, carry):
    @pl.when(pl.program_id(0) == 0)
    def _():
        carry[...] = h0_ref[...]

    g1 = g1_ref[...]
    for b in range(B):
        sh = mod_ref[b:b + 1, 0:D]
        gs = g1 * (1.0 + mod_ref[b:b + 1, D:2 * D])
        hbuf[b * TP:(b + 1) * TP, :] = _rms_mod(x_ref[b], gs, sh).astype(BF16)
    z = jnp.dot(hbuf[...], win_ref[...], preferred_element_type=F32)
    ug = _gelu(z[:, 0:DS])
    vg = _gelu(z[:, DS:2 * DS])
    xx = z[:, 2 * DS:2 * DS + DL]
    gg = _gelu(z[:, 2 * DS + DL:])
    for b in range(B):
        gg_ref[b] = gg[b * TP:(b + 1) * TP, :].astype(BF16)

    for h in range(NH):
        hs = slice(h * HD, (h + 1) * HD)
        vh = vg[:, hs]
        vn = (vh * lax.rsqrt(jnp.mean(vh * vh, axis=-1, keepdims=True) + EPS)) * sg_ref[:, hs]
        vnb = vn.astype(BF16)
        for b in range(B):
            rs = slice(b * TP, (b + 1) * TP)
            s = jnp.dot(sw_ref[h], vnb[rs], preferred_element_type=F32) + sbias_ref[:, hs]
            sgu_ref[b, :, hs] = (ug[rs, hs] * s).astype(BF16)

    cx = _conv_interleaved(xx, xi_ref, TP, GRID_W, cw_ref[...], cb_ref[...])
    cx_ref[...] = cx
    a, dr = _rglru_coeffs(cx, wg_ref, 0, ba_ref[0:1, :], bi_ref[0:1, :], lam_ref[0:1, :])
    carry[...] = _scan(a, dr, hf_ref, carry[...], TP, reverse=False)


def _mix_in(x, mod, g1, w_in_b, sg, sw_b, sbias, cw, cb, wg, ba, bi, lam, h0f):
    full = lambda shape: pl.BlockSpec(shape, lambda i: (0,) * len(shape))
    return pl.pallas_call(
        _mix_in_kernel,
        grid=(N_TT,),
        in_specs=[pl.BlockSpec((B, TP, D), lambda i: (0, i, 0)),
                  full((2 * B, N_MOD * D)), full((1, D)), full((D, 2 * DS + 2 * DL)), full((1, DS)),
                  full((NH, CHUNK, CHUNK)), full((CHUNK, DS)), full((4, DL)), full((1, DL)),
                  full((2, 2, DL // 2, DL)), full((2, DL)), full((2, DL)), full((2, DL)), full((B, DL))],
        out_specs=[pl.BlockSpec((B, TP, DS), lambda i: (0, i, 0)),
                   pl.BlockSpec((B, TP, DL), lambda i: (0, i, 0)),
                   pl.BlockSpec((ROWS, DL), lambda i: (i, 0)),
                   pl.BlockSpec((ROWS, DL), lambda i: (i, 0))],
        out_shape=[jax.ShapeDtypeStruct((B, S, DS), BF16), jax.ShapeDtypeStruct((B, S, DL), BF16),
                   jax.ShapeDtypeStruct((S * B, DL), F32), jax.ShapeDtypeStruct((S * B, DL), F32)],
        scratch_shapes=[pltpu.VMEM((ROWS, D), BF16),
                        pltpu.VMEM((DL // LANES, PAD_F + ROWS + PAD_B, LANES), F32),
                        pltpu.VMEM((B, DL), F32)],
        compiler_params=_params(("arbitrary",), 56),
        name="mix_in",
    )(x, mod, g1, w_in_b, sg, sw_b, sbias, cw, cb, wg, ba, bi, lam, h0f)


def _mix_out_kernel(x_ref, mod_ref, cx_ref, hf_ref, sgu_ref, gg_ref, wg_ref, ba_ref, bi_ref, lam_ref, h0_ref,
                    wout_ref, g2_ref, wr_ref,
                    x1_ref, hx2t_ref, lg_ref,
                    hs_ref, carry):
    @pl.when(pl.program_id(0) == 0)
    def _():
        carry[...] = h0_ref[...]

    a, dr = _rglru_coeffs(cx_ref[...], wg_ref, 1, ba_ref[1:2, :], bi_ref[1:2, :], lam_ref[1:2, :])
    nslab = DL // LANES
    h = carry[...]
    for p in range(TP - 1, -1, -1):
        rs = slice(p * SUBLANES, (p + 1) * SUBLANES)
        h = a[rs, :] * h + dr[rs, :]
        hsum = hf_ref[rs, :] + h
        for k in range(nslab):
            hs_ref[k, rs, :] = hsum[:, k * LANES:(k + 1) * LANES]
    carry[...] = h
    rows = []
    for b in range(B):
        rec = [(gg_ref[b, :, k * LANES:(k + 1) * LANES].astype(F32)
                * hs_ref[k, pl.ds(b, TP, stride=B), :]).astype(BF16) for k in range(nslab)]
        rows.append(jnp.concatenate([sgu_ref[b]] + rec, axis=1))
    gb = B // PROJ_GROUPS
    ys = [jnp.dot(jnp.concatenate(rows[g * gb:(g + 1) * gb], axis=0), wout_ref[...], preferred_element_type=F32)
          for g in range(PROJ_GROUPS)]

    g2 = g2_ref[...]
    wr = wr_ref[...]
    wr_hi = wr.astype(BF16)
    wr_lo = (wr - wr_hi.astype(F32)).astype(BF16)
    wr_both = jnp.concatenate([wr_hi, wr_lo], axis=0)
    his, los = [], []
    for b in range(B):
        g1x = mod_ref[b:b + 1, 2 * D:3 * D]
        sh2 = mod_ref[b:b + 1, 3 * D:4 * D]
        gs2 = g2 * (1.0 + mod_ref[b:b + 1, 4 * D:5 * D])
        x1 = x_ref[b] + g1x * ys[b // gb][(b % gb) * TP:(b % gb + 1) * TP, :]
        x1_ref[b] = x1
        hx2 = _rms_mod(x1, gs2, sh2)
        for k in range(SUB_PER_TOK):
            hx2t_ref[b, pl.ds(k, TP, stride=SUB_PER_TOK), :] = hx2[:, k * LANES:(k + 1) * LANES]
        hx_hi = hx2.astype(BF16)
        his.append(hx_hi)
        los.append((hx2 - hx_hi.astype(F32)).astype(BF16))

    nt = (((1,), (1,)), ((), ()))
    p = lax.dot_general(wr_both, jnp.concatenate(his, axis=0), nt, preferred_element_type=F32)
    q = lax.dot_general(wr_hi, jnp.concatenate(los, axis=0), nt, preferred_element_type=F32)
    lg = p[0:E] + (q + p[E:2 * E])
    for b in range(B):
        lg_ref[b] = lg[:, b * TP:(b + 1) * TP]


def _mix_out(x, mod, cx, hf, sgu, gg, wg, ba, bi, lam, h0b, w_out_b, g2, wr_t):
    full = lambda shape: pl.BlockSpec(shape, lambda i: (0,) * len(shape))
    rev = lambda i: N_TT - 1 - i
    return pl.pallas_call(
        _mix_out_kernel,
        grid=(N_TT,),
        in_specs=[pl.BlockSpec((B, TP, D), lambda i: (0, rev(i), 0)),
                  full((2 * B, N_MOD * D)),
                  pl.BlockSpec((ROWS, DL), lambda i: (rev(i), 0)),
                  pl.BlockSpec((ROWS, DL), lambda i: (rev(i), 0)),
                  pl.BlockSpec((B, TP, DS), lambda i: (0, rev(i), 0)),
                  pl.BlockSpec((B, TP, DL), lambda i: (0, rev(i), 0)),
                  full((2, 2, DL // 2, DL)), full((2, DL)), full((2, DL)), full((2, DL)), full((B, DL)),
                  full((D, D)), full((1, D)), full((E, D))],
        out_specs=[pl.BlockSpec((B, TP, D), lambda i: (0, rev(i), 0)),
                   pl.BlockSpec((B, TP * SUB_PER_TOK, LANES), lambda i: (0, rev(i), 0)),
                   pl.BlockSpec((B, E, TP), lambda i: (0, 0, rev(i)))],
        out_shape=[jax.ShapeDtypeStruct((B, S, D), F32),
                   jax.ShapeDtypeStruct((B, S * SUB_PER_TOK, LANES), F32),
                   jax.ShapeDtypeStruct((B, E, S), F32)],
        scratch_shapes=[pltpu.VMEM((DL // LANES, ROWS, LANES), F32),
                        pltpu.VMEM((B, DL), F32)],
        compiler_params=_params(("arbitrary",), 56),
        name="mix_out",
    )(x, mod, cx, hf, sgu, gg, wg, ba, bi, lam, h0b, w_out_b, g2, wr_t)


NBLK = S // LANES
BIG = 1.0e9


def _bf16_parts(a):
    hi = a.astype(BF16).astype(F32)
    r1 = a - hi
    mid = r1.astype(BF16).astype(F32)
    lo = (r1 - mid).astype(BF16).astype(F32)
    return hi, mid, lo


def _route_kernel(lg_ref, idx_ref, gate_ref, aff_s, lci_s, offb_s, ahi_s, amid_s, alo_s, offi_s):
    for b in range(B):
        l = lg_ref[b]
        ex = jnp.exp(l - jnp.max(l, axis=0, keepdims=True))
        aff_s[b * E:(b + 1) * E, :] = ex / jnp.sum(ex, axis=0, keepdims=True)
    aff = aff_s[...]
    nrow = B * E

    def bisect(_, lohi):
        lo, hi = lohi
        mid = lo + ((hi - lo + 1) >> 1)
        cnt = jnp.sum(jnp.where(aff >= pltpu.bitcast(mid, F32), 1.0, 0.0), axis=1, keepdims=True)
        ge = cnt >= float(CAP)
        return jnp.where(ge, mid, lo), jnp.where(ge, hi, mid - 1)

    lo0 = jnp.zeros((nrow, 1), jnp.int32)
    hi0 = jnp.full((nrow, 1), 0x7F800000, jnp.int32)
    thr_bits, _ = lax.fori_loop(0, 31, bisect, (lo0, hi0))
    thr = pltpu.bitcast(thr_bits, F32)
    gt = aff > thr
    eq = aff == thr
    need = float(CAP) - jnp.sum(jnp.where(gt, 1.0, 0.0), axis=1, keepdims=True)

    qi = lax.broadcasted_iota(jnp.int32, (LANES, LANES), 0)
    ti = lax.broadcasted_iota(jnp.int32, (LANES, LANES), 1)
    tri = jnp.where(qi <= ti, 1.0, 0.0).astype(BF16)
    blocks = [slice(j * LANES, (j + 1) * LANES) for j in range(NBLK)]

    ties_before = jnp.zeros((nrow, 1), F32)
    sel = []
    for sl in blocks:
        eqb = jnp.where(eq[:, sl], 1.0, 0.0)
        incl = jnp.dot(eqb.astype(BF16), tri, preferred_element_type=F32) + ties_before
        sel.append(gt[:, sl] | (eq[:, sl] & ((incl - eqb) < need)))
        ties_before = incl[:, LANES - 1:LANES]

    lane_sq = lax.broadcasted_iota(jnp.int32, (nrow, LANES), 1)
    off = jnp.zeros((nrow, 1), F32)
    offi = jnp.full((nrow, LANES), BIG, F32)
    for j, sl in enumerate(blocks):
        rows = slice(j * nrow, (j + 1) * nrow)
        lci = jnp.dot(jnp.where(sel[j], 1.0, 0.0).astype(BF16), tri, preferred_element_type=F32)
        lci_s[rows, :] = lci
        offb_s[rows, :] = jnp.broadcast_to(off, (nrow, LANES))
        ahi_s[rows, :], amid_s[rows, :], alo_s[rows, :] = _bf16_parts(aff[:, sl])
        off = off + lci[:, LANES - 1:LANES]
        offi = jnp.where(lane_sq == j, off, offi)
    offi_s[...] = offi

    slot = lax.broadcasted_iota(jnp.int32, (CAP, LANES), 0).astype(F32)
    lane = lax.broadcasted_iota(jnp.int32, (CAP, LANES), 1)
    lane_f = lane.astype(F32)
    zpad = jnp.zeros((LANES - NBLK, 5 * LANES), BF16)

    def row_body(r, carry):
        idxm, gm = carry
        take = lambda ref: ref[pl.ds(r, NBLK, stride=nrow), :]
        table = jnp.concatenate([take(lci_s), take(offb_s), take(ahi_s), take(amid_s), take(alo_s)], axis=1)
        table = jnp.concatenate([table.astype(BF16), zpad], axis=0)
        blk = jnp.sum(jnp.where(offi_s[pl.ds(r, 1), :] <= slot, 1.0, 0.0), axis=1, keepdims=True)
        pick = jnp.where(lane_f == blk, 1.0, 0.0).astype(BF16)
        res = jnp.dot(pick, table, preferred_element_type=F32)
        rank1 = (slot + 1.0) - res[:, LANES:2 * LANES]
        tokl = jnp.sum(jnp.where(res[:, 0:LANES] < rank1, 1.0, 0.0), axis=1, keepdims=True)
        affs = (res[:, 2 * LANES:3 * LANES] + res[:, 3 * LANES:4 * LANES]) + res[:, 4 * LANES:5 * LANES]
        gv = jnp.sum(jnp.where(lane_f == tokl, affs, 0.0), axis=1, keepdims=True)
        iv = blk * float(LANES) + tokl
        put = lane == r
        return jnp.where(put, iv, idxm), jnp.where(put, gv, gm)

    z = jnp.zeros((CAP, LANES), F32)
    idxm, gm = lax.fori_loop(0, nrow, row_body, (z, z), unroll=8)
    idx_ref[...] = (idxm.T * float(SUB_PER_TOK)).astype(jnp.int32)
    gate_ref[...] = gm


def _route(logits_t):
    full = lambda shape: pl.BlockSpec(shape, lambda i: (0,) * len(shape))
    return pl.pallas_call(
        _route_kernel,
        grid=(1,),
        in_specs=[full((B, E, S))],
        out_specs=[full((B * E, CAP)), full((CAP, B * E))],
        out_shape=[jax.ShapeDtypeStruct((B * E, CAP), jnp.int32), jax.ShapeDtypeStruct((CAP, B * E), F32)],
        scratch_shapes=[pltpu.VMEM((B * E, S), F32)] + [pltpu.VMEM((NBLK * B * E, LANES), F32)] * 5
                       + [pltpu.VMEM((B * E, LANES), F32)],
        compiler_params=_params(("arbitrary",), 48),
        name="route",
    )(logits_t)


def _dispatch_kernel(idx_ref, h_ref, xg_ref, xt0, xt1):
    for e in range(E):
        xt = xt0 if e % 2 == 0 else xt1
        for s in range(CAP):
            src = pl.multiple_of(idx_ref[e * CAP + s], SUB_PER_TOK)
            xt[s * SUB_PER_TOK:(s + 1) * SUB_PER_TOK, :] = h_ref[0, pl.ds(src, SUB_PER_TOK), :]
        for k in range(SUB_PER_TOK):
            xg_ref[e, :, k * LANES:(k + 1) * LANES] = xt[pl.ds(k, CAP, stride=SUB_PER_TOK), :].astype(BF16)


def _dispatch(idx1, hx2t):
    return pl.pallas_call(
        _dispatch_kernel,
        grid=(B,),
        in_specs=[pl.BlockSpec((E * CAP,), lambda b: (b,), memory_space=pltpu.SMEM),
                  pl.BlockSpec((1, S * SUB_PER_TOK, LANES), lambda b: (b, 0, 0))],
        out_specs=pl.BlockSpec((E, CAP, D), lambda b: (0, b, 0)),
        out_shape=jax.ShapeDtypeStruct((E, B * CAP, D), BF16),
        scratch_shapes=[pltpu.VMEM((CAP * SUB_PER_TOK, LANES), F32), pltpu.VMEM((CAP * SUB_PER_TOK, LANES), F32)],
        compiler_params=_params(("arbitrary",), 40),
        name="dispatch",
    )(idx1, hx2t)


def _moe_kernel(x_ref, gate_ref, w1_ref, w3_ref, w2_ref, y_ref, hid_s, w2_s):
    f = pl.program_id(1)
    w1 = w1_ref[0].astype(BF16)
    w3 = w3_ref[0].astype(BF16)
    fs = pl.multiple_of(f * FN, FN)
    w2_s[pl.ds(fs, FN), :] = w2_ref[0].astype(BF16)
    for c in range(B * CAP // MC):
        rs = slice(c * MC, (c + 1) * MC)
        x = x_ref[0, rs, :]
        h1 = jnp.dot(x, w1, preferred_element_type=F32)
        h3 = jnp.dot(x, w3, preferred_element_type=F32)
        half = 0.5 * h1
        hid_s[f, rs, :] = ((half + half * jnp.tanh(half)) * h3).astype(BF16)

    @pl.when(f == FF // FN - 1)
    def _():
        gates = gate_ref[...]
        lane = lax.broadcasted_iota(jnp.int32, gates.shape, 1)
        e = pl.program_id(0)
        for c in range(B * CAP // MC):
            rs = slice(c * MC, (c + 1) * MC)
            hid = jnp.concatenate([hid_s[j, rs, :] for j in range(FF // FN)], axis=1)
            y = jnp.dot(hid, w2_s[...], preferred_element_type=F32)
            for bb in range(MC // CAP):
                b = c * (MC // CAP) + bb
                gcol = jnp.sum(jnp.where(lane == b * E + e, gates, 0.0), axis=1, keepdims=True)
                y_ref[0, b * CAP:(b + 1) * CAP, :] = (y[bb * CAP:(bb + 1) * CAP, :] * gcol).astype(BF16)


def _moe(xg, gate_cols, w1, w3, w2):
    return pl.pallas_call(
        _moe_kernel,
        grid=(E, FF // FN),
        in_specs=[pl.BlockSpec((1, B * CAP, D), lambda e, f: (e, 0, 0)),
                  pl.BlockSpec((CAP, B * E), lambda e, f: (0, 0)),
                  pl.BlockSpec((1, D, FN), lambda e, f: (e, 0, f)),
                  pl.BlockSpec((1, D, FN), lambda e, f: (e, 0, f)),
                  pl.BlockSpec((1, FN, D), lambda e, f: (e, f, 0))],
        out_specs=pl.BlockSpec((1, B * CAP, D), lambda e, f: (e, 0, 0)),
        out_shape=jax.ShapeDtypeStruct((E, B * CAP, D), BF16),
        scratch_shapes=[pltpu.VMEM((FF // FN, B * CAP, FN), BF16), pltpu.VMEM((FF, D), BF16)],
        compiler_params=_params(("arbitrary", "arbitrary"), 56),
        name="moe",
    )(xg, gate_cols, w1, w3, w2)


def _combine_kernel(idx_ref, y_ref, x1_ref, mod_ref, fg_ref, out_ref, acc, yt0, yt1):
    b = pl.program_id(0)
    g = pl.program_id(1)

    @pl.when(g == 0)
    def _():
        acc[...] = jnp.zeros(acc.shape, F32)

    @pl.when(g < NG)
    def _():
        for el in range(EG):
            yt = yt0 if el % 2 == 0 else yt1
            for k in range(SUB_PER_TOK):
                yt[pl.ds(k, CAP, stride=SUB_PER_TOK), :] = y_ref[el, :, k * LANES:(k + 1) * LANES].astype(F32)
            for s0 in range(0, CAP, SCATTER_BATCH):
                dsts, news = [], []
                for j in range(SCATTER_BATCH):
                    dst = pl.multiple_of(idx_ref[el * CAP + s0 + j], SUB_PER_TOK)
                    src = (s0 + j) * SUB_PER_TOK
                    dsts.append(dst)
                    news.append(acc[pl.ds(dst, SUB_PER_TOK), :] + yt[src:src + SUB_PER_TOK, :])
                for dst, new in zip(dsts, news):
                    acc[pl.ds(dst, SUB_PER_TOK), :] = new

    @pl.when(g >= NG)
    def _():
        row0 = pl.multiple_of((g - NG) * (FIN_ROWS * SUB_PER_TOK), FIN_ROWS * SUB_PER_TOK)
        g2x = mod_ref[pl.ds(b, 1), 5 * D:6 * D]
        sq = jnp.zeros((FIN_ROWS, LANES), F32)
        for k in range(SUB_PER_TOK):
            ls = slice(k * LANES, (k + 1) * LANES)
            xo = x1_ref[0, :, ls] + g2x[:, ls] * acc[pl.ds(row0 + k, FIN_ROWS, stride=SUB_PER_TOK), :]
            out_ref[0, :, ls] = xo
            sq = sq + xo * xo
        inv = lax.rsqrt(jnp.sum(sq, axis=1, keepdims=True) * (1.0 / D) + EPS)
        out_ref[0] = (out_ref[0] * inv) * fg_ref[...]


def _combine(idx1, y, x1, mod, fg):
    fin = lambda b, g: (b, jnp.maximum(g - NG, 0), 0)
    grp = lambda g: jnp.minimum(g, NG - 1)
    return pl.pallas_call(
        _combine_kernel,
        grid=(B, NG + S // FIN_ROWS),
        in_specs=[pl.BlockSpec((EG * CAP,), lambda b, g: (b * NG + grp(g),), memory_space=pltpu.SMEM),
                  pl.BlockSpec((EG, CAP, D), lambda b, g: (grp(g), b, 0)),
                  pl.BlockSpec((1, FIN_ROWS, D), fin),
                  pl.BlockSpec((2 * B, N_MOD * D), lambda b, g: (0, 0)),
                  pl.BlockSpec((1, D), lambda b, g: (0, 0))],
        out_specs=pl.BlockSpec((1, FIN_ROWS, D), fin),
        out_shape=jax.ShapeDtypeStruct((B, S, D), F32),
        scratch_shapes=[pltpu.VMEM((S * SUB_PER_TOK, LANES), F32),
                        pltpu.VMEM((CAP * SUB_PER_TOK, LANES), F32), pltpu.VMEM((CAP * SUB_PER_TOK, LANES), F32)],
        compiler_params=_params(("arbitrary", "arbitrary"), 56),
        name="combine",
    )(idx1, y, x1, mod, fg)


def _pack_gate_weights(wa, wi):
    eye = jnp.eye(4, dtype=wa.dtype)

    def bdiag(w4):
        return jnp.einsum('hij,hg->higj', w4, eye).reshape(4 * LRU_HD, 4 * LRU_HD)

    dirs = []
    for d in range(2):
        halves = []
        for k in range(2):
            hs = slice(4 * k, 4 * (k + 1))
            halves.append(jnp.concatenate([bdiag(wa[d, hs]), bdiag(wi[d, hs])], axis=1))
        dirs.append(jnp.stack(halves))
    return (0.5 * jnp.stack(dirs)).astype(BF16)


def kernel(x, c, ctx, c_ctx, w_mod, b_mod, norm1_g, norm2_g, w_in, sgu_g, sgu_w, sgu_b, conv_w, conv_b,
           rg_wa, rg_ba, rg_wi, rg_bi, rg_lam, w_out, w_router, w1, w3, w2, final_g):
    assert x.shape == (B, S, D) and ctx.shape == (B, LC, D) and w_mod.shape[0] == 1

    cc = jnp.concatenate([c, c_ctx[None, :], jnp.zeros((B - 1, D), F32)], axis=0)
    mod = _modulation(cc, w_mod[0], b_mod[0][None, :])

    g1 = norm1_g[0][None, :]
    g2 = norm2_g[0][None, :]
    w_in_b = w_in[0].astype(BF16)
    w_in_x = w_in_b[:, 2 * DS:2 * DS + DL]
    wg = _pack_gate_weights(rg_wa[0], rg_wi[0])
    cw = conv_w[0]
    cb = conv_b[0][None, :]
    ba, bi, lam = 0.5 * rg_ba[0], 0.5 * rg_bi[0], rg_lam[0]
    sbias = jnp.repeat(sgu_b[0].T, HD, axis=1)

    h0f, h0b = _context_states(ctx, mod, g1, w_in_x, cw, cb, wg, ba, bi, lam)
    sgu, gg, cx, hf = _mix_in(x, mod, g1, w_in_b, sgu_g[0][None, :], sgu_w[0].astype(BF16), sbias, cw, cb,
                              wg, ba, bi, lam, h0f)
    x1, hx2t, logits_t = _mix_out(x, mod, cx, hf, sgu, gg, wg, ba, bi, lam, h0b, w_out[0].astype(BF16), g2,
                                  w_router[0].T)
    idx, gate = _route(logits_t)
    idx1 = idx.reshape(B * E * CAP)
    xg = _dispatch(idx1, hx2t)
    y = _moe(xg, gate, w1[0], w3[0], w2[0])
    return _combine(idx1, y, x1, mod, final_g[None, :])
```
